```python
import math
import jax, jax.numpy as jnp
from jax import lax
import numpy as np

D_MODEL = 2048
BATCH = 4
SEQ = 4096
DEPTH = 2

N_MIXERS = 2
EPS = 1e-5

SSM_EXPAND = 2
D_INNER = SSM_EXPAND * D_MODEL
SSM_HEAD_DIM = 64
SSM_HEADS = D_INNER // SSM_HEAD_DIM
SSM_GROUPS = 8
SSM_HEADS_PER_GROUP = SSM_HEADS // SSM_GROUPS
SSM_STATE = 128
CONV_WIDTH = 4
CHUNK = 128
D_BC = SSM_GROUPS * SSM_STATE
D_XBC = D_INNER + 2 * D_BC
D_IN_SSM = 2 * D_INNER + 2 * D_BC + SSM_HEADS

ATT_HEADS = 16
ATT_HEAD_DIM = 64
ATT_V_DIM = 2 * ATT_HEAD_DIM
ATT_WIDTH = ATT_HEADS * ATT_V_DIM
Q_BLOCK = 128

D_FF = 5632
N_EXPERTS = 8
TOP_K = 2

kernel_name = "hybrid_mamba2_diffattn_moe"


def rms_norm(x, g, eps=EPS):
    xf = x.astype(jnp.float32)
    y = xf * lax.rsqrt(jnp.mean(xf * xf, axis=-1, keepdims=True) + eps)
    return (y * g.astype(jnp.float32)).astype(x.dtype)


def swiglu(x, w_gate, w_up, w_down):
    return (jax.nn.silu(x @ w_gate) * (x @ w_up)) @ w_down


def causal_dwconv(x, w, b):
    y = lax.conv_general_dilated(
        x, w[:, None, :].astype(x.dtype), window_strides=(1,),
        padding=[(CONV_WIDTH - 1, 0)], dimension_numbers=("NWC", "WIO", "NWC"),
        feature_group_count=x.shape[-1])
    return y + b.astype(x.dtype)


def ssd_chunked(xh, dt, A, Bm, Cm):
    b, s, g, j, p = xh.shape
    n = Bm.shape[-1]
    c = s // CHUNK
    x = xh.reshape(b, c, CHUNK, g, j, p).astype(jnp.float32)
    dtc = dt.reshape(b, c, CHUNK, g, j)
    Bc = Bm.reshape(b, c, CHUNK, g, n).astype(jnp.float32)
    Cc = Cm.reshape(b, c, CHUNK, g, n).astype(jnp.float32)
    xdt = x * dtc[..., None]
    a_cs = jnp.cumsum(dtc * A, axis=2)
    seg = a_cs[:, :, :, None] - a_cs[:, :, None, :]
    causal = jnp.tril(jnp.ones((CHUNK, CHUNK), dtype=bool))[:, :, None, None]
    decay = jnp.exp(jnp.where(causal, seg, -jnp.inf))
    cb = jnp.einsum("bclgn,bcsgn->bclsg", Cc, Bc)
    y_diag = jnp.einsum("bclsgj,bcsgjp->bclgjp", cb[..., None] * decay, xdt)
    decay_to_end = jnp.exp(a_cs[:, :, -1:] - a_cs)
    states = jnp.einsum("bclgn,bclgj,bclgjp->bcgjpn", Bc, decay_to_end, xdt)
    chunk_decay = jnp.exp(a_cs[:, :, -1])

    def step(h, inp):
        st, dec = inp
        return h * dec[..., None, None] + st, h

    h0 = jnp.zeros((b, g, j, p, n), jnp.float32)
    _, h_in = lax.scan(step, h0, (jnp.moveaxis(states, 1, 0), jnp.moveaxis(chunk_decay, 1, 0)))
    h_in = jnp.moveaxis(h_in, 0, 1)
    y_off = jnp.einsum("bclgn,bcgjpn,bclgj->bclgjp", Cc, h_in, jnp.exp(a_cs))
    return (y_diag + y_off).reshape(b, s, g, j, p)


def mamba2_mixer(h, in_proj, conv_w, conv_b, dt_bias, A_log, D_skip, out_norm, out_proj):
    b, s, _ = h.shape
    zxbcdt = h @ in_proj
    z, xbc, dt = jnp.split(zxbcdt, [D_INNER, D_INNER + D_XBC], axis=-1)
    xbc = jax.nn.silu(causal_dwconv(xbc, conv_w, conv_b))
    xs, Bm, Cm = jnp.split(xbc, [D_INNER, D_INNER + D_BC], axis=-1)
    dt = jax.nn.softplus(dt.astype(jnp.float32) + dt_bias.astype(jnp.float32))
    A = -jnp.exp(A_log.astype(jnp.float32))
    g, jh, p = SSM_GROUPS, SSM_HEADS_PER_GROUP, SSM_HEAD_DIM
    xh = xs.reshape(b, s, g, jh, p)
    y = ssd_chunked(xh, dt.reshape(b, s, g, jh), A.reshape(g, jh),
                    Bm.reshape(b, s, g, SSM_STATE), Cm.reshape(b, s, g, SSM_STATE))
    y = y + D_skip.astype(jnp.float32).reshape(g, jh)[..., None] * xh.astype(jnp.float32)
    y = y.astype(h.dtype).reshape(b, s, D_INNER) * jax.nn.silu(z)
    y = rms_norm(y.reshape(b, s, g, D_INNER // g), out_norm.reshape(g, D_INNER // g))
    return y.reshape(b, s, D_INNER) @ out_proj


def diff_attention(h, w_qkv, q_norm, k_norm, lq1, lk1, lq2, lk2, sub_norm, w_o, lambda_init):
    b, s, _ = h.shape
    q, k, v = jnp.split(h @ w_qkv, 3, axis=-1)
    q = rms_norm(q.reshape(b, s, ATT_HEADS, 2, ATT_HEAD_DIM), q_norm) * (ATT_HEAD_DIM ** -0.5)
    k = rms_norm(k.reshape(b, s, ATT_HEADS, 2, ATT_HEAD_DIM), k_norm)
    v = v.reshape(b, s, ATT_HEADS, ATT_V_DIM)
    f32 = jnp.float32
    lam = (jnp.exp(jnp.sum(lq1.astype(f32) * lk1.astype(f32)))
           - jnp.exp(jnp.sum(lq2.astype(f32) * lk2.astype(f32))) + lambda_init)
    outs = []
    for blk in range(s // Q_BLOCK):
        q0 = blk * Q_BLOCK
        kv_len = q0 + Q_BLOCK
        qb = q[:, q0:kv_len]
        kb = k[:, :kv_len]
        vb = v[:, :kv_len]
        sc = jnp.einsum("bqhcd,bkhcd->bhcqk", qb, kb).astype(f32)
        mask = (q0 + jnp.arange(Q_BLOCK))[:, None] >= jnp.arange(kv_len)[None, :]
        pr = jax.nn.softmax(jnp.where(mask, sc, -jnp.inf), axis=-1)
        wts = pr[:, :, 0] - lam * pr[:, :, 1]
        outs.append(jnp.einsum("bhqk,bkhe->bqhe", wts.astype(vb.dtype), vb))
    o = jnp.concatenate(outs, axis=1)
    o = rms_norm(o, sub_norm) * (1.0 - lambda_init)
    return o.reshape(b, s, ATT_WIDTH) @ w_o


def moe_swiglu(h, router, w_gate, w_up, w_down):
    b, s, d = h.shape
    t = h.reshape(b * s, d)
    logits = (t @ router).astype(jnp.float32)
    top_vals, top_idx = lax.top_k(logits, TOP_K)
    gates = jax.nn.softmax(top_vals, axis=-1)
    combine = jnp.sum(jax.nn.one_hot(top_idx, N_EXPERTS, dtype=jnp.float32) * gates[..., None], axis=1)
    out = jnp.zeros_like(t)
    for e in range(N_EXPERTS):
        out = out + combine[:, e:e + 1].astype(t.dtype) * swiglu(t, w_gate[e], w_up[e], w_down[e])
    return out.reshape(b, s, d)


def setup_inputs(seed: int = 0) -> dict:
    key = jax.random.key(seed)
    ks = jax.random.split(key, 32)
    n_a = (DEPTH + 1) // 2
    n_b = DEPTH // 2
    nrm = jax.random.normal
    f32 = jnp.float32

    def gain(k, shape):
        return 1.0 + 0.02 * nrm(k, shape, f32)

    u = jax.random.uniform(ks[5], (n_a, SSM_HEADS), f32)
    dt = jnp.exp(u * (math.log(0.1) - math.log(0.001)) + math.log(0.001))
    dt = jnp.maximum(dt, 1e-4)
    dt_bias = dt + jnp.log(-jnp.expm1(-dt))
    A_log = jnp.log(jax.random.uniform(ks[6], (n_a, SSM_HEADS), f32, 1.0, 16.0))
    return {
        "x": nrm(ks[0], (BATCH, SEQ, D_MODEL), f32),
        "ssm_pre_norm": gain(ks[1], (n_a, D_MODEL)),
        "ssm_in_proj": nrm(ks[2], (n_a, D_MODEL, D_IN_SSM), f32) * D_MODEL ** -0.5,
        "ssm_conv_w": nrm(ks[3], (n_a, CONV_WIDTH, D_XBC), f32) * CONV_WIDTH ** -0.5,
        "ssm_conv_b": 0.02 * nrm(ks[4], (n_a, D_XBC), f32),
        "ssm_dt_bias": dt_bias,
        "ssm_A_log": A_log,
        "ssm_D": gain(ks[7], (n_a, SSM_HEADS)),
        "ssm_out_norm": gain(ks[8], (n_a, D_INNER)),
        "ssm_out_proj": nrm(ks[9], (n_a, D_INNER, D_MODEL), f32) * D_INNER ** -0.5,
        "dense_pre_norm": gain(ks[10], (n_a, D_MODEL)),
        "dense_w_gate": nrm(ks[11], (n_a, D_MODEL, D_FF), f32) * D_MODEL ** -0.5,
        "dense_w_up": nrm(ks[12], (n_a, D_MODEL, D_FF), f32) * D_MODEL ** -0.5,
        "dense_w_down": nrm(ks[13], (n_a, D_FF, D_MODEL), f32) * D_FF ** -0.5,
        "att_pre_norm": gain(ks[14], (n_b, D_MODEL)),
        "att_w_qkv": nrm(ks[15], (n_b, D_MODEL, 3 * ATT_WIDTH), f32) * D_MODEL ** -0.5,
        "att_q_norm": gain(ks[16], (n_b, ATT_HEAD_DIM)),
        "att_k_norm": gain(ks[17], (n_b, ATT_HEAD_DIM)),
        "att_lambda_q1": 0.1 * nrm(ks[18], (n_b, ATT_HEAD_DIM), f32),
        "att_lambda_k1": 0.1 * nrm(ks[19], (n_b, ATT_HEAD_DIM), f32),
        "att_lambda_q2": 0.1 * nrm(ks[20], (n_b, ATT_HEAD_DIM), f32),
        "att_lambda_k2": 0.1 * nrm(ks[21], (n_b, ATT_HEAD_DIM), f32),
        "att_sub_norm": gain(ks[22], (n_b, ATT_V_DIM)),
        "att_w_o": nrm(ks[23], (n_b, ATT_WIDTH, D_MODEL), f32) * ATT_WIDTH ** -0.5,
        "moe_pre_norm": gain(ks[24], (n_b, D_MODEL)),
        "moe_router": nrm(ks[25], (n_b, D_MODEL, N_EXPERTS), f32) * D_MODEL ** -0.5,
        "moe_w_gate": nrm(ks[26], (n_b, N_EXPERTS, D_MODEL, D_FF), f32) * D_MODEL ** -0.5,
        "moe_w_up": nrm(ks[27], (n_b, N_EXPERTS, D_MODEL, D_FF), f32) * D_MODEL ** -0.5,
        "moe_w_down": nrm(ks[28], (n_b, N_EXPERTS, D_FF, D_MODEL), f32) * D_FF ** -0.5,
    }


def reference(x, ssm_pre_norm, ssm_in_proj, ssm_conv_w, ssm_conv_b, ssm_dt_bias, ssm_A_log,
              ssm_D, ssm_out_norm, ssm_out_proj, dense_pre_norm, dense_w_gate, dense_w_up,
              dense_w_down, att_pre_norm, att_w_qkv, att_q_norm, att_k_norm, att_lambda_q1,
              att_lambda_k1, att_lambda_q2, att_lambda_k2, att_sub_norm, att_w_o, moe_pre_norm,
              moe_router, moe_w_gate, moe_w_up, moe_w_down):
    h = x
    for i in range(DEPTH):
        j = i // N_MIXERS
        if i % N_MIXERS == 0:
            h = h + mamba2_mixer(rms_norm(h, ssm_pre_norm[j]), ssm_in_proj[j], ssm_conv_w[j],
                                 ssm_conv_b[j], ssm_dt_bias[j], ssm_A_log[j], ssm_D[j],
                                 ssm_out_norm[j], ssm_out_proj[j])
            h = h + swiglu(rms_norm(h, dense_pre_norm[j]), dense_w_gate[j], dense_w_up[j],
                           dense_w_down[j])
        else:
            lambda_init = 0.8 - 0.6 * math.exp(-0.3 * i)
            h = h + diff_attention(rms_norm(h, att_pre_norm[j]), att_w_qkv[j], att_q_norm[j],
                                   att_k_norm[j], att_lambda_q1[j], att_lambda_k1[j],
                                   att_lambda_q2[j], att_lambda_k2[j], att_sub_norm[j],
                                   att_w_o[j], lambda_init)
            h = h + moe_swiglu(rms_norm(h, moe_pre_norm[j]), moe_router[j], moe_w_gate[j],
                               moe_w_up[j], moe_w_down[j])
    return h
```

```python
import functools
import math

import jax
import jax.numpy as jnp
from jax import lax
from jax.experimental import pallas as pl
from jax.experimental.pallas import tpu as pltpu

F32 = jnp.float32
BF16 = jnp.bfloat16

EPS = 1e-5
DEPTH = 2
N_MIXERS = 2

SSM_GROUPS = 8
SSM_STATE = 128
CHUNK = 128
CONV_WIDTH = 4
SSM_HEAD_DIM = 64

ATT_HEAD_DIM = 64
ATT_V_DIM = 2 * ATT_HEAD_DIM

TOP_K = 2

LANES = 128
SUBLANES = 8
VMEM_LIMIT_BYTES = 56 * 1024 * 1024

TM_PROJ = 1024
TN_PROJ = 1024
TM_RES = 512
TN_RES = 1024
TF_FFN = 512
TM_MOE = 512
TN_MOE_DOWN = 512
T_ATT = 256
ATT_NORM_ROWS = 512


def _tile(dim, pref, quantum=LANES):
    if dim <= pref:
        return dim
    t = (pref // quantum) * quantum
    while t >= quantum:
        if dim % t == 0:
            return t
        t -= quantum
    return dim


def _params(sem):
    return pltpu.CompilerParams(dimension_semantics=sem, vmem_limit_bytes=VMEM_LIMIT_BYTES)


def _rms_rows(x, gain):
    return x * lax.rsqrt(jnp.mean(x * x, axis=-1, keepdims=True) + EPS) * gain


def _silu(x):
    return x * (1.0 / (1.0 + jnp.exp(-x)))


def _norm_mm_kernel(x_ref, g_ref, w_ref, o_ref, xn_ref):
    @pl.when(pl.program_id(1) == 0)
    def _():
        xn_ref[...] = _rms_rows(x_ref[...], g_ref[...]).astype(BF16)

    o_ref[...] = jnp.dot(xn_ref[...], w_ref[...], preferred_element_type=F32).astype(o_ref.dtype)


def norm_matmul(x, gain, w, out_dtype):
    t, d = x.shape
    n = w.shape[1]
    tm, tn = _tile(t, TM_PROJ, SUBLANES), _tile(n, TN_PROJ)
    return pl.pallas_call(
        _norm_mm_kernel,
        out_shape=jax.ShapeDtypeStruct((t, n), out_dtype),
        grid=(t // tm, n // tn),
        in_specs=[
            pl.BlockSpec((tm, d), lambda i, j: (i, 0)),
            pl.BlockSpec((1, d), lambda i, j: (0, 0)),
            pl.BlockSpec((d, tn), lambda i, j: (0, j)),
        ],
        out_specs=pl.BlockSpec((tm, tn), lambda i, j: (i, j)),
        scratch_shapes=[pltpu.VMEM((tm, d), BF16)],
        compiler_params=_params(("arbitrary", "arbitrary")),
        name="norm_matmul",
    )(x, gain.reshape(1, d), w)


def _norm_gateup_kernel(x_ref, g_ref, wg_ref, wu_ref, o_ref, xn_ref):
    @pl.when(pl.program_id(1) == 0)
    def _():
        xn_ref[...] = _rms_rows(x_ref[...], g_ref[...]).astype(BF16)

    xn = xn_ref[...]
    a = jnp.dot(xn, wg_ref[...], preferred_element_type=F32)
    b = jnp.dot(xn, wu_ref[...], preferred_element_type=F32)
    o_ref[...] = (_silu(a) * b).astype(o_ref.dtype)


def norm_gateup(x, gain, wg, wu):
    t, d = x.shape
    f = wg.shape[1]
    tm, tf = _tile(t, TM_PROJ, SUBLANES), _tile(f, TF_FFN)
    return pl.pallas_call(
        _norm_gateup_kernel,
        out_shape=jax.ShapeDtypeStruct((t, f), BF16),
        grid=(t // tm, f // tf),
        in_specs=[
            pl.BlockSpec((tm, d), lambda i, j: (i, 0)),
            pl.BlockSpec((1, d), lambda i, j: (0, 0)),
            pl.BlockSpec((d, tf), lambda i, j: (0, j)),
            pl.BlockSpec((d, tf), lambda i, j: (0, j)),
        ],
        out_specs=pl.BlockSpec((tm, tf), lambda i, j: (i, j)),
        scratch_shapes=[pltpu.VMEM((tm, d), BF16)],
        compiler_params=_params(("arbitrary", "arbitrary")),
        name="norm_gateup",
    )(x, gain.reshape(1, d), wg, wu)


def _mm_res_kernel(a_ref, w_ref, r_ref, o_ref):
    o_ref[...] = r_ref[...] + jnp.dot(a_ref[...], w_ref[...], preferred_element_type=F32)


def matmul_residual(a, w, res):
    t, k = a.shape
    n = w.shape[1]
    tm, tn = _tile(t, TM_RES, SUBLANES), _tile(n, TN_RES)
    return pl.pallas_call(
        _mm_res_kernel,
        out_shape=jax.ShapeDtypeStruct((t, n), F32),
        grid=(t // tm, n // tn),
        in_specs=[
            pl.BlockSpec((tm, k), lambda i, j: (i, 0)),
            pl.BlockSpec((k, tn), lambda i, j: (0, j)),
            pl.BlockSpec((tm, tn), lambda i, j: (i, j)),
        ],
        out_specs=pl.BlockSpec((tm, tn), lambda i, j: (i, j)),
        compiler_params=_params(("arbitrary", "arbitrary")),
        name="matmul_residual",
    )(a, w, res)


def _ssd_kernel(x_ref, bm_ref, cm_ref, z_ref, dtt_ref, cwx_ref, cbx_ref, cwb_ref, cbb_ref,
                cwc_ref, cbc_ref, dtb_ref, alog_ref, dx_ref, gn_ref, ex_ref, ecol_ref,
                o_ref, pad_ref, h_ref, *, hpg, gw):
    L = CHUNK
    n = SSM_STATE
    hd = gw // hpg
    c = pl.program_id(2)
    tail = SUBLANES

    @pl.when(c == 0)
    def _():
        pad_ref[0:tail, :] = jnp.zeros((tail, gw + 2 * n), F32)
        h_ref[...] = jnp.zeros_like(h_ref)

    @pl.when(c > 0)
    def _():
        pad_ref[0:tail, :] = pad_ref[L:L + tail, :]

    pad_ref[tail:tail + L, 0:gw] = x_ref[0].astype(F32)
    pad_ref[tail:tail + L, gw:gw + n] = bm_ref[0].astype(F32)
    pad_ref[tail:tail + L, gw + n:gw + 2 * n] = cm_ref[0].astype(F32)

    def conv(col0, width, w_ref, b_ref):
        acc = b_ref[...]
        for k in range(CONV_WIDTH):
            r0 = tail - (CONV_WIDTH - 1) + k
            acc = acc + w_ref[k:k + 1, :] * pad_ref[r0:r0 + L, col0:col0 + width]
        return _silu(acc)

    xs = conv(0, gw, cwx_ref, cbx_ref)
    bc = conv(gw, n, cwb_ref, cbb_ref)
    cc = conv(gw + n, n, cwc_ref, cbc_ref)

    raw = dtt_ref[0] + dtb_ref[...]
    dt = jnp.maximum(raw, 0.0) + jnp.log1p(jnp.exp(-jnp.abs(raw)))
    a = dt * (-jnp.exp(alog_ref[...]))
    row = lax.broadcasted_iota(jnp.int32, (L, L), 0)
    col = lax.broadcasted_iota(jnp.int32, (L, L), 1)
    upper = (row <= col).astype(F32)
    acs = jnp.dot(a, upper, preferred_element_type=F32, precision=lax.Precision.HIGHEST)
    a_last = acs[:, L - 1:L]
    dte = jnp.exp(a_last - acs)
    eacs = jnp.exp(acs)

    def expand(v, e_ref):
        return lax.dot_general(v, e_ref[...], (((0,), (0,)), ((), ())),
                               preferred_element_type=F32, precision=lax.Precision.HIGHEST)

    dt_x = expand(dt, ex_ref)
    w_x = expand(dt * dte, ex_ref)
    eacs_x = expand(eacs, ex_ref)
    acs_col = expand(acs, ecol_ref)

    xdt = xs * dt_x
    bcb = bc.astype(BF16)
    ccb = cc.astype(BF16)
    cb = lax.dot_general(ccb, bcb, (((1,), (1,)), ((), ())), preferred_element_type=F32)
    causal = row >= col

    lane = lax.broadcasted_iota(jnp.int32, (1, 2 * hd), 1)
    lo = lane < hd
    y_parts = []
    for q in range(hpg // 2):
        ms = []
        for j in (2 * q, 2 * q + 1):
            seg = acs_col[:, j * L:(j + 1) * L] - acs[j:j + 1, :]
            dec = jnp.exp(jnp.where(causal, seg, -jnp.inf))
            ms.append((cb * dec).astype(BF16))
        mcat = jnp.concatenate(ms, axis=1)
        xq = xdt[:, q * 2 * hd:(q + 1) * 2 * hd]
        rhs = jnp.concatenate([jnp.where(lo, xq, 0.0), jnp.where(lo, 0.0, xq)], axis=0)
        y_parts.append(jnp.dot(mcat, rhs.astype(BF16), preferred_element_type=F32))
    y_diag = jnp.concatenate(y_parts, axis=1)

    h_in = h_ref[...]
    y_off = jnp.dot(ccb, h_in.astype(BF16), preferred_element_type=F32) * eacs_x
    states = lax.dot_general(bcb, (xs * w_x).astype(BF16), (((0,), (0,)), ((), ())),
                             preferred_element_type=F32)
    h_ref[...] = h_in * eacs_x[L - 1:L, :] + states

    y = y_diag + y_off + dx_ref[...] * xs
    gated = y * _silu(z_ref[0].astype(F32))
    o_ref[0] = _rms_rows(gated, gn_ref[...]).astype(o_ref.dtype)


def ssd_core(z, xbc, dtt, conv_w, conv_b, dt_bias, a_log, d_skip, out_norm):
    b, s, d_inner = z.shape
    g = SSM_GROUPS
    n = SSM_STATE
    heads = dt_bias.shape[0]
    hpg = heads // g
    gw = d_inner // g
    hd = gw // hpg
    L = CHUNK
    nb = d_inner // n
    assert gw % LANES == 0 and hpg % 2 == 0 and 2 * hd == LANES and s % L == 0

    ex = jnp.repeat(jnp.eye(hpg, dtype=F32), hd, axis=1)
    ecol = jnp.repeat(jnp.eye(hpg, dtype=F32), L, axis=1)
    dx = jnp.repeat(d_skip.astype(F32), hd).reshape(1, d_inner)

    kern = functools.partial(_ssd_kernel, hpg=hpg, gw=gw)
    return pl.pallas_call(
        kern,
        out_shape=jax.ShapeDtypeStruct((b, s, d_inner), BF16),
        grid=(b, g, s // L),
        in_specs=[
            pl.BlockSpec((1, L, gw), lambda i, j, c: (i, c, j)),
            pl.BlockSpec((1, L, n), lambda i, j, c: (i, c, nb + j)),
            pl.BlockSpec((1, L, n), lambda i, j, c: (i, c, nb + g + j)),
            pl.BlockSpec((1, L, gw), lambda i, j, c: (i, c, j)),
            pl.BlockSpec((1, hpg, L), lambda i, j, c: (i, j, c)),
            pl.BlockSpec((CONV_WIDTH, gw), lambda i, j, c: (0, j)),
            pl.BlockSpec((1, gw), lambda i, j, c: (0, j)),
            pl.BlockSpec((CONV_WIDTH, n), lambda i, j, c: (0, nb + j)),
            pl.BlockSpec((1, n), lambda i, j, c: (0, nb + j)),
            pl.BlockSpec((CONV_WIDTH, n), lambda i, j, c: (0, nb + g + j)),
            pl.BlockSpec((1, n), lambda i, j, c: (0, nb + g + j)),
            pl.BlockSpec((hpg, 1), lambda i, j, c: (j, 0)),
            pl.BlockSpec((hpg, 1), lambda i, j, c: (j, 0)),
            pl.BlockSpec((1, gw), lambda i, j, c: (0, j)),
            pl.BlockSpec((1, gw), lambda i, j, c: (0, j)),
            pl.BlockSpec((hpg, gw), lambda i, j, c: (0, 0)),
            pl.BlockSpec((hpg, hpg * L), lambda i, j, c: (0, 0)),
        ],
        out_specs=pl.BlockSpec((1, L, gw), lambda i, j, c: (i, c, j)),
        scratch_shapes=[pltpu.VMEM((L + SUBLANES, gw + 2 * n), F32), pltpu.VMEM((n, gw), F32)],
        compiler_params=_params(("arbitrary", "arbitrary", "arbitrary")),
        name="ssd_core",
    )(xbc, xbc, xbc, z, dtt, conv_w, conv_b.reshape(1, -1), conv_w, conv_b.reshape(1, -1),
      conv_w, conv_b.reshape(1, -1), dt_bias.reshape(heads, 1), a_log.reshape(heads, 1),
      dx, out_norm.reshape(1, d_inner), ex, ecol)


def _attn_kernel(q_ref, k_ref, v_ref, qg_ref, kg_ref, lq1_ref, lk1_ref, lq2_ref, lk2_ref,
                 sg_ref, o_ref, k1_ref, k2_ref, m_ref, l_ref, acc_ref, *, t, lambda_init):
    s_len = k_ref.shape[1]
    hd = ATT_HEAD_DIM
    qi = pl.program_id(2)
    lane = lax.broadcasted_iota(jnp.int32, (1, 2 * hd), 1)
    lo = lane < hd

    def comp_norm(x, gain):
        x2 = x * x
        s_lo = jnp.sum(jnp.where(lo, x2, 0.0), axis=-1, keepdims=True)
        s_hi = jnp.sum(jnp.where(lo, 0.0, x2), axis=-1, keepdims=True)
        inv = jnp.where(lo, lax.rsqrt(s_lo * (1.0 / hd) + EPS), lax.rsqrt(s_hi * (1.0 / hd) + EPS))
        return x * inv * gain

    @pl.when(qi == 0)
    def _():
        rows = _tile(s_len, ATT_NORM_ROWS, SUBLANES)

        def body(i, carry):
            r0 = pl.multiple_of(i * rows, rows)
            kn = comp_norm(k_ref[0, pl.ds(r0, rows), :].astype(F32), kg_ref[...])
            k1_ref[pl.ds(r0, rows), :] = jnp.where(lo, kn, 0.0).astype(BF16)
            k2_ref[pl.ds(r0, rows), :] = jnp.where(lo, 0.0, kn).astype(BF16)
            return carry

        lax.fori_loop(0, s_len // rows, body, 0)

    qn = (comp_norm(q_ref[0].astype(F32), qg_ref[...]) * (hd ** -0.5)).astype(BF16)

    m_ref[...] = jnp.full(m_ref.shape, -jnp.inf, F32)
    l_ref[...] = jnp.zeros(l_ref.shape, F32)
    acc_ref[...] = jnp.zeros(acc_ref.shape, F32)

    def step(j, masked):
        r0 = pl.multiple_of(j * t, t)
        vv = v_ref[0, pl.ds(r0, t), :]
        for c, kref in enumerate((k1_ref, k2_ref)):
            kk = kref[pl.ds(r0, t), :]
            sc = lax.dot_general(qn, kk, (((1,), (1,)), ((), ())), preferred_element_type=F32)
            if masked:
                row = lax.broadcasted_iota(jnp.int32, (t, t), 0)
                col = lax.broadcasted_iota(jnp.int32, (t, t), 1)
                sc = jnp.where(row >= col, sc, -jnp.inf)
            m_prev = m_ref[c]
            m_new = jnp.maximum(m_prev, jnp.max(sc, axis=-1, keepdims=True))
            alpha = jnp.exp(m_prev - m_new)
            p = jnp.exp(sc - m_new)
            l_ref[c] = alpha * l_ref[c] + jnp.sum(p, axis=-1, keepdims=True)
            acc_ref[c] = alpha * acc_ref[c] + jnp.dot(p.astype(BF16), vv,
                                                      preferred_element_type=F32)
            m_ref[c] = m_new

    def full_step(j, carry):
        step(j, False)
        return carry

    lax.fori_loop(0, qi, full_step, 0)
    step(qi, True)

    lam = (jnp.exp(jnp.sum(lq1_ref[...] * lk1_ref[...], axis=-1, keepdims=True))
           - jnp.exp(jnp.sum(lq2_ref[...] * lk2_ref[...], axis=-1, keepdims=True)) + lambda_init)
    o = acc_ref[0] / l_ref[0] - lam * (acc_ref[1] / l_ref[1])
    o_ref[0] = (_rms_rows(o, sg_ref[...]) * (1.0 - lambda_init)).astype(o_ref.dtype)


def diff_attention_core(qkv, q_norm, k_norm, lq1, lk1, lq2, lk2, sub_norm, lambda_init):
    b, s, w3 = qkv.shape
    width = w3 // 3
    heads = width // ATT_V_DIM
    hd = ATT_HEAD_DIM
    t = _tile(s, T_ATT, SUBLANES)
    vec = lambda a: a.astype(F32).reshape(1, -1)
    qg = jnp.tile(q_norm.astype(F32), 2).reshape(1, 2 * hd)
    kg = jnp.tile(k_norm.astype(F32), 2).reshape(1, 2 * hd)
    small = lambda width_: pl.BlockSpec((1, width_), lambda i, h, j: (0, 0))
    kern = functools.partial(_attn_kernel, t=t, lambda_init=lambda_init)
    return pl.pallas_call(
        kern,
        out_shape=jax.ShapeDtypeStruct((b, s, width), BF16),
        grid=(b, heads, s // t),
        in_specs=[
            pl.BlockSpec((1, t, 2 * hd), lambda i, h, j: (i, j, h)),
            pl.BlockSpec((1, s, 2 * hd), lambda i, h, j: (i, 0, heads + h)),
            pl.BlockSpec((1, s, ATT_V_DIM), lambda i, h, j: (i, 0, 2 * heads + h)),
            small(2 * hd), small(2 * hd), small(hd), small(hd), small(hd), small(hd),
            small(ATT_V_DIM),
        ],
        out_specs=pl.BlockSpec((1, t, ATT_V_DIM), lambda i, h, j: (i, j, h)),
        scratch_shapes=[
            pltpu.VMEM((s, 2 * hd), BF16), pltpu.VMEM((s, 2 * hd), BF16),
            pltpu.VMEM((2, t, 1), F32), pltpu.VMEM((2, t, 1), F32),
            pltpu.VMEM((2, t, ATT_V_DIM), F32),
        ],
        compiler_params=_params(("arbitrary", "arbitrary", "arbitrary")),
        name="diff_attention",
    )(qkv, qkv, qkv, qg, kg, vec(lq1), vec(lk1), vec(lq2), vec(lk2), vec(sub_norm))


def _route_kernel(x_ref, g_ref, r_ref, xn_ref, info_ref, *, n_experts):
    xn = _rms_rows(x_ref[...], g_ref[...])
    xn_ref[...] = xn.astype(BF16)
    logits = jnp.dot(xn, r_ref[...], preferred_element_type=F32, precision=lax.Precision.HIGHEST)
    lane = lax.broadcasted_iota(jnp.int32, logits.shape, 1)
    logits = jnp.where(lane < n_experts, logits, -jnp.inf)
    m1 = jnp.max(logits, axis=-1, keepdims=True)
    i1 = jnp.min(jnp.where(logits == m1, lane, LANES), axis=-1, keepdims=True)
    rest = jnp.where(lane == i1, -jnp.inf, logits)
    m2 = jnp.max(rest, axis=-1, keepdims=True)
    i2 = jnp.min(jnp.where(rest == m2, lane, LANES), axis=-1, keepdims=True)
    e2 = jnp.exp(m2 - m1)
    g1 = 1.0 / (1.0 + e2)
    g2 = e2 * g1
    info = jnp.where(lane == 0, i1.astype(F32),
                     jnp.where(lane == 1, i2.astype(F32),
                               jnp.where(lane == 2, g1, jnp.where(lane == 3, g2, 0.0))))
    info_ref[...] = info


def moe_route(h, gain, router):
    t, d = h.shape
    e = router.shape[1]
    tm = _tile(t, TM_RES, SUBLANES)
    rpad = jnp.zeros((d, LANES), F32).at[:, :e].set(router.astype(F32))
    return pl.pallas_call(
        functools.partial(_route_kernel, n_experts=e),
        out_shape=(jax.ShapeDtypeStruct((t, d), BF16), jax.ShapeDtypeStruct((t, LANES), F32)),
        grid=(t // tm,),
        in_specs=[
            pl.BlockSpec((tm, d), lambda i: (i, 0)),
            pl.BlockSpec((1, d), lambda i: (0, 0)),
            pl.BlockSpec((d, LANES), lambda i: (0, 0)),
        ],
        out_specs=(pl.BlockSpec((tm, d), lambda i: (i, 0)),
                   pl.BlockSpec((tm, LANES), lambda i: (i, 0))),
        compiler_params=_params(("arbitrary",)),
        name="moe_route",
    )(h, gain.reshape(1, d), rpad)


def _new_weights(te_ref, i):
    prev = te_ref[jnp.maximum(i - 1, 0)]
    return jnp.logical_or(i == 0, te_ref[i] != prev)


def _moe_gateup_kernel(te_ref, nu_ref, x_ref, wg_ref, wu_ref, o_ref, wgb_ref, wub_ref):
    i = pl.program_id(1)

    @pl.when(_new_weights(te_ref, i))
    def _():
        wgb_ref[...] = wg_ref[0].astype(BF16)
        wub_ref[...] = wu_ref[0].astype(BF16)

    @pl.when(i < nu_ref[0])
    def _():
        x = x_ref[...]
        a = jnp.dot(x, wgb_ref[...], preferred_element_type=F32)
        b = jnp.dot(x, wub_ref[...], preferred_element_type=F32)
        o_ref[...] = (_silu(a) * b).astype(o_ref.dtype)

    @pl.when(i >= nu_ref[0])
    def _():
        o_ref[...] = jnp.zeros_like(o_ref)


def _moe_down_kernel(te_ref, nu_ref, h_ref, wd_ref, o_ref, wdb_ref):
    i = pl.program_id(1)

    @pl.when(_new_weights(te_ref, i))
    def _():
        wdb_ref[...] = wd_ref[0].astype(BF16)

    @pl.when(i < nu_ref[0])
    def _():
        o_ref[...] = jnp.dot(h_ref[...], wdb_ref[...], preferred_element_type=F32)

    @pl.when(i >= nu_ref[0])
    def _():
        o_ref[...] = jnp.zeros_like(o_ref)


def moe_experts(xs, tile_expert, n_used, w_gate, w_up, w_down, tm):
    slots, d = xs.shape
    e, _, f = w_gate.shape
    n_tiles = slots // tm
    tf = _tile(f, TF_FFN)
    hmid = pl.pallas_call(
        _moe_gateup_kernel,
        out_shape=jax.ShapeDtypeStruct((slots, f), BF16),
        grid_spec=pltpu.PrefetchScalarGridSpec(
            num_scalar_prefetch=2,
            grid=(f // tf, n_tiles),
            in_specs=[
                pl.BlockSpec((tm, d), lambda j, i, te, nu: (i, 0)),
                pl.BlockSpec((1, d, tf), lambda j, i, te, nu: (te[i], 0, j)),
                pl.BlockSpec((1, d, tf), lambda j, i, te, nu: (te[i], 0, j)),
            ],
            out_specs=pl.BlockSpec((tm, tf), lambda j, i, te, nu: (i, j)),
            scratch_shapes=[pltpu.VMEM((d, tf), BF16), pltpu.VMEM((d, tf), BF16)],
        ),
        compiler_params=_params(("arbitrary", "arbitrary")),
        name="moe_gateup",
    )(tile_expert, n_used, xs, w_gate, w_up)

    tn = _tile(d, TN_MOE_DOWN)
    return pl.pallas_call(
        _moe_down_kernel,
        out_shape=jax.ShapeDtypeStruct((slots, d), F32),
        grid_spec=pltpu.PrefetchScalarGridSpec(
            num_scalar_prefetch=2,
            grid=(d // tn, n_tiles),
            in_specs=[
                pl.BlockSpec((tm, f), lambda j, i, te, nu: (i, 0)),
                pl.BlockSpec((1, f, tn), lambda j, i, te, nu: (te[i], 0, j)),
            ],
            out_specs=pl.BlockSpec((tm, tn), lambda j, i, te, nu: (i, j)),
            scratch_shapes=[pltpu.VMEM((f, tn), BF16)],
        ),
        compiler_params=_params(("arbitrary", "arbitrary")),
        name="moe_down",
    )(tile_expert, n_used, hmid, w_down)


def moe_block(h, gain, router, w_gate, w_up, w_down):
    t, d = h.shape
    e = router.shape[1]
    tm = min(TM_MOE, t)
    xn, info = moe_route(h, gain, router)
    idx = info[:, :TOP_K].astype(jnp.int32)
    gates = info[:, TOP_K:2 * TOP_K]

    e_flat = idx.reshape(-1)
    onehot = (e_flat[:, None] == jnp.arange(e, dtype=jnp.int32)[None, :]).astype(jnp.int32)
    csum = jnp.cumsum(onehot, axis=0)
    rank = jnp.sum((csum - onehot) * onehot, axis=1)
    counts = csum[-1]
    tiles_per_e = (counts + tm - 1) // tm
    tile_end = jnp.cumsum(tiles_per_e)
    starts = (tile_end - tiles_per_e) * tm
    slot = starts[e_flat] + rank
    n_tiles = (TOP_K * t) // tm + e
    n_slots = n_tiles * tm
    token_of_slot = jnp.zeros((n_slots,), jnp.int32).at[slot].set(
        jnp.arange(TOP_K * t, dtype=jnp.int32) // TOP_K)
    tile_expert = jnp.minimum(
        jnp.searchsorted(tile_end, jnp.arange(n_tiles, dtype=jnp.int32), side="right"),
        e - 1).astype(jnp.int32)
    n_used = tile_end[-1:].astype(jnp.int32)

    xs = jnp.take(xn, token_of_slot, axis=0)
    ys = moe_experts(xs, tile_expert, n_used, w_gate, w_up, w_down, tm)
    slot2 = slot.reshape(t, TOP_K)
    out = h
    for k in range(TOP_K):
        out = out + gates[:, k:k + 1] * jnp.take(ys, slot2[:, k], axis=0)
    return out


def mamba_layer(h, b, s, pre_norm, in_proj, conv_w, conv_b, dt_bias, a_log, d_skip, out_norm,
                out_proj):
    d_inner = out_proj.shape[0]
    d_xbc = conv_w.shape[1]
    heads = dt_bias.shape[0]
    w = in_proj.astype(BF16)
    z = norm_matmul(h, pre_norm, w[:, :d_inner], BF16)
    xbc = norm_matmul(h, pre_norm, w[:, d_inner:d_inner + d_xbc], BF16)
    w_dt = jnp.zeros((h.shape[1], LANES), BF16).at[:, :heads].set(w[:, d_inner + d_xbc:])
    dt = norm_matmul(h, pre_norm, w_dt, F32)[:, :heads]
    dtt = jnp.swapaxes(dt.reshape(b, s, heads), 1, 2)
    y = ssd_core(z.reshape(b, s, d_inner), xbc.reshape(b, s, d_xbc), dtt, conv_w.astype(F32),
                 conv_b.astype(F32), dt_bias.astype(F32), a_log.astype(F32), d_skip, out_norm)
    return matmul_residual(y.reshape(b * s, d_inner), out_proj.astype(BF16), h)


def dense_layer(h, pre_norm, w_gate, w_up, w_down):
    mid = norm_gateup(h, pre_norm, w_gate.astype(BF16), w_up.astype(BF16))
    return matmul_residual(mid, w_down.astype(BF16), h)


def attention_layer(h, b, s, pre_norm, w_qkv, q_norm, k_norm, lq1, lk1, lq2, lk2, sub_norm, w_o,
                    lambda_init):
    qkv = norm_matmul(h, pre_norm, w_qkv.astype(BF16), BF16)
    o = diff_attention_core(qkv.reshape(b, s, -1), q_norm, k_norm, lq1, lk1, lq2, lk2, sub_norm,
                            lambda_init)
    return matmul_residual(o.reshape(b * s, -1), w_o.astype(BF16), h)


def kernel(x, ssm_pre_norm, ssm_in_proj, ssm_conv_w, ssm_conv_b, ssm_dt_bias, ssm_A_log, ssm_D,
           ssm_out_norm, ssm_out_proj, dense_pre_norm, dense_w_gate, dense_w_up, dense_w_down,
           att_pre_norm, att_w_qkv, att_q_norm, att_k_norm, att_lambda_q1, att_lambda_k1,
           att_lambda_q2, att_lambda_k2, att_sub_norm, att_w_o, moe_pre_norm, moe_router,
           moe_w_gate, moe_w_up, moe_w_down):
    b, s, d = x.shape
    h = x.reshape(b * s, d)
    for i in range(DEPTH):
        j = i // N_MIXERS
        if i % N_MIXERS == 0:
            h = mamba_layer(h, b, s, ssm_pre_norm[j], ssm_in_proj[j], ssm_conv_w[j],
                            ssm_conv_b[j], ssm_dt_bias[j], ssm_A_log[j], ssm_D[j],
                            ssm_out_norm[j], ssm_out_proj[j])
            h = dense_layer(h, dense_pre_norm[j], dense_w_gate[j], dense_w_up[j],
                            dense_w_down[j])
        else:
            lambda_init = 0.8 - 0.6 * math.exp(-0.3 * i)
            h = attention_layer(h, b, s, att_pre_norm[j], att_w_qkv[j], att_q_norm[j],
                                att_k_norm[j], att_lambda_q1[j], att_lambda_k1[j],
                                att_lambda_q2[j], att_lambda_k2[j], att_sub_norm[j], att_w_o[j],
                                lambda_init)
            h = moe_block(h, moe_pre_norm[j], moe_router[j], moe_w_gate[j], moe_w_up[j],
                          moe_w_down[j])
    return h.reshape(b, s, d)
```

```python
import functools
import math

import jax
import jax.numpy as jnp
from jax import lax
from jax.experimental import pallas as pl
from jax.experimental.pallas import tpu as pltpu

F32 = jnp.float32
BF16 = jnp.bfloat16

EPS = 1e-5
DEPTH = 2
N_MIXERS = 2

SSM_GROUPS = 8
SSM_STATE = 128
CHUNK = 128
CONV_WIDTH = 4
SSM_HEAD_DIM = 64

ATT_HEAD_DIM = 64
ATT_V_DIM = 2 * ATT_HEAD_DIM

TOP_K = 2

LOG2E = math.log2(math.e)

LANES = 128
SUBLANES = 8
VMEM_LIMIT_BYTES = 56 * 1024 * 1024

TM_PROJ = 1024
TN_PROJ = 1024
TM_RES = 512
TN_RES = 1024
TF_FFN = 512
TM_MOE = 512
TN_MOE_DOWN = 512
T_ATT = 256


def _tile(dim, pref, quantum=LANES):
    if dim <= pref:
        return dim
    t = (pref // quantum) * quantum
    while t >= quantum:
        if dim % t == 0:
            return t
        t -= quantum
    return dim


def _params(sem):
    return pltpu.CompilerParams(dimension_semantics=sem, vmem_limit_bytes=VMEM_LIMIT_BYTES)


def _rms_rows(x, gain):
    return x * lax.rsqrt(jnp.mean(x * x, axis=-1, keepdims=True) + EPS) * gain


def _silu(x):
    return 0.5 * x * (1.0 + jnp.tanh(0.5 * x))


def _split3(v):
    hi = v.astype(BF16).astype(F32)
    r = v - hi
    mid = r.astype(BF16).astype(F32)
    return [hi, mid, r - mid]


def _norm_mm_kernel(x_ref, g_ref, w_ref, o_ref, xn_ref):
    @pl.when(pl.program_id(1) == 0)
    def _():
        xn_ref[...] = _rms_rows(x_ref[...], g_ref[...]).astype(BF16)

    o_ref[...] = jnp.dot(xn_ref[...], w_ref[...], preferred_element_type=F32).astype(o_ref.dtype)


def norm_matmul(x, gain, w, out_dtype):
    t, d = x.shape
    n = w.shape[1]
    tm, tn = _tile(t, TM_PROJ, SUBLANES), _tile(n, TN_PROJ)
    return pl.pallas_call(
        _norm_mm_kernel,
        out_shape=jax.ShapeDtypeStruct((t, n), out_dtype),
        grid=(t // tm, n // tn),
        in_specs=[
            pl.BlockSpec((tm, d), lambda i, j: (i, 0)),
            pl.BlockSpec((1, d), lambda i, j: (0, 0)),
            pl.BlockSpec((d, tn), lambda i, j: (0, j)),
        ],
        out_specs=pl.BlockSpec((tm, tn), lambda i, j: (i, j)),
        scratch_shapes=[pltpu.VMEM((tm, d), BF16)],
        compiler_params=_params(("arbitrary", "arbitrary")),
        name="norm_matmul",
    )(x, gain.reshape(1, d), w)


def _norm_gateup_kernel(x_ref, g_ref, wg_ref, wu_ref, o_ref, xn_ref):
    @pl.when(pl.program_id(1) == 0)
    def _():
        xn_ref[...] = _rms_rows(x_ref[...], g_ref[...]).astype(BF16)

    xn = xn_ref[...]
    a = jnp.dot(xn, wg_ref[...], preferred_element_type=F32)
    b = jnp.dot(xn, wu_ref[...], preferred_element_type=F32)
    o_ref[...] = (_silu(a) * b).astype(o_ref.dtype)


def norm_gateup(x, gain, wg, wu):
    t, d = x.shape
    f = wg.shape[1]
    tm, tf = _tile(t, TM_PROJ, SUBLANES), _tile(f, TF_FFN)
    return pl.pallas_call(
        _norm_gateup_kernel,
        out_shape=jax.ShapeDtypeStruct((t, f), BF16),
        grid=(t // tm, f // tf),
        in_specs=[
            pl.BlockSpec((tm, d), lambda i, j: (i, 0)),
            pl.BlockSpec((1, d), lambda i, j: (0, 0)),
            pl.BlockSpec((d, tf), lambda i, j: (0, j)),
            pl.BlockSpec((d, tf), lambda i, j: (0, j)),
        ],
        out_specs=pl.BlockSpec((tm, tf), lambda i, j: (i, j)),
        scratch_shapes=[pltpu.VMEM((tm, d), BF16)],
        compiler_params=_params(("arbitrary", "arbitrary")),
        name="norm_gateup",
    )(x, gain.reshape(1, d), wg, wu)


def _mm_res_kernel(a_ref, w_ref, r_ref, o_ref):
    o_ref[...] = r_ref[...] + jnp.dot(a_ref[...], w_ref[...], preferred_element_type=F32)


def matmul_residual(a, w, res):
    t, k = a.shape
    n = w.shape[1]
    tm, tn = _tile(t, TM_RES, SUBLANES), _tile(n, TN_RES)
    return pl.pallas_call(
        _mm_res_kernel,
        out_shape=jax.ShapeDtypeStruct((t, n), F32),
        grid=(t // tm, n // tn),
        in_specs=[
            pl.BlockSpec((tm, k), lambda i, j: (i, 0)),
            pl.BlockSpec((k, tn), lambda i, j: (0, j)),
            pl.BlockSpec((tm, tn), lambda i, j: (i, j)),
        ],
        out_specs=pl.BlockSpec((tm, tn), lambda i, j: (i, j)),
        compiler_params=_params(("arbitrary", "arbitrary")),
        name="matmul_residual",
    )(a, w, res)


def _ssd_kernel(x_ref, bm_ref, cm_ref, z_ref, dtt_ref, cwx_ref, cbx_ref, cwb_ref, cbb_ref,
                cwc_ref, cbc_ref, dtb_ref, alog_ref, dx_ref, gn_ref, e_ref,
                o_ref, pad_ref, h_ref, *, hpg, gw):
    L = CHUNK
    n = SSM_STATE
    hd = gw // hpg
    c = pl.program_id(2)
    tail = SUBLANES

    @pl.when(c == 0)
    def _():
        pad_ref[0:tail, :] = jnp.zeros((tail, gw + 2 * n), F32)
        h_ref[...] = jnp.zeros_like(h_ref)

    @pl.when(c > 0)
    def _():
        pad_ref[0:tail, :] = pad_ref[L:L + tail, :]

    pad_ref[tail:tail + L, 0:gw] = x_ref[0].astype(F32)
    pad_ref[tail:tail + L, gw:gw + n] = bm_ref[0].astype(F32)
    pad_ref[tail:tail + L, gw + n:gw + 2 * n] = cm_ref[0].astype(F32)

    def conv(col0, width, w_ref, b_ref):
        acc = b_ref[...]
        for k in range(CONV_WIDTH):
            r0 = tail - (CONV_WIDTH - 1) + k
            acc = acc + w_ref[k:k + 1, :] * pad_ref[r0:r0 + L, col0:col0 + width]
        return _silu(acc)

    xs = conv(0, gw, cwx_ref, cbx_ref)
    bc = conv(gw, n, cwb_ref, cbb_ref)
    cc = conv(gw + n, n, cwc_ref, cbc_ref)

    raw = dtt_ref[0] + dtb_ref[...]
    dt = jnp.maximum(raw, 0.0) + jnp.log1p(jnp.exp(-jnp.abs(raw)))
    a = dt * (-jnp.exp(alog_ref[...]))
    row = lax.broadcasted_iota(jnp.int32, (L, L), 0)
    col = lax.broadcasted_iota(jnp.int32, (L, L), 1)
    upper = (row <= col).astype(F32)
    acs = jnp.dot(a, upper, preferred_element_type=F32, precision=lax.Precision.HIGHEST)
    a_last = acs[:, L - 1:L]
    dte = jnp.exp(a_last - acs)
    eacs = jnp.exp(acs)

    parts = _split3(acs) + _split3(dt) + _split3(dt * dte) + _split3(eacs)
    stacked = jnp.concatenate(parts, axis=0).astype(BF16)
    spread = lax.dot_general(stacked, e_ref[...], (((0,), (0,)), ((), ())),
                             preferred_element_type=F32)
    acs_col = spread[:, :hpg * L]
    dt_x = spread[:, hpg * L:hpg * L + gw]
    w_x = spread[:, hpg * L + gw:hpg * L + 2 * gw]
    eacs_x = spread[:, hpg * L + 2 * gw:]

    xdt = xs * dt_x
    bcb = bc.astype(BF16)
    ccb = cc.astype(BF16)
    cb = lax.dot_general(ccb, bcb, (((1,), (1,)), ((), ())), preferred_element_type=F32)
    causal = row >= col

    lane = lax.broadcasted_iota(jnp.int32, (1, 2 * hd), 1)
    lo = lane < hd
    y_parts = []
    for q in range(hpg // 2):
        ms = []
        for j in (2 * q, 2 * q + 1):
            seg = acs_col[:, j * L:(j + 1) * L] - acs[j:j + 1, :]
            dec = jnp.exp(jnp.where(causal, seg, -jnp.inf))
            ms.append((cb * dec).astype(BF16))
        mcat = jnp.concatenate(ms, axis=1)
        xq = xdt[:, q * 2 * hd:(q + 1) * 2 * hd]
        rhs = jnp.concatenate([jnp.where(lo, xq, 0.0), jnp.where(lo, 0.0, xq)], axis=0)
        y_parts.append(jnp.dot(mcat, rhs.astype(BF16), preferred_element_type=F32))
    y_diag = jnp.concatenate(y_parts, axis=1)

    h_in = h_ref[...]
    y_off = jnp.dot(ccb, h_in.astype(BF16), preferred_element_type=F32) * eacs_x
    states = lax.dot_general(bcb, (xs * w_x).astype(BF16), (((0,), (0,)), ((), ())),
                             preferred_element_type=F32)
    h_ref[...] = h_in * eacs_x[L - 1:L, :] + states

    y = y_diag + y_off + dx_ref[...] * xs
    gated = y * _silu(z_ref[0].astype(F32))
    o_ref[0] = _rms_rows(gated, gn_ref[...]).astype(o_ref.dtype)


def ssd_core(z, xbc, dtt, conv_w, conv_b, dt_bias, a_log, d_skip, out_norm):
    b, s, d_inner = z.shape
    g = SSM_GROUPS
    n = SSM_STATE
    heads = dt_bias.shape[0]
    hpg = heads // g
    gw = d_inner // g
    hd = gw // hpg
    L = CHUNK
    nb = d_inner // n
    assert gw % LANES == 0 and hpg % 2 == 0 and 2 * hd == LANES and s % L == 0

    eye = jnp.eye(hpg, dtype=F32)
    widths = [hpg * L, gw, gw, gw]
    mats = [jnp.repeat(eye, L, axis=1)] + [jnp.repeat(eye, hd, axis=1)] * 3
    rows = []
    for qty in range(4):
        blk = jnp.concatenate([mats[qty] if c == qty else jnp.zeros((hpg, widths[c]), F32)
                               for c in range(4)], axis=1)
        rows += [blk] * 3
    spread_mat = jnp.concatenate(rows, axis=0).astype(BF16)
    dx = jnp.repeat(d_skip.astype(F32), hd).reshape(1, d_inner)

    kern = functools.partial(_ssd_kernel, hpg=hpg, gw=gw)
    return pl.pallas_call(
        kern,
        out_shape=jax.ShapeDtypeStruct((b, s, d_inner), BF16),
        grid=(b, g, s // L),
        in_specs=[
            pl.BlockSpec((1, L, gw), lambda i, j, c: (i, c, j)),
            pl.BlockSpec((1, L, n), lambda i, j, c: (i, c, nb + j)),
            pl.BlockSpec((1, L, n), lambda i, j, c: (i, c, nb + g + j)),
            pl.BlockSpec((1, L, gw), lambda i, j, c: (i, c, j)),
            pl.BlockSpec((1, hpg, L), lambda i, j, c: (i, j, c)),
            pl.BlockSpec((CONV_WIDTH, gw), lambda i, j, c: (0, j)),
            pl.BlockSpec((1, gw), lambda i, j, c: (0, j)),
            pl.BlockSpec((CONV_WIDTH, n), lambda i, j, c: (0, nb + j)),
            pl.BlockSpec((1, n), lambda i, j, c: (0, nb + j)),
            pl.BlockSpec((CONV_WIDTH, n), lambda i, j, c: (0, nb + g + j)),
            pl.BlockSpec((1, n), lambda i, j, c: (0, nb + g + j)),
            pl.BlockSpec((hpg, 1), lambda i, j, c: (j, 0)),
            pl.BlockSpec((hpg, 1), lambda i, j, c: (j, 0)),
            pl.BlockSpec((1, gw), lambda i, j, c: (0, j)),
            pl.BlockSpec((1, gw), lambda i, j, c: (0, j)),
            pl.BlockSpec((12 * hpg, hpg * L + 3 * gw), lambda i, j, c: (0, 0)),
        ],
        out_specs=pl.BlockSpec((1, L, gw), lambda i, j, c: (i, c, j)),
        scratch_shapes=[pltpu.VMEM((L + SUBLANES, gw + 2 * n), F32), pltpu.VMEM((n, gw), F32)],
        compiler_params=_params(("arbitrary", "arbitrary", "arbitrary")),
        name="ssd_core",
    )(xbc, xbc, xbc, z, dtt, conv_w, conv_b.reshape(1, -1), conv_w, conv_b.reshape(1, -1),
      conv_w, conv_b.reshape(1, -1), dt_bias.reshape(heads, 1), a_log.reshape(heads, 1),
      dx, out_norm.reshape(1, d_inner), spread_mat)


def _attn_kernel(q_ref, k_ref, v_ref, qg_ref, kg_ref, lq1_ref, lk1_ref, lq2_ref, lk2_ref,
                 sg_ref, o_ref, k1_ref, k2_ref, vt_ref, m_ref, l_ref, acc_ref, s_ref, p_ref, a_ref,
                 *, t, lambda_init):
    s_len = k_ref.shape[1]
    n_tiles = s_len // t
    hd = ATT_HEAD_DIM
    qi = pl.program_id(2)
    lane = lax.broadcasted_iota(jnp.int32, (1, 2 * hd), 1)
    lo = lane < hd

    def comp_norm(x, gain):
        x2 = x * x
        s_lo = jnp.sum(jnp.where(lo, x2, 0.0), axis=-1, keepdims=True)
        s_hi = jnp.sum(jnp.where(lo, 0.0, x2), axis=-1, keepdims=True)
        inv = jnp.where(lo, lax.rsqrt(s_lo * (1.0 / hd) + EPS), lax.rsqrt(s_hi * (1.0 / hd) + EPS))
        return x * inv * gain

    @pl.when(qi == 0)
    def _():
        def body(i, carry):
            r0 = pl.multiple_of(i * t, t)
            kn = comp_norm(k_ref[0, pl.ds(r0, t), :].astype(F32), kg_ref[...])
            k1_ref[pl.ds(r0, t), :] = jnp.where(lo, kn, 0.0).astype(BF16)
            k2_ref[pl.ds(r0, t), :] = jnp.where(lo, 0.0, kn).astype(BF16)
            vt_ref[i] = v_ref[0, pl.ds(r0, t), :].astype(F32).T.astype(BF16)
            return carry

        lax.fori_loop(0, s_len // t, body, 0)

    qn = (comp_norm(q_ref[0].astype(F32), qg_ref[...]) * (hd ** -0.5 * LOG2E)).astype(BF16)

    m_ref[...] = jnp.full(m_ref.shape, -jnp.inf, F32)
    l_ref[...] = jnp.zeros(l_ref.shape, F32)
    acc_ref[...] = jnp.zeros(acc_ref.shape, F32)

    def scores(tile, slot):
        r0 = pl.multiple_of(jnp.minimum(tile, n_tiles - 1) * t, t)
        for c, kref in enumerate((k1_ref, k2_ref)):
            s_ref[slot, c] = lax.dot_general(kref[pl.ds(r0, t), :], qn, (((1,), (1,)), ((), ())),
                                             preferred_element_type=F32)

    def softmax(tile, slot, masked):
        for c in range(2):
            sc = s_ref[slot, c]
            if masked:
                key = tile * t + lax.broadcasted_iota(jnp.int32, (t, t), 0)
                qry = qi * t + lax.broadcasted_iota(jnp.int32, (t, t), 1)
                sc = jnp.where(key <= qry, sc, -jnp.inf)
            m_prev = m_ref[c]
            m_new = jnp.maximum(m_prev, jnp.max(sc, axis=0, keepdims=True))
            alpha = jnp.exp2(m_prev - m_new)
            p = jnp.exp2(sc - m_new)
            l_ref[c] = alpha * l_ref[c] + jnp.sum(p, axis=0, keepdims=True)
            m_ref[c] = m_new
            a_ref[slot, c] = alpha
            p_ref[slot, c] = p.astype(BF16)

    def update(tile, slot):
        vt = vt_ref[jnp.minimum(tile, n_tiles - 1)]
        for c in range(2):
            acc_ref[c] = a_ref[slot, c] * acc_ref[c] + jnp.dot(vt, p_ref[slot, c],
                                                               preferred_element_type=F32)

    n_pairs = qi // 2 + 1
    d0 = 2 * (qi // 2)
    scores(d0, 0)
    scores(d0 + 1, 1)
    softmax(d0, 0, True)
    scores(0, 0)
    softmax(d0 + 1, 1, True)
    update(d0, 0)

    def pair(u, carry):
        x = 2 * u - 2
        scores(x + 1, 1)
        softmax(x, 0, False)
        update(jnp.where(u == 1, d0 + 1, x - 1), 1)
        scores(x + 2, 0)
        softmax(x + 1, 1, False)
        update(x, 0)
        return carry

    lax.fori_loop(1, n_pairs, pair, 0)
    update(jnp.where(n_pairs == 1, d0 + 1, d0 - 1), 1)

    lam = (jnp.exp(jnp.sum(lq1_ref[...] * lk1_ref[...], axis=-1, keepdims=True))
           - jnp.exp(jnp.sum(lq2_ref[...] * lk2_ref[...], axis=-1, keepdims=True)) + lambda_init)
    o = acc_ref[0] / l_ref[0] - lam * (acc_ref[1] / l_ref[1])
    o = o * lax.rsqrt(jnp.mean(o * o, axis=0, keepdims=True) + EPS) * sg_ref[...]
    o_ref[0] = (o * (1.0 - lambda_init)).T.astype(o_ref.dtype)


def diff_attention_core(qkv, q_norm, k_norm, lq1, lk1, lq2, lk2, sub_norm, lambda_init):
    b, s, w3 = qkv.shape
    width = w3 // 3
    heads = width // ATT_V_DIM
    hd = ATT_HEAD_DIM
    t = _tile(s, T_ATT, LANES)
    vec = lambda a: a.astype(F32).reshape(1, -1)
    qg = jnp.tile(q_norm.astype(F32), 2).reshape(1, 2 * hd)
    kg = jnp.tile(k_norm.astype(F32), 2).reshape(1, 2 * hd)
    small = lambda width_: pl.BlockSpec((1, width_), lambda i, h, j: (0, 0))
    kern = functools.partial(_attn_kernel, t=t, lambda_init=lambda_init)
    return pl.pallas_call(
        kern,
        out_shape=jax.ShapeDtypeStruct((b, s, width), BF16),
        grid=(b, heads, s // t),
        in_specs=[
            pl.BlockSpec((1, t, 2 * hd), lambda i, h, j: (i, j, h)),
            pl.BlockSpec((1, s, 2 * hd), lambda i, h, j: (i, 0, heads + h)),
            pl.BlockSpec((1, s, ATT_V_DIM), lambda i, h, j: (i, 0, 2 * heads + h)),
            small(2 * hd), small(2 * hd), small(hd), small(hd), small(hd), small(hd),
            pl.BlockSpec((ATT_V_DIM, 1), lambda i, h, j: (0, 0)),
        ],
        out_specs=pl.BlockSpec((1, t, ATT_V_DIM), lambda i, h, j: (i, j, h)),
        scratch_shapes=[
            pltpu.VMEM((s, 2 * hd), BF16), pltpu.VMEM((s, 2 * hd), BF16),
            pltpu.VMEM((s // t, ATT_V_DIM, t), BF16),
            pltpu.VMEM((2, 1, t), F32), pltpu.VMEM((2, 1, t), F32),
            pltpu.VMEM((2, ATT_V_DIM, t), F32),
            pltpu.VMEM((2, 2, t, t), F32), pltpu.VMEM((2, 2, t, t), BF16),
            pltpu.VMEM((2, 2, 1, t), F32),
        ],
        compiler_params=_params(("arbitrary", "arbitrary", "arbitrary")),
        name="diff_attention",
    )(qkv, qkv, qkv, qg, kg, vec(lq1), vec(lk1), vec(lq2), vec(lk2),
      sub_norm.astype(F32).reshape(ATT_V_DIM, 1))


def _route_kernel(x_ref, g_ref, r_ref, xn_ref, info_ref, *, n_experts):
    xn = _rms_rows(x_ref[...], g_ref[...])
    xn_ref[...] = xn.astype(BF16)
    logits = jnp.dot(xn, r_ref[...], preferred_element_type=F32, precision=lax.Precision.HIGHEST)
    lane = lax.broadcasted_iota(jnp.int32, logits.shape, 1)
    logits = jnp.where(lane < n_experts, logits, -jnp.inf)
    m1 = jnp.max(logits, axis=-1, keepdims=True)
    i1 = jnp.min(jnp.where(logits == m1, lane, LANES), axis=-1, keepdims=True)
    rest = jnp.where(lane == i1, -jnp.inf, logits)
    m2 = jnp.max(rest, axis=-1, keepdims=True)
    i2 = jnp.min(jnp.where(rest == m2, lane, LANES), axis=-1, keepdims=True)
    e2 = jnp.exp(m2 - m1)
    g1 = 1.0 / (1.0 + e2)
    g2 = e2 * g1
    info = jnp.where(lane == 0, i1.astype(F32),
                     jnp.where(lane == 1, i2.astype(F32),
                               jnp.where(lane == 2, g1, jnp.where(lane == 3, g2, 0.0))))
    info_ref[...] = info


def moe_route(h, gain, router):
    t, d = h.shape
    e = router.shape[1]
    tm = _tile(t, TM_RES, SUBLANES)
    rpad = jnp.zeros((d, LANES), F32).at[:, :e].set(router.astype(F32))
    return pl.pallas_call(
        functools.partial(_route_kernel, n_experts=e),
        out_shape=(jax.ShapeDtypeStruct((t, d), BF16), jax.ShapeDtypeStruct((t, LANES), F32)),
        grid=(t // tm,),
        in_specs=[
            pl.BlockSpec((tm, d), lambda i: (i, 0)),
            pl.BlockSpec((1, d), lambda i: (0, 0)),
            pl.BlockSpec((d, LANES), lambda i: (0, 0)),
        ],
        out_specs=(pl.BlockSpec((tm, d), lambda i: (i, 0)),
                   pl.BlockSpec((tm, LANES), lambda i: (i, 0))),
        compiler_params=_params(("arbitrary",)),
        name="moe_route",
    )(h, gain.reshape(1, d), rpad)


def _new_weights(te_ref, i):
    prev = te_ref[jnp.maximum(i - 1, 0)]
    return jnp.logical_or(i == 0, te_ref[i] != prev)


def _moe_gateup_kernel(te_ref, nu_ref, x_ref, wg_ref, wu_ref, o_ref, wgb_ref, wub_ref):
    i = pl.program_id(1)

    @pl.when(_new_weights(te_ref, i))
    def _():
        wgb_ref[...] = wg_ref[0].astype(BF16)
        wub_ref[...] = wu_ref[0].astype(BF16)

    @pl.when(i < nu_ref[0])
    def _():
        x = x_ref[...]
        a = jnp.dot(x, wgb_ref[...], preferred_element_type=F32)
        b = jnp.dot(x, wub_ref[...], preferred_element_type=F32)
        o_ref[...] = (_silu(a) * b).astype(o_ref.dtype)

    @pl.when(i >= nu_ref[0])
    def _():
        o_ref[...] = jnp.zeros_like(o_ref)


def _moe_down_kernel(te_ref, nu_ref, h_ref, wd_ref, o_ref, wdb_ref):
    i = pl.program_id(1)

    @pl.when(_new_weights(te_ref, i))
    def _():
        wdb_ref[...] = wd_ref[0].astype(BF16)

    @pl.when(i < nu_ref[0])
    def _():
        o_ref[...] = jnp.dot(h_ref[...], wdb_ref[...], preferred_element_type=F32)

    @pl.when(i >= nu_ref[0])
    def _():
        o_ref[...] = jnp.zeros_like(o_ref)


def moe_experts(xs, tile_expert, n_used, w_gate, w_up, w_down, tm):
    slots, d = xs.shape
    e, _, f = w_gate.shape
    n_tiles = slots // tm
    tf = _tile(f, TF_FFN)
    hmid = pl.pallas_call(
        _moe_gateup_kernel,
        out_shape=jax.ShapeDtypeStruct((slots, f), BF16),
        grid_spec=pltpu.PrefetchScalarGridSpec(
            num_scalar_prefetch=2,
            grid=(f // tf, n_tiles),
            in_specs=[
                pl.BlockSpec((tm, d), lambda j, i, te, nu: (i, 0)),
                pl.BlockSpec((1, d, tf), lambda j, i, te, nu: (te[i], 0, j)),
                pl.BlockSpec((1, d, tf), lambda j, i, te, nu: (te[i], 0, j)),
            ],
            out_specs=pl.BlockSpec((tm, tf), lambda j, i, te, nu: (i, j)),
            scratch_shapes=[pltpu.VMEM((d, tf), BF16), pltpu.VMEM((d, tf), BF16)],
        ),
        compiler_params=_params(("arbitrary", "arbitrary")),
        name="moe_gateup",
    )(tile_expert, n_used, xs, w_gate, w_up)

    tn = _tile(d, TN_MOE_DOWN)
    return pl.pallas_call(
        _moe_down_kernel,
        out_shape=jax.ShapeDtypeStruct((slots, d), F32),
        grid_spec=pltpu.PrefetchScalarGridSpec(
            num_scalar_prefetch=2,
            grid=(d // tn, n_tiles),
            in_specs=[
                pl.BlockSpec((tm, f), lambda j, i, te, nu: (i, 0)),
                pl.BlockSpec((1, f, tn), lambda j, i, te, nu: (te[i], 0, j)),
            ],
            out_specs=pl.BlockSpec((tm, tn), lambda j, i, te, nu: (i, j)),
            scratch_shapes=[pltpu.VMEM((f, tn), BF16)],
        ),
        compiler_params=_params(("arbitrary", "arbitrary")),
        name="moe_down",
    )(tile_expert, n_used, hmid, w_down)


def moe_block(h, gain, router, w_gate, w_up, w_down):
    t, d = h.shape
    e = router.shape[1]
    tm = min(TM_MOE, t)
    xn, info = moe_route(h, gain, router)
    idx = info[:, :TOP_K].astype(jnp.int32)
    gates = info[:, TOP_K:2 * TOP_K]

    e_flat = idx.reshape(-1)
    onehot = (e_flat[:, None] == jnp.arange(e, dtype=jnp.int32)[None, :]).astype(jnp.int32)
    csum = jnp.cumsum(onehot, axis=0)
    rank = jnp.sum((csum - onehot) * onehot, axis=1)
    counts = csum[-1]
    tiles_per_e = (counts + tm - 1) // tm
    tile_end = jnp.cumsum(tiles_per_e)
    starts = (tile_end - tiles_per_e) * tm
    slot = starts[e_flat] + rank
    n_tiles = (TOP_K * t) // tm + e
    n_slots = n_tiles * tm
    token_of_slot = jnp.zeros((n_slots,), jnp.int32).at[slot].set(
        jnp.arange(TOP_K * t, dtype=jnp.int32) // TOP_K)
    tile_expert = jnp.minimum(
        jnp.searchsorted(tile_end, jnp.arange(n_tiles, dtype=jnp.int32), side="right"),
        e - 1).astype(jnp.int32)
    n_used = tile_end[-1:].astype(jnp.int32)

    xs = jnp.take(xn, token_of_slot, axis=0)
    ys = moe_experts(xs, tile_expert, n_used, w_gate, w_up, w_down, tm)
    slot2 = slot.reshape(t, TOP_K)
    out = h
    for k in range(TOP_K):
        out = out + gates[:, k:k + 1] * jnp.take(ys, slot2[:, k], axis=0)
    return out


def mamba_layer(h, b, s, pre_norm, in_proj, conv_w, conv_b, dt_bias, a_log, d_skip, out_norm,
                out_proj):
    d_inner = out_proj.shape[0]
    d_xbc = conv_w.shape[1]
    heads = dt_bias.shape[0]
    w = in_proj.astype(BF16)
    z = norm_matmul(h, pre_norm, w[:, :d_inner], BF16)
    xbc = norm_matmul(h, pre_norm, w[:, d_inner:d_inner + d_xbc], BF16)
    w_dt = jnp.zeros((h.shape[1], LANES), BF16).at[:, :heads].set(w[:, d_inner + d_xbc:])
    dt = norm_matmul(h, pre_norm, w_dt, F32)[:, :heads]
    dtt = jnp.swapaxes(dt.reshape(b, s, heads), 1, 2)
    y = ssd_core(z.reshape(b, s, d_inner), xbc.reshape(b, s, d_xbc), dtt, conv_w.astype(F32),
                 conv_b.astype(F32), dt_bias.astype(F32), a_log.astype(F32), d_skip, out_norm)
    return matmul_residual(y.reshape(b * s, d_inner), out_proj.astype(BF16), h)


def dense_layer(h, pre_norm, w_gate, w_up, w_down):
    mid = norm_gateup(h, pre_norm, w_gate.astype(BF16), w_up.astype(BF16))
    return matmul_residual(mid, w_down.astype(BF16), h)


def attention_layer(h, b, s, pre_norm, w_qkv, q_norm, k_norm, lq1, lk1, lq2, lk2, sub_norm, w_o,
                    lambda_init):
    qkv = norm_matmul(h, pre_norm, w_qkv.astype(BF16), BF16)
    o = diff_attention_core(qkv.reshape(b, s, -1), q_norm, k_norm, lq1, lk1, lq2, lk2, sub_norm,
                            lambda_init)
    return matmul_residual(o.reshape(b * s, -1), w_o.astype(BF16), h)


def kernel(x, ssm_pre_norm, ssm_in_proj, ssm_conv_w, ssm_conv_b, ssm_dt_bias, ssm_A_log, ssm_D,
           ssm_out_norm, ssm_out_proj, dense_pre_norm, dense_w_gate, dense_w_up, dense_w_down,
           att_pre_norm, att_w_qkv, att_q_norm, att_k_norm, att_lambda_q1, att_lambda_k1,
           att_lambda_q2, att_lambda_k2, att_sub_norm, att_w_o, moe_pre_norm, moe_router,
           moe_w_gate, moe_w_up, moe_w_down):
    b, s, d = x.shape
    h = x.reshape(b * s, d)
    for i in range(DEPTH):
        j = i // N_MIXERS
        if i % N_MIXERS == 0:
            h = mamba_layer(h, b, s, ssm_pre_norm[j], ssm_in_proj[j], ssm_conv_w[j],
                            ssm_conv_b[j], ssm_dt_bias[j], ssm_A_log[j], ssm_D[j],
                            ssm_out_norm[j], ssm_out_proj[j])
            h = dense_layer(h, dense_pre_norm[j], dense_w_gate[j], dense_w_up[j],
                            dense_w_down[j])
        else:
            lambda_init = 0.8 - 0.6 * math.exp(-0.3 * i)
            h = attention_layer(h, b, s, att_pre_norm[j], att_w_qkv[j], att_q_norm[j],
                                att_k_norm[j], att_lambda_q1[j], att_lambda_k1[j],
                                att_lambda_q2[j], att_lambda_k2[j], att_sub_norm[j], att_w_o[j],
                                lambda_init)
            h = moe_block(h, moe_pre_norm[j], moe_router[j], moe_w_gate[j], moe_w_up[j],
                          moe_w_down[j])
    return h.reshape(b, s, d)
```

```python
import functools
import math

import jax
import jax.numpy as jnp
from jax import lax
from jax.experimental import pallas as pl
from jax.experimental.pallas import tpu as pltpu

F32 = jnp.float32
BF16 = jnp.bfloat16

EPS = 1e-5
DEPTH = 2
N_MIXERS = 2

SSM_GROUPS = 8
SSM_STATE = 128
CHUNK = 128
CONV_WIDTH = 4
SSM_HEAD_DIM = 64
CONV_TAIL = 16

ATT_HEAD_DIM = 64
ATT_V_DIM = 2 * ATT_HEAD_DIM

TOP_K = 2

LOG2E = math.log2(math.e)

LANES = 128
SUBLANES = 8
VMEM_LIMIT_BYTES = 56 * 1024 * 1024

TM_PROJ = 1024
TN_PROJ = 1024
TM_ROUTE = 512
RES_VMEM_BUDGET = 46 * 1024 * 1024
TF_FFN = 512
SSD_CHUNKS_PER_STEP = 4
TM_MOE = 512
TN_MOE_DOWN = 512
TQ_ATT = 512
TK_ATT = 256


def _tile(dim, pref, quantum=LANES):
    if dim <= pref:
        return dim
    t = (pref // quantum) * quantum
    while t >= quantum:
        if dim % t == 0:
            return t
        t -= quantum
    return dim


def _params(sem):
    return pltpu.CompilerParams(dimension_semantics=sem, vmem_limit_bytes=VMEM_LIMIT_BYTES)


def _rms_rows(x, gain):
    return x * lax.rsqrt(jnp.mean(x * x, axis=-1, keepdims=True) + EPS) * gain


def _silu(x):
    h = 0.5 * x
    return h + h * jnp.tanh(h)


def _split3(v):
    hi = v.astype(BF16).astype(F32)
    r = v - hi
    mid = r.astype(BF16).astype(F32)
    return [hi, mid, r - mid]


def _norm_mm_kernel(x_ref, g_ref, w_ref, o_ref, xn_ref):
    @pl.when(pl.program_id(1) == 0)
    def _():
        xn_ref[...] = _rms_rows(x_ref[...], g_ref[...]).astype(BF16)

    o_ref[...] = jnp.dot(xn_ref[...], w_ref[...], preferred_element_type=F32).astype(o_ref.dtype)


def norm_matmul(x, gain, w, out_dtype):
    t, d = x.shape
    n = w.shape[1]
    tm, tn = _tile(t, TM_PROJ, SUBLANES), _tile(n, TN_PROJ)
    return pl.pallas_call(
        _norm_mm_kernel,
        out_shape=jax.ShapeDtypeStruct((t, n), out_dtype),
        grid=(t // tm, n // tn),
        in_specs=[
            pl.BlockSpec((tm, d), lambda i, j: (i, 0)),
            pl.BlockSpec((1, d), lambda i, j: (0, 0)),
            pl.BlockSpec((d, tn), lambda i, j: (0, j)),
        ],
        out_specs=pl.BlockSpec((tm, tn), lambda i, j: (i, j)),
        scratch_shapes=[pltpu.VMEM((tm, d), BF16)],
        compiler_params=_params(("arbitrary", "arbitrary")),
        name="norm_matmul",
    )(x, gain.reshape(1, d), w)


def _norm_gateup_kernel(x_ref, g_ref, wg_ref, wu_ref, o_ref, xn_ref):
    @pl.when(pl.program_id(1) == 0)
    def _():
        xn_ref[...] = _rms_rows(x_ref[...], g_ref[...]).astype(BF16)

    xn = xn_ref[...]
    a = jnp.dot(xn, wg_ref[...], preferred_element_type=F32)
    b = jnp.dot(xn, wu_ref[...], preferred_element_type=F32)
    o_ref[...] = (_silu(a) * b).astype(o_ref.dtype)


def norm_gateup(x, gain, wg, wu):
    t, d = x.shape
    f = wg.shape[1]
    tm, tf = _tile(t, TM_PROJ, SUBLANES), _tile(f, TF_FFN)
    return pl.pallas_call(
        _norm_gateup_kernel,
        out_shape=jax.ShapeDtypeStruct((t, f), BF16),
        grid=(t // tm, f // tf),
        in_specs=[
            pl.BlockSpec((tm, d), lambda i, j: (i, 0)),
            pl.BlockSpec((1, d), lambda i, j: (0, 0)),
            pl.BlockSpec((d, tf), lambda i, j: (0, j)),
            pl.BlockSpec((d, tf), lambda i, j: (0, j)),
        ],
        out_specs=pl.BlockSpec((tm, tf), lambda i, j: (i, j)),
        scratch_shapes=[pltpu.VMEM((tm, d), BF16)],
        compiler_params=_params(("arbitrary", "arbitrary")),
        name="norm_gateup",
    )(x, gain.reshape(1, d), wg, wu)


def _mm_res_kernel(a_ref, w_ref, r_ref, o_ref):
    o_ref[...] = r_ref[...] + jnp.dot(a_ref[...], w_ref[...], preferred_element_type=F32)


def matmul_residual(a, w, res):
    t, k = a.shape
    n = w.shape[1]
    tm = t
    for cand in (1024, 512, 256, 128):
        need = k * n * 2 + 2 * cand * k * 2 + 4 * cand * n * 4 + cand * n * 4
        if t % cand == 0 and need <= RES_VMEM_BUDGET:
            tm = cand
            break
    return pl.pallas_call(
        _mm_res_kernel,
        out_shape=jax.ShapeDtypeStruct((t, n), F32),
        grid=(t // tm,),
        in_specs=[
            pl.BlockSpec((tm, k), lambda i: (i, 0)),
            pl.BlockSpec((k, n), lambda i: (0, 0), pipeline_mode=pl.Buffered(1)),
            pl.BlockSpec((tm, n), lambda i: (i, 0)),
        ],
        out_specs=pl.BlockSpec((tm, n), lambda i: (i, 0)),
        compiler_params=_params(("arbitrary",)),
        name="matmul_residual",
    )(a, w, res)


def _ssd_kernel(x_ref, bm_ref, cm_ref, z_ref, dtt_ref, cwx_ref, cbx_ref, cwb_ref, cbb_ref,
                cwc_ref, cbc_ref, dtb_ref, alog_ref, dx_ref, gn_ref, e_ref, sh_ref,
                o_ref, pad_ref, h_ref, *, hpg, gw, nc):
    L = CHUNK
    n = SSM_STATE
    hd = gw // hpg
    rows = nc * L
    c = pl.program_id(2)
    tail = CONV_TAIL

    @pl.when(c == 0)
    def _():
        pad_ref[0:tail, :] = jnp.zeros((tail, gw + 2 * n), BF16)
        h_ref[...] = jnp.zeros_like(h_ref)

    @pl.when(c > 0)
    def _():
        pad_ref[0:tail, :] = pad_ref[rows:rows + tail, :]

    pad_ref[tail:tail + rows, 0:gw] = x_ref[0]
    pad_ref[tail:tail + rows, gw:gw + n] = bm_ref[0]
    pad_ref[tail:tail + rows, gw + n:gw + 2 * n] = cm_ref[0]

    def conv(shifted, t0, col0, width, w_ref, b_ref):
        acc = b_ref[...] + w_ref[CONV_WIDTH - 1:CONV_WIDTH, :] * pad_ref[
            t0 + tail:t0 + tail + L, col0:col0 + width].astype(F32)
        for k in range(CONV_WIDTH - 1):
            acc = acc + w_ref[k:k + 1, :] * shifted[k * L:(k + 1) * L, col0:col0 + width]
        return _silu(acc)

    raw = dtt_ref[0] + dtb_ref[...]
    dt_all = jnp.maximum(raw, 0.0) + jnp.log1p(jnp.exp(-jnp.abs(raw)))
    a_all = dt_all * (-jnp.exp(alog_ref[...]))
    row = lax.broadcasted_iota(jnp.int32, (L, L), 0)
    col = lax.broadcasted_iota(jnp.int32, (L, L), 1)
    upper = (row <= col).astype(F32)
    causal = row >= col
    lane = lax.broadcasted_iota(jnp.int32, (1, 2 * hd), 1)
    lo = lane < hd

    for ci in range(nc):
        t0 = ci * L
        shifted = jnp.dot(sh_ref[...], pad_ref[t0:t0 + tail + L, :],
                          preferred_element_type=F32)
        xs = conv(shifted, t0, 0, gw, cwx_ref, cbx_ref)
        bc = conv(shifted, t0, gw, n, cwb_ref, cbb_ref)
        cc = conv(shifted, t0, gw + n, n, cwc_ref, cbc_ref)

        dt = dt_all[:, t0:t0 + L]
        acs = jnp.dot(a_all[:, t0:t0 + L], upper, preferred_element_type=F32,
                      precision=lax.Precision.HIGHEST)
        a_last = acs[:, L - 1:L]
        dte = jnp.exp(a_last - acs)
        eacs = jnp.exp(acs)

        parts = _split3(acs) + _split3(dt) + _split3(dt * dte) + _split3(eacs)
        stacked = jnp.concatenate(parts, axis=0).T.astype(BF16)
        spread = jnp.dot(stacked, e_ref[...], preferred_element_type=F32)
        acs_col = spread[:, :hpg * L]
        dt_x = spread[:, hpg * L:hpg * L + gw]
        w_x = spread[:, hpg * L + gw:hpg * L + 2 * gw]
        eacs_x = spread[:, hpg * L + 2 * gw:]

        xdt = xs * dt_x
        bct = bc.T.astype(BF16)
        ccb = cc.astype(BF16)
        cb = jnp.dot(ccb, bct, preferred_element_type=F32)

        y_parts = []
        for q in range(hpg // 2):
            ms = []
            for j in (2 * q, 2 * q + 1):
                seg = acs_col[:, j * L:(j + 1) * L] - acs[j:j + 1, :]
                dec = jnp.exp(jnp.where(causal, seg, -jnp.inf))
                ms.append((cb * dec).astype(BF16))
            mcat = jnp.concatenate(ms, axis=1)
            xq = xdt[:, q * 2 * hd:(q + 1) * 2 * hd]
            rhs = jnp.concatenate([jnp.where(lo, xq, 0.0), jnp.where(lo, 0.0, xq)], axis=0)
            y_parts.append(jnp.dot(mcat, rhs.astype(BF16), preferred_element_type=F32))
        y_diag = jnp.concatenate(y_parts, axis=1)

        h_in = h_ref[...]
        y_off = jnp.dot(ccb, h_in.astype(BF16), preferred_element_type=F32) * eacs_x
        states = jnp.dot(bct, (xs * w_x).astype(BF16), preferred_element_type=F32)
        h_ref[...] = h_in * eacs_x[L - 1:L, :] + states

        y = y_diag + y_off + dx_ref[...] * xs
        gated = y * _silu(z_ref[0, t0:t0 + L, :].astype(F32))
        o_ref[0, t0:t0 + L, :] = _rms_rows(gated, gn_ref[...]).astype(o_ref.dtype)


def ssd_core(z, xbc, dtt, conv_w, conv_b, dt_bias, a_log, d_skip, out_norm):
    b, s, d_inner = z.shape
    g = SSM_GROUPS
    n = SSM_STATE
    heads = dt_bias.shape[0]
    hpg = heads // g
    gw = d_inner // g
    hd = gw // hpg
    L = CHUNK
    nb = d_inner // n
    assert gw % LANES == 0 and hpg % 2 == 0 and 2 * hd == LANES and s % L == 0

    eye = jnp.eye(hpg, dtype=F32)
    widths = [hpg * L, gw, gw, gw]
    mats = [jnp.repeat(eye, L, axis=1)] + [jnp.repeat(eye, hd, axis=1)] * 3
    spread_rows = []
    for qty in range(4):
        blk = jnp.concatenate([mats[qty] if c == qty else jnp.zeros((hpg, widths[c]), F32)
                               for c in range(4)], axis=1)
        spread_rows += [blk] * 3
    spread_mat = jnp.concatenate(spread_rows, axis=0).astype(BF16)
    dx = jnp.repeat(d_skip.astype(F32), hd).reshape(1, d_inner)

    nc = SSD_CHUNKS_PER_STEP if (s // L) % SSD_CHUNKS_PER_STEP == 0 else 1
    rows = nc * L
    out_row = jnp.arange((CONV_WIDTH - 1) * L, dtype=jnp.int32)
    src_row = out_row % L + CONV_TAIL - (CONV_WIDTH - 1) + out_row // L
    shift_mat = (src_row[:, None] == jnp.arange(CONV_TAIL + L, dtype=jnp.int32)[None, :]).astype(BF16)
    kern = functools.partial(_ssd_kernel, hpg=hpg, gw=gw, nc=nc)
    return pl.pallas_call(
        kern,
        out_shape=jax.ShapeDtypeStruct((b, s, d_inner), BF16),
        grid=(b, g, s // rows),
        in_specs=[
            pl.BlockSpec((1, rows, gw), lambda i, j, c: (i, c, j)),
            pl.BlockSpec((1, rows, n), lambda i, j, c: (i, c, nb + j)),
            pl.BlockSpec((1, rows, n), lambda i, j, c: (i, c, nb + g + j)),
            pl.BlockSpec((1, rows, gw), lambda i, j, c: (i, c, j)),
            pl.BlockSpec((1, hpg, rows), lambda i, j, c: (i, j, c)),
            pl.BlockSpec((CONV_WIDTH, gw), lambda i, j, c: (0, j)),
            pl.BlockSpec((1, gw), lambda i, j, c: (0, j)),
            pl.BlockSpec((CONV_WIDTH, n), lambda i, j, c: (0, nb + j)),
            pl.BlockSpec((1, n), lambda i, j, c: (0, nb + j)),
            pl.BlockSpec((CONV_WIDTH, n), lambda i, j, c: (0, nb + g + j)),
            pl.BlockSpec((1, n), lambda i, j, c: (0, nb + g + j)),
            pl.BlockSpec((hpg, 1), lambda i, j, c: (j, 0)),
            pl.BlockSpec((hpg, 1), lambda i, j, c: (j, 0)),
            pl.BlockSpec((1, gw), lambda i, j, c: (0, j)),
            pl.BlockSpec((1, gw), lambda i, j, c: (0, j)),
            pl.BlockSpec((12 * hpg, hpg * L + 3 * gw), lambda i, j, c: (0, 0)),
            pl.BlockSpec(((CONV_WIDTH - 1) * L, CONV_TAIL + L), lambda i, j, c: (0, 0)),
        ],
        out_specs=pl.BlockSpec((1, rows, gw), lambda i, j, c: (i, c, j)),
        scratch_shapes=[pltpu.VMEM((rows + CONV_TAIL, gw + 2 * n), BF16), pltpu.VMEM((n, gw), F32)],
        compiler_params=_params(("arbitrary", "arbitrary", "arbitrary")),
        name="ssd_core",
    )(xbc, xbc, xbc, z, dtt, conv_w, conv_b.reshape(1, -1), conv_w, conv_b.reshape(1, -1),
      conv_w, conv_b.reshape(1, -1), dt_bias.reshape(heads, 1), a_log.reshape(heads, 1),
      dx, out_norm.reshape(1, d_inner), spread_mat, shift_mat)


def _attn_kernel(q_ref, k_ref, v_ref, qg_ref, kg_ref, lq1_ref, lk1_ref, lq2_ref, lk2_ref,
                 sg_ref, o_ref, k1_ref, k2_ref, vt_ref, m_ref, l_ref, acc_ref, s_ref, p_ref, a_ref,
                 *, tq, t, lambda_init):
    s_len = k_ref.shape[1]
    n_tiles = s_len // t
    hd = ATT_HEAD_DIM
    qi = pl.program_id(2)
    lane = lax.broadcasted_iota(jnp.int32, (1, 2 * hd), 1)
    lo = lane < hd

    def comp_norm(x, gain):
        x2 = x * x
        s_lo = jnp.sum(jnp.where(lo, x2, 0.0), axis=-1, keepdims=True)
        s_hi = jnp.sum(jnp.where(lo, 0.0, x2), axis=-1, keepdims=True)
        inv = jnp.where(lo, lax.rsqrt(s_lo * (1.0 / hd) + EPS), lax.rsqrt(s_hi * (1.0 / hd) + EPS))
        return x * inv * gain

    @pl.when(qi == 0)
    def _():
        def body(i, carry):
            r0 = pl.multiple_of(i * t, t)
            kn = comp_norm(k_ref[0, pl.ds(r0, t), :].astype(F32), kg_ref[...])
            k1_ref[pl.ds(r0, t), :] = jnp.where(lo, kn, 0.0).astype(BF16)
            k2_ref[pl.ds(r0, t), :] = jnp.where(lo, 0.0, kn).astype(BF16)
            vt_ref[i] = v_ref[0, pl.ds(r0, t), :].astype(F32).T.astype(BF16)
            return carry

        lax.fori_loop(0, s_len // t, body, 0)

    qn = (comp_norm(q_ref[0].astype(F32), qg_ref[...]) * (hd ** -0.5 * LOG2E)).astype(BF16)

    m_ref[...] = jnp.full(m_ref.shape, -jnp.inf, F32)
    l_ref[...] = jnp.zeros(l_ref.shape, F32)
    acc_ref[...] = jnp.zeros(acc_ref.shape, F32)

    def scores(tile, slot):
        r0 = pl.multiple_of(jnp.minimum(tile, n_tiles - 1) * t, t)
        for c, kref in enumerate((k1_ref, k2_ref)):
            s_ref[slot, c] = lax.dot_general(kref[pl.ds(r0, t), :], qn, (((1,), (1,)), ((), ())),
                                             preferred_element_type=F32)

    def softmax(tile, slot, masked):
        for c in range(2):
            sc = s_ref[slot, c]
            if masked:
                key = tile * t + lax.broadcasted_iota(jnp.int32, (t, tq), 0)
                qry = qi * tq + lax.broadcasted_iota(jnp.int32, (t, tq), 1)
                sc = jnp.where(key <= qry, sc, -jnp.inf)
            m_prev = m_ref[c]
            m_new = jnp.maximum(m_prev, jnp.max(sc, axis=0, keepdims=True))
            alpha = jnp.exp2(m_prev - m_new)
            p = jnp.exp2(sc - m_new)
            l_ref[c] = alpha * l_ref[c] + jnp.sum(p, axis=0, keepdims=True)
            m_ref[c] = m_new
            a_ref[slot, c] = alpha
            p_ref[slot, c] = p.astype(BF16)

    def update(tile, slot):
        vt = vt_ref[jnp.minimum(tile, n_tiles - 1)]
        for c in range(2):
            acc_ref[c] = a_ref[slot, c] * acc_ref[c] + jnp.dot(vt, p_ref[slot, c],
                                                               preferred_element_type=F32)

    d0 = 2 * ((qi * tq) // (2 * t))
    n_pairs = d0 // 2 + 1
    scores(d0, 0)
    scores(d0 + 1, 1)
    softmax(d0, 0, True)
    scores(0, 0)
    softmax(d0 + 1, 1, True)
    update(d0, 0)

    def pair(u, carry):
        x = 2 * u - 2
        scores(x + 1, 1)
        softmax(x, 0, False)
        update(jnp.where(u == 1, d0 + 1, x - 1), 1)
        scores(x + 2, 0)
        softmax(x + 1, 1, False)
        update(x, 0)
        return carry

    lax.fori_loop(1, n_pairs, pair, 0)
    update(jnp.where(n_pairs == 1, d0 + 1, d0 - 1), 1)

    lam = (jnp.exp(jnp.sum(lq1_ref[...] * lk1_ref[...], axis=-1, keepdims=True))
           - jnp.exp(jnp.sum(lq2_ref[...] * lk2_ref[...], axis=-1, keepdims=True)) + lambda_init)
    o = acc_ref[0] / l_ref[0] - lam * (acc_ref[1] / l_ref[1])
    o = o * lax.rsqrt(jnp.mean(o * o, axis=0, keepdims=True) + EPS) * sg_ref[...]
    o_ref[0] = (o * (1.0 - lambda_init)).T.astype(o_ref.dtype)


def diff_attention_core(qkv, q_norm, k_norm, lq1, lk1, lq2, lk2, sub_norm, lambda_init):
    b, s, w3 = qkv.shape
    width = w3 // 3
    heads = width // ATT_V_DIM
    hd = ATT_HEAD_DIM
    tq = _tile(s, TQ_ATT, LANES)
    t = _tile(s, TK_ATT, LANES)
    assert tq % t == 0 and (2 * t) % tq == 0
    vec = lambda a: a.astype(F32).reshape(1, -1)
    qg = jnp.tile(q_norm.astype(F32), 2).reshape(1, 2 * hd)
    kg = jnp.tile(k_norm.astype(F32), 2).reshape(1, 2 * hd)
    small = lambda width_: pl.BlockSpec((1, width_), lambda i, h, j: (0, 0))
    kern = functools.partial(_attn_kernel, tq=tq, t=t, lambda_init=lambda_init)
    return pl.pallas_call(
        kern,
        out_shape=jax.ShapeDtypeStruct((b, s, width), BF16),
        grid=(b, heads, s // tq),
        in_specs=[
            pl.BlockSpec((1, tq, 2 * hd), lambda i, h, j: (i, j, h)),
            pl.BlockSpec((1, s, 2 * hd), lambda i, h, j: (i, 0, heads + h)),
            pl.BlockSpec((1, s, ATT_V_DIM), lambda i, h, j: (i, 0, 2 * heads + h)),
            small(2 * hd), small(2 * hd), small(hd), small(hd), small(hd), small(hd),
            pl.BlockSpec((ATT_V_DIM, 1), lambda i, h, j: (0, 0)),
        ],
        out_specs=pl.BlockSpec((1, tq, ATT_V_DIM), lambda i, h, j: (i, j, h)),
        scratch_shapes=[
            pltpu.VMEM((s, 2 * hd), BF16), pltpu.VMEM((s, 2 * hd), BF16),
            pltpu.VMEM((s // t, ATT_V_DIM, t), BF16),
            pltpu.VMEM((2, 1, tq), F32), pltpu.VMEM((2, 1, tq), F32),
            pltpu.VMEM((2, ATT_V_DIM, tq), F32),
            pltpu.VMEM((2, 2, t, tq), F32), pltpu.VMEM((2, 2, t, tq), BF16),
            pltpu.VMEM((2, 2, 1, tq), F32),
        ],
        compiler_params=_params(("arbitrary", "arbitrary", "arbitrary")),
        name="diff_attention",
    )(qkv, qkv, qkv, qg, kg, vec(lq1), vec(lk1), vec(lq2), vec(lk2),
      sub_norm.astype(F32).reshape(ATT_V_DIM, 1))


def _route_kernel(x_ref, g_ref, r_ref, xn_ref, info_ref, cnt_ref, run_ref, *, n_experts):
    @pl.when(pl.program_id(0) == 0)
    def _():
        run_ref[...] = jnp.zeros_like(run_ref)

    xn = _rms_rows(x_ref[...], g_ref[...])
    xn_ref[...] = xn.astype(BF16)
    logits = jnp.dot(xn, r_ref[...], preferred_element_type=F32, precision=lax.Precision.HIGHEST)
    lane = lax.broadcasted_iota(jnp.int32, logits.shape, 1)
    logits = jnp.where(lane < n_experts, logits, -jnp.inf)
    m1 = jnp.max(logits, axis=-1, keepdims=True)
    i1 = jnp.min(jnp.where(logits == m1, lane, LANES), axis=-1, keepdims=True)
    rest = jnp.where(lane == i1, -jnp.inf, logits)
    m2 = jnp.max(rest, axis=-1, keepdims=True)
    i2 = jnp.min(jnp.where(rest == m2, lane, LANES), axis=-1, keepdims=True)
    e2 = jnp.exp(m2 - m1)
    g1 = 1.0 / (1.0 + e2)
    g2 = e2 * g1
    tm = logits.shape[0]
    oh1 = (lane == i1).astype(F32)
    oh2 = (lane == i2).astype(F32)
    both = oh1 + oh2
    row = lax.broadcasted_iota(jnp.int32, (tm, tm), 0)
    col = lax.broadcasted_iota(jnp.int32, (tm, tm), 1)
    before = jnp.dot((col < row).astype(BF16), both.astype(BF16),
                     preferred_element_type=F32) + run_ref[0:1, :]
    r1 = jnp.sum(before * oh1, axis=-1, keepdims=True)
    r2 = jnp.sum(before * oh2, axis=-1, keepdims=True)
    run_ref[...] = run_ref[...] + jnp.sum(both, axis=0, keepdims=True)
    cnt_ref[...] = run_ref[...]
    info = jnp.zeros_like(logits)
    for k, val in enumerate((i1.astype(F32), i2.astype(F32), g1, g2, r1, r2)):
        info = jnp.where(lane == k, val, info)
    info_ref[...] = info


def moe_route(h, gain, router):
    t, d = h.shape
    e = router.shape[1]
    tm = _tile(t, TM_ROUTE, SUBLANES)
    rpad = jnp.zeros((d, LANES), F32).at[:, :e].set(router.astype(F32))
    return pl.pallas_call(
        functools.partial(_route_kernel, n_experts=e),
        out_shape=(jax.ShapeDtypeStruct((t, d), BF16), jax.ShapeDtypeStruct((t, LANES), F32),
                   jax.ShapeDtypeStruct((SUBLANES, LANES), F32)),
        grid=(t // tm,),
        in_specs=[
            pl.BlockSpec((tm, d), lambda i: (i, 0)),
            pl.BlockSpec((1, d), lambda i: (0, 0)),
            pl.BlockSpec((d, LANES), lambda i: (0, 0)),
        ],
        out_specs=(pl.BlockSpec((tm, d), lambda i: (i, 0)),
                   pl.BlockSpec((tm, LANES), lambda i: (i, 0)),
                   pl.BlockSpec((SUBLANES, LANES), lambda i: (0, 0))),
        scratch_shapes=[pltpu.VMEM((SUBLANES, LANES), F32)],
        compiler_params=_params(("arbitrary",)),
        name="moe_route",
    )(h, gain.reshape(1, d), rpad)


def _new_weights(te_ref, i):
    prev = te_ref[jnp.maximum(i - 1, 0)]
    return jnp.logical_or(i == 0, te_ref[i] != prev)


def _moe_gateup_kernel(te_ref, nu_ref, x_ref, wg_ref, wu_ref, o_ref, wgb_ref, wub_ref):
    i = pl.program_id(1)

    @pl.when(_new_weights(te_ref, i))
    def _():
        wgb_ref[...] = wg_ref[0].astype(BF16)
        wub_ref[...] = wu_ref[0].astype(BF16)

    @pl.when(i < nu_ref[0])
    def _():
        x = x_ref[...]
        a = jnp.dot(x, wgb_ref[...], preferred_element_type=F32)
        b = jnp.dot(x, wub_ref[...], preferred_element_type=F32)
        o_ref[...] = (_silu(a) * b).astype(o_ref.dtype)

    @pl.when(i >= nu_ref[0])
    def _():
        o_ref[...] = jnp.zeros_like(o_ref)


def _moe_down_kernel(te_ref, nu_ref, h_ref, wd_ref, o_ref, wdb_ref):
    i = pl.program_id(1)

    @pl.when(_new_weights(te_ref, i))
    def _():
        wdb_ref[...] = wd_ref[0].astype(BF16)

    @pl.when(i < nu_ref[0])
    def _():
        o_ref[...] = jnp.dot(h_ref[...], wdb_ref[...],
                             preferred_element_type=F32).astype(o_ref.dtype)

    @pl.when(i >= nu_ref[0])
    def _():
        o_ref[...] = jnp.zeros_like(o_ref)


def moe_experts(xs, tile_expert, n_used, w_gate, w_up, w_down, tm):
    slots, d = xs.shape
    e, _, f = w_gate.shape
    n_tiles = slots // tm
    tf = _tile(f, TF_FFN)
    hmid = pl.pallas_call(
        _moe_gateup_kernel,
        out_shape=jax.ShapeDtypeStruct((slots, f), BF16),
        grid_spec=pltpu.PrefetchScalarGridSpec(
            num_scalar_prefetch=2,
            grid=(f // tf, n_tiles),
            in_specs=[
                pl.BlockSpec((tm, d), lambda j, i, te, nu: (i, 0)),
                pl.BlockSpec((1, d, tf), lambda j, i, te, nu: (te[i], 0, j)),
                pl.BlockSpec((1, d, tf), lambda j, i, te, nu: (te[i], 0, j)),
            ],
            out_specs=pl.BlockSpec((tm, tf), lambda j, i, te, nu: (i, j)),
            scratch_shapes=[pltpu.VMEM((d, tf), BF16), pltpu.VMEM((d, tf), BF16)],
        ),
        compiler_params=_params(("arbitrary", "arbitrary")),
        name="moe_gateup",
    )(tile_expert, n_used, xs, w_gate, w_up)

    tn = _tile(d, TN_MOE_DOWN)
    return pl.pallas_call(
        _moe_down_kernel,
        out_shape=jax.ShapeDtypeStruct((slots, d), BF16),
        grid_spec=pltpu.PrefetchScalarGridSpec(
            num_scalar_prefetch=2,
            grid=(d // tn, n_tiles),
            in_specs=[
                pl.BlockSpec((tm, f), lambda j, i, te, nu: (i, 0)),
                pl.BlockSpec((1, f, tn), lambda j, i, te, nu: (te[i], 0, j)),
            ],
            out_specs=pl.BlockSpec((tm, tn), lambda j, i, te, nu: (i, j)),
            scratch_shapes=[pltpu.VMEM((f, tn), BF16)],
        ),
        compiler_params=_params(("arbitrary", "arbitrary")),
        name="moe_down",
    )(tile_expert, n_used, hmid, w_down)


def moe_block(h, gain, router, w_gate, w_up, w_down):
    t, d = h.shape
    e = router.shape[1]
    tm = min(TM_MOE, t)
    xn, info, cnt = moe_route(h, gain, router)
    idx = info[:, :TOP_K].astype(jnp.int32)
    gates = info[:, TOP_K:2 * TOP_K]
    rank = info[:, 2 * TOP_K:3 * TOP_K].astype(jnp.int32)
    counts = cnt[0, :e].astype(jnp.int32)

    tiles_per_e = (counts + tm - 1) // tm
    tile_end = jnp.cumsum(tiles_per_e)
    starts = (tile_end - tiles_per_e) * tm
    slot = (jnp.take(starts, idx.reshape(-1)) + rank.reshape(-1))
    n_tiles = (TOP_K * t) // tm + e
    n_slots = n_tiles * tm
    token_of_slot = jnp.zeros((n_slots,), jnp.int32).at[slot].set(
        jnp.arange(TOP_K * t, dtype=jnp.int32) // TOP_K)
    tile_expert = jnp.minimum(
        jnp.searchsorted(tile_end, jnp.arange(n_tiles, dtype=jnp.int32), side="right"),
        e - 1).astype(jnp.int32)
    n_used = tile_end[-1:].astype(jnp.int32)

    xs = jnp.take(xn, token_of_slot, axis=0)
    ys = moe_experts(xs, tile_expert, n_used, w_gate, w_up, w_down, tm)
    slot2 = slot.reshape(t, TOP_K)
    out = h
    for k in range(TOP_K):
        out = out + gates[:, k:k + 1] * jnp.take(ys, slot2[:, k], axis=0).astype(F32)
    return out


def mamba_layer(h, b, s, pre_norm, in_proj, conv_w, conv_b, dt_bias, a_log, d_skip, out_norm,
                out_proj):
    d_inner = out_proj.shape[0]
    d_xbc = conv_w.shape[1]
    heads = dt_bias.shape[0]
    w = in_proj.astype(BF16)
    z = norm_matmul(h, pre_norm, w[:, :d_inner], BF16)
    xbc = norm_matmul(h, pre_norm, w[:, d_inner:d_inner + d_xbc], BF16)
    w_dt = jnp.zeros((h.shape[1], LANES), BF16).at[:, :heads].set(w[:, d_inner + d_xbc:])
    dt = norm_matmul(h, pre_norm, w_dt, F32)[:, :heads]
    dtt = jnp.swapaxes(dt.reshape(b, s, heads), 1, 2)
    y = ssd_core(z.reshape(b, s, d_inner), xbc.reshape(b, s, d_xbc), dtt, conv_w.astype(F32),
                 conv_b.astype(F32), dt_bias.astype(F32), a_log.astype(F32), d_skip, out_norm)
    return matmul_residual(y.reshape(b * s, d_inner), out_proj.astype(BF16), h)


def dense_layer(h, pre_norm, w_gate, w_up, w_down):
    mid = norm_gateup(h, pre_norm, w_gate.astype(BF16), w_up.astype(BF16))
    return matmul_residual(mid, w_down.astype(BF16), h)


def attention_layer(h, b, s, pre_norm, w_qkv, q_norm, k_norm, lq1, lk1, lq2, lk2, sub_norm, w_o,
                    lambda_init):
    qkv = norm_matmul(h, pre_norm, w_qkv.astype(BF16), BF16)
    o = diff_attention_core(qkv.reshape(b, s, -1), q_norm, k_norm, lq1, lk1, lq2, lk2, sub_norm,
                            lambda_init)
    return matmul_residual(o.reshape(b * s, -1), w_o.astype(BF16), h)


def kernel(x, ssm_pre_norm, ssm_in_proj, ssm_conv_w, ssm_conv_b, ssm_dt_bias, ssm_A_log, ssm_D,
           ssm_out_norm, ssm_out_proj, dense_pre_norm, dense_w_gate, dense_w_up, dense_w_down,
           att_pre_norm, att_w_qkv, att_q_norm, att_k_norm, att_lambda_q1, att_lambda_k1,
           att_lambda_q2, att_lambda_k2, att_sub_norm, att_w_o, moe_pre_norm, moe_router,
           moe_w_gate, moe_w_up, moe_w_down):
    b, s, d = x.shape
    h = x.reshape(b * s, d)
    for i in range(DEPTH):
        j = i // N_MIXERS
        if i % N_MIXERS == 0:
            h = mamba_layer(h, b, s, ssm_pre_norm[j], ssm_in_proj[j], ssm_conv_w[j],
                            ssm_conv_b[j], ssm_dt_bias[j], ssm_A_log[j], ssm_D[j],
                            ssm_out_norm[j], ssm_out_proj[j])
            h = dense_layer(h, dense_pre_norm[j], dense_w_gate[j], dense_w_up[j],
                            dense_w_down[j])
        else:
            lambda_init = 0.8 - 0.6 * math.exp(-0.3 * i)
            h = attention_layer(h, b, s, att_pre_norm[j], att_w_qkv[j], att_q_norm[j],
                                att_k_norm[j], att_lambda_q1[j], att_lambda_k1[j],
                                att_lambda_q2[j], att_lambda_k2[j], att_sub_norm[j], att_w_o[j],
                                lambda_init)
            h = moe_block(h, moe_pre_norm[j], moe_router[j], moe_w_gate[j], moe_w_up[j],
                          moe_w_down[j])
    return h.reshape(b, s, d)
```

```python
import functools
import math

import jax
import jax.numpy as jnp
from jax import lax
from jax.experimental import pallas as pl
from jax.experimental.pallas import tpu as pltpu

F32 = jnp.float32
BF16 = jnp.bfloat16

EPS = 1e-5
DEPTH = 2
N_MIXERS = 2

SSM_GROUPS = 8
SSM_STATE = 128
CHUNK = 128
CONV_WIDTH = 4
SSM_HEAD_DIM = 64
CONV_TAIL = 16

ATT_HEAD_DIM = 64
ATT_V_DIM = 2 * ATT_HEAD_DIM

TOP_K = 2

LOG2E = math.log2(math.e)

LANES = 128
SUBLANES = 8
VMEM_LIMIT_BYTES = 56 * 1024 * 1024

TM_PROJ = 1024
TN_PROJ = 1024
TM_ROUTE = 512
RES_VMEM_BUDGET = 46 * 1024 * 1024
TF_FFN = 512
SSD_CHUNKS_PER_STEP = 8
TM_MOE = 512
TN_MOE_DOWN = 512
TQ_ATT = 512
TK_ATT = 256


def _tile(dim, pref, quantum=LANES):
    if dim <= pref:
        return dim
    t = (pref // quantum) * quantum
    while t >= quantum:
        if dim % t == 0:
            return t
        t -= quantum
    return dim


def _params(sem):
    return pltpu.CompilerParams(dimension_semantics=sem, vmem_limit_bytes=VMEM_LIMIT_BYTES)


def _rms_rows(x, gain):
    return x * lax.rsqrt(jnp.mean(x * x, axis=-1, keepdims=True) + EPS) * gain


def _silu(x):
    h = 0.5 * x
    return h + h * jnp.tanh(h)


def _split3(v):
    hi = v.astype(BF16).astype(F32)
    r = v - hi
    mid = r.astype(BF16).astype(F32)
    return [hi, mid, r - mid]


def _norm_mm_kernel(x_ref, g_ref, w_ref, o_ref, xn_ref):
    @pl.when(pl.program_id(1) == 0)
    def _():
        xn_ref[...] = _rms_rows(x_ref[...], g_ref[...]).astype(BF16)

    o_ref[...] = jnp.dot(xn_ref[...], w_ref[...], preferred_element_type=F32).astype(o_ref.dtype)


def norm_matmul(x, gain, w, out_dtype):
    t, d = x.shape
    n = w.shape[1]
    tm, tn = _tile(t, TM_PROJ, SUBLANES), _tile(n, TN_PROJ)
    return pl.pallas_call(
        _norm_mm_kernel,
        out_shape=jax.ShapeDtypeStruct((t, n), out_dtype),
        grid=(t // tm, n // tn),
        in_specs=[
            pl.BlockSpec((tm, d), lambda i, j: (i, 0)),
            pl.BlockSpec((1, d), lambda i, j: (0, 0)),
            pl.BlockSpec((d, tn), lambda i, j: (0, j)),
        ],
        out_specs=pl.BlockSpec((tm, tn), lambda i, j: (i, j)),
        scratch_shapes=[pltpu.VMEM((tm, d), BF16)],
        compiler_params=_params(("arbitrary", "arbitrary")),
        name="norm_matmul",
    )(x, gain.reshape(1, d), w)


def _norm_gateup_kernel(x_ref, g_ref, wg_ref, wu_ref, o_ref, xn_ref):
    @pl.when(pl.program_id(1) == 0)
    def _():
        xn_ref[...] = _rms_rows(x_ref[...], g_ref[...]).astype(BF16)

    xn = xn_ref[...]
    a = jnp.dot(xn, wg_ref[...], preferred_element_type=F32)
    b = jnp.dot(xn, wu_ref[...], preferred_element_type=F32)
    o_ref[...] = (_silu(a) * b).astype(o_ref.dtype)


def norm_gateup(x, gain, wg, wu):
    t, d = x.shape
    f = wg.shape[1]
    tm, tf = _tile(t, TM_PROJ, SUBLANES), _tile(f, TF_FFN)
    return pl.pallas_call(
        _norm_gateup_kernel,
        out_shape=jax.ShapeDtypeStruct((t, f), BF16),
        grid=(t // tm, f // tf),
        in_specs=[
            pl.BlockSpec((tm, d), lambda i, j: (i, 0)),
            pl.BlockSpec((1, d), lambda i, j: (0, 0)),
            pl.BlockSpec((d, tf), lambda i, j: (0, j)),
            pl.BlockSpec((d, tf), lambda i, j: (0, j)),
        ],
        out_specs=pl.BlockSpec((tm, tf), lambda i, j: (i, j)),
        scratch_shapes=[pltpu.VMEM((tm, d), BF16)],
        compiler_params=_params(("arbitrary", "arbitrary")),
        name="norm_gateup",
    )(x, gain.reshape(1, d), wg, wu)


def _mm_res_kernel(a_ref, w_ref, r_ref, o_ref):
    o_ref[...] = r_ref[...] + jnp.dot(a_ref[...], w_ref[...], preferred_element_type=F32)


def matmul_residual(a, w, res):
    t, k = a.shape
    n = w.shape[1]
    tm = t
    for cand in (1024, 512, 256, 128):
        need = k * n * 2 + 2 * cand * k * 2 + 4 * cand * n * 4 + cand * n * 4
        if t % cand == 0 and need <= RES_VMEM_BUDGET:
            tm = cand
            break
    return pl.pallas_call(
        _mm_res_kernel,
        out_shape=jax.ShapeDtypeStruct((t, n), F32),
        grid=(t // tm,),
        in_specs=[
            pl.BlockSpec((tm, k), lambda i: (i, 0)),
            pl.BlockSpec((k, n), lambda i: (0, 0), pipeline_mode=pl.Buffered(1)),
            pl.BlockSpec((tm, n), lambda i: (i, 0)),
        ],
        out_specs=pl.BlockSpec((tm, n), lambda i: (i, 0)),
        compiler_params=_params(("arbitrary",)),
        name="matmul_residual",
    )(a, w, res)


def _ssd_kernel(x_ref, bm_ref, cm_ref, z_ref, dtt_ref, cwx_ref, cbx_ref, cwb_ref, cbb_ref,
                cwc_ref, cbc_ref, dtb_ref, alog_ref, dx_ref, gn_ref, e_ref, sh_ref,
                o_ref, pad_ref, h_ref, *, hpg, gw, nc):
    L = CHUNK
    n = SSM_STATE
    hd = gw // hpg
    rows = nc * L
    c = pl.program_id(2)
    tail = CONV_TAIL

    @pl.when(c == 0)
    def _():
        pad_ref[0:tail, :] = jnp.zeros((tail, gw + 2 * n), BF16)
        h_ref[...] = jnp.zeros_like(h_ref)

    @pl.when(c > 0)
    def _():
        pad_ref[0:tail, :] = pad_ref[rows:rows + tail, :]

    pad_ref[tail:tail + rows, 0:gw] = x_ref[0]
    pad_ref[tail:tail + rows, gw:gw + n] = bm_ref[0]
    pad_ref[tail:tail + rows, gw + n:gw + 2 * n] = cm_ref[0]

    def conv(shifted, t0, col0, width, w_ref, b_ref):
        acc = b_ref[...] + w_ref[CONV_WIDTH - 1:CONV_WIDTH, :] * pad_ref[
            t0 + tail:t0 + tail + L, col0:col0 + width].astype(F32)
        for k in range(CONV_WIDTH - 1):
            acc = acc + w_ref[k:k + 1, :] * shifted[k * L:(k + 1) * L, col0:col0 + width]
        return _silu(acc)

    raw = dtt_ref[0] + dtb_ref[...]
    dt_all = jnp.maximum(raw, 0.0) + jnp.log1p(jnp.exp(-jnp.abs(raw)))
    a_all = dt_all * (-jnp.exp(alog_ref[...]))
    row = lax.broadcasted_iota(jnp.int32, (L, L), 0)
    col = lax.broadcasted_iota(jnp.int32, (L, L), 1)
    upper = (row <= col).astype(F32)
    causal = row >= col
    lane = lax.broadcasted_iota(jnp.int32, (1, 2 * hd), 1)
    lo = lane < hd

    for ci in range(nc):
        t0 = ci * L
        shifted = jnp.dot(sh_ref[...], pad_ref[t0:t0 + tail + L, :],
                          preferred_element_type=F32)
        xs = conv(shifted, t0, 0, gw, cwx_ref, cbx_ref)
        bc = conv(shifted, t0, gw, n, cwb_ref, cbb_ref)
        cc = conv(shifted, t0, gw + n, n, cwc_ref, cbc_ref)

        dt = dt_all[:, t0:t0 + L]
        acs = jnp.dot(a_all[:, t0:t0 + L], upper, preferred_element_type=F32,
                      precision=lax.Precision.HIGHEST)
        a_last = acs[:, L - 1:L]
        dte = jnp.exp(a_last - acs)
        eacs = jnp.exp(acs)

        parts = _split3(acs) + _split3(dt) + _split3(dt * dte) + _split3(eacs)
        stacked = jnp.concatenate(parts, axis=0).T.astype(BF16)
        spread = jnp.dot(stacked, e_ref[...], preferred_element_type=F32)
        acs_col = spread[:, :hpg * L]
        dt_x = spread[:, hpg * L:hpg * L + gw]
        w_x = spread[:, hpg * L + gw:hpg * L + 2 * gw]
        eacs_x = spread[:, hpg * L + 2 * gw:]

        xdt = xs * dt_x
        bct = bc.T.astype(BF16)
        ccb = cc.astype(BF16)
        cb = jnp.dot(ccb, bct, preferred_element_type=F32)

        y_parts = []
        for q in range(hpg // 2):
            ms = []
            for j in (2 * q, 2 * q + 1):
                seg = acs_col[:, j * L:(j + 1) * L] - acs[j:j + 1, :]
                dec = jnp.exp(jnp.where(causal, seg, -jnp.inf))
                ms.append((cb * dec).astype(BF16))
            mcat = jnp.concatenate(ms, axis=1)
            xq = xdt[:, q * 2 * hd:(q + 1) * 2 * hd]
            rhs = jnp.concatenate([jnp.where(lo, xq, 0.0), jnp.where(lo, 0.0, xq)], axis=0)
            y_parts.append(jnp.dot(mcat, rhs.astype(BF16), preferred_element_type=F32))
        y_diag = jnp.concatenate(y_parts, axis=1)

        h_in = h_ref[...]
        y_off = jnp.dot(ccb, h_in.astype(BF16), preferred_element_type=F32) * eacs_x
        states = jnp.dot(bct, (xs * w_x).astype(BF16), preferred_element_type=F32)
        h_ref[...] = h_in * eacs_x[L - 1:L, :] + states

        y = y_diag + y_off + dx_ref[...] * xs
        gated = y * _silu(z_ref[0, t0:t0 + L, :].astype(F32))
        o_ref[0, t0:t0 + L, :] = _rms_rows(gated, gn_ref[...]).astype(o_ref.dtype)


def ssd_core(z, xbc, dtt, conv_w, conv_b, dt_bias, a_log, d_skip, out_norm):
    b, s, d_inner = z.shape
    g = SSM_GROUPS
    n = SSM_STATE
    heads = dt_bias.shape[0]
    hpg = heads // g
    gw = d_inner // g
    hd = gw // hpg
    L = CHUNK
    nb = d_inner // n
    assert gw % LANES == 0 and hpg % 2 == 0 and 2 * hd == LANES and s % L == 0

    eye = jnp.eye(hpg, dtype=F32)
    widths = [hpg * L, gw, gw, gw]
    mats = [jnp.repeat(eye, L, axis=1)] + [jnp.repeat(eye, hd, axis=1)] * 3
    spread_rows = []
    for qty in range(4):
        blk = jnp.concatenate([mats[qty] if c == qty else jnp.zeros((hpg, widths[c]), F32)
                               for c in range(4)], axis=1)
        spread_rows += [blk] * 3
    spread_mat = jnp.concatenate(spread_rows, axis=0).astype(BF16)
    dx = jnp.repeat(d_skip.astype(F32), hd).reshape(1, d_inner)

    nc = SSD_CHUNKS_PER_STEP if (s // L) % SSD_CHUNKS_PER_STEP == 0 else 1
    rows = nc * L
    out_row = jnp.arange((CONV_WIDTH - 1) * L, dtype=jnp.int32)
    src_row = out_row % L + CONV_TAIL - (CONV_WIDTH - 1) + out_row // L
    shift_mat = (src_row[:, None] == jnp.arange(CONV_TAIL + L, dtype=jnp.int32)[None, :]).astype(BF16)
    kern = functools.partial(_ssd_kernel, hpg=hpg, gw=gw, nc=nc)
    return pl.pallas_call(
        kern,
        out_shape=jax.ShapeDtypeStruct((b, s, d_inner), BF16),
        grid=(b, g, s // rows),
        in_specs=[
            pl.BlockSpec((1, rows, gw), lambda i, j, c: (i, c, j)),
            pl.BlockSpec((1, rows, n), lambda i, j, c: (i, c, nb + j)),
            pl.BlockSpec((1, rows, n), lambda i, j, c: (i, c, nb + g + j)),
            pl.BlockSpec((1, rows, gw), lambda i, j, c: (i, c, j)),
            pl.BlockSpec((1, hpg, rows), lambda i, j, c: (i, j, c)),
            pl.BlockSpec((CONV_WIDTH, gw), lambda i, j, c: (0, j)),
            pl.BlockSpec((1, gw), lambda i, j, c: (0, j)),
            pl.BlockSpec((CONV_WIDTH, n), lambda i, j, c: (0, nb + j)),
            pl.BlockSpec((1, n), lambda i, j, c: (0, nb + j)),
            pl.BlockSpec((CONV_WIDTH, n), lambda i, j, c: (0, nb + g + j)),
            pl.BlockSpec((1, n), lambda i, j, c: (0, nb + g + j)),
            pl.BlockSpec((hpg, 1), lambda i, j, c: (j, 0)),
            pl.BlockSpec((hpg, 1), lambda i, j, c: (j, 0)),
            pl.BlockSpec((1, gw), lambda i, j, c: (0, j)),
            pl.BlockSpec((1, gw), lambda i, j, c: (0, j)),
            pl.BlockSpec((12 * hpg, hpg * L + 3 * gw), lambda i, j, c: (0, 0)),
            pl.BlockSpec(((CONV_WIDTH - 1) * L, CONV_TAIL + L), lambda i, j, c: (0, 0)),
        ],
        out_specs=pl.BlockSpec((1, rows, gw), lambda i, j, c: (i, c, j)),
        scratch_shapes=[pltpu.VMEM((rows + CONV_TAIL, gw + 2 * n), BF16), pltpu.VMEM((n, gw), F32)],
        compiler_params=_params(("arbitrary", "arbitrary", "arbitrary")),
        name="ssd_core",
    )(xbc, xbc, xbc, z, dtt, conv_w, conv_b.reshape(1, -1), conv_w, conv_b.reshape(1, -1),
      conv_w, conv_b.reshape(1, -1), dt_bias.reshape(heads, 1), a_log.reshape(heads, 1),
      dx, out_norm.reshape(1, d_inner), spread_mat, shift_mat)


def _attn_kernel(q_ref, k_ref, v_ref, qg_ref, kg_ref, lq1_ref, lk1_ref, lq2_ref, lk2_ref,
                 sg_ref, o_ref, kn_ref, vt_ref, m_ref, l_ref, acc_ref, s_ref, p_ref, a_ref,
                 *, tq, t, lambda_init):
    s_len = k_ref.shape[1]
    n_tiles = s_len // t
    hd = ATT_HEAD_DIM
    qi = pl.program_id(2)
    lane = lax.broadcasted_iota(jnp.int32, (1, 2 * hd), 1)
    lo = lane < hd

    def comp_norm(x, gain):
        x2 = x * x
        s_lo = jnp.sum(jnp.where(lo, x2, 0.0), axis=-1, keepdims=True)
        s_hi = jnp.sum(jnp.where(lo, 0.0, x2), axis=-1, keepdims=True)
        inv = jnp.where(lo, lax.rsqrt(s_lo * (1.0 / hd) + EPS), lax.rsqrt(s_hi * (1.0 / hd) + EPS))
        return x * inv * gain

    @pl.when(qi == 0)
    def _():
        def body(i, carry):
            r0 = pl.multiple_of(i * t, t)
            kn = comp_norm(k_ref[0, pl.ds(r0, t), :].astype(F32), kg_ref[...])
            kn_ref[pl.ds(r0, t), :] = kn.astype(BF16)
            vt_ref[i] = v_ref[0, pl.ds(r0, t), :].astype(F32).T.astype(BF16)
            return carry

        lax.fori_loop(0, s_len // t, body, 0)

    qn = comp_norm(q_ref[0].astype(F32), qg_ref[...]) * (hd ** -0.5 * LOG2E)
    qc = (jnp.where(lo, qn, 0.0).astype(BF16), jnp.where(lo, 0.0, qn).astype(BF16))

    m_ref[...] = jnp.full(m_ref.shape, -jnp.inf, F32)
    l_ref[...] = jnp.zeros(l_ref.shape, F32)
    acc_ref[...] = jnp.zeros(acc_ref.shape, F32)

    def scores(tile, slot):
        r0 = pl.multiple_of(jnp.minimum(tile, n_tiles - 1) * t, t)
        kk = kn_ref[pl.ds(r0, t), :]
        for c in range(2):
            s_ref[slot, c] = lax.dot_general(kk, qc[c], (((1,), (1,)), ((), ())),
                                             preferred_element_type=F32)

    def softmax(tile, slot, masked):
        for c in range(2):
            sc = s_ref[slot, c]
            if masked:
                key = tile * t + lax.broadcasted_iota(jnp.int32, (t, tq), 0)
                qry = qi * tq + lax.broadcasted_iota(jnp.int32, (t, tq), 1)
                sc = jnp.where(key <= qry, sc, -jnp.inf)
            m_prev = m_ref[c]
            m_new = jnp.maximum(m_prev, jnp.max(sc, axis=0, keepdims=True))
            alpha = jnp.exp2(m_prev - m_new)
            p = jnp.exp2(sc - m_new)
            l_ref[c] = alpha * l_ref[c] + jnp.sum(p, axis=0, keepdims=True)
            m_ref[c] = m_new
            a_ref[slot, c] = alpha
            p_ref[slot, c] = p.astype(BF16)

    def update(tile0, tile1):
        vt0 = vt_ref[jnp.minimum(tile0, n_tiles - 1)]
        vt1 = vt_ref[jnp.minimum(tile1, n_tiles - 1)]
        for c in range(2):
            a1 = a_ref[1, c]
            pv0 = jnp.dot(vt0, p_ref[0, c], preferred_element_type=F32)
            pv1 = jnp.dot(vt1, p_ref[1, c], preferred_element_type=F32)
            acc_ref[c] = (a_ref[0, c] * a1) * acc_ref[c] + a1 * pv0 + pv1

    d0 = 2 * ((qi * tq) // (2 * t))
    n_pairs = d0 // 2 + 1
    scores(d0, 0)
    scores(d0 + 1, 1)
    softmax(d0, 0, True)
    scores(0, 0)
    softmax(d0 + 1, 1, True)

    def pair(u, carry):
        x = 2 * u - 2
        first = u == 1
        update(jnp.where(first, d0, x - 2), jnp.where(first, d0 + 1, x - 1))
        scores(x + 1, 1)
        softmax(x, 0, False)
        scores(x + 2, 0)
        softmax(x + 1, 1, False)
        return carry

    lax.fori_loop(1, n_pairs, pair, 0)
    only = n_pairs == 1
    update(jnp.where(only, d0, d0 - 2), jnp.where(only, d0 + 1, d0 - 1))

    lam = (jnp.exp(jnp.sum(lq1_ref[...] * lk1_ref[...], axis=-1, keepdims=True))
           - jnp.exp(jnp.sum(lq2_ref[...] * lk2_ref[...], axis=-1, keepdims=True)) + lambda_init)
    o = acc_ref[0] / l_ref[0] - lam * (acc_ref[1] / l_ref[1])
    o = o * lax.rsqrt(jnp.mean(o * o, axis=0, keepdims=True) + EPS) * sg_ref[...]
    o_ref[0] = (o * (1.0 - lambda_init)).T.astype(o_ref.dtype)


def diff_attention_core(qkv, q_norm, k_norm, lq1, lk1, lq2, lk2, sub_norm, lambda_init):
    b, s, w3 = qkv.shape
    width = w3 // 3
    heads = width // ATT_V_DIM
    hd = ATT_HEAD_DIM
    tq = _tile(s, TQ_ATT, LANES)
    t = _tile(s, TK_ATT, LANES)
    assert tq % t == 0 and (2 * t) % tq == 0
    vec = lambda a: a.astype(F32).reshape(1, -1)
    qg = jnp.tile(q_norm.astype(F32), 2).reshape(1, 2 * hd)
    kg = jnp.tile(k_norm.astype(F32), 2).reshape(1, 2 * hd)
    small = lambda width_: pl.BlockSpec((1, width_), lambda i, h, j: (0, 0))
    kern = functools.partial(_attn_kernel, tq=tq, t=t, lambda_init=lambda_init)
    return pl.pallas_call(
        kern,
        out_shape=jax.ShapeDtypeStruct((b, s, width), BF16),
        grid=(b, heads, s // tq),
        in_specs=[
            pl.BlockSpec((1, tq, 2 * hd), lambda i, h, j: (i, j, h)),
            pl.BlockSpec((1, s, 2 * hd), lambda i, h, j: (i, 0, heads + h)),
            pl.BlockSpec((1, s, ATT_V_DIM), lambda i, h, j: (i, 0, 2 * heads + h)),
            small(2 * hd), small(2 * hd), small(hd), small(hd), small(hd), small(hd),
            pl.BlockSpec((ATT_V_DIM, 1), lambda i, h, j: (0, 0)),
        ],
        out_specs=pl.BlockSpec((1, tq, ATT_V_DIM), lambda i, h, j: (i, j, h)),
        scratch_shapes=[
            pltpu.VMEM((s, 2 * hd), BF16),
            pltpu.VMEM((s // t, ATT_V_DIM, t), BF16),
            pltpu.VMEM((2, 1, tq), F32), pltpu.VMEM((2, 1, tq), F32),
            pltpu.VMEM((2, ATT_V_DIM, tq), F32),
            pltpu.VMEM((2, 2, t, tq), F32), pltpu.VMEM((2, 2, t, tq), BF16),
            pltpu.VMEM((2, 2, 1, tq), F32),
        ],
        compiler_params=_params(("arbitrary", "arbitrary", "arbitrary")),
        name="diff_attention",
    )(qkv, qkv, qkv, qg, kg, vec(lq1), vec(lk1), vec(lq2), vec(lk2),
      sub_norm.astype(F32).reshape(ATT_V_DIM, 1))


def _route_kernel(x_ref, g_ref, r_ref, xn_ref, info_ref, cnt_ref, run_ref, *, n_experts):
    @pl.when(pl.program_id(0) == 0)
    def _():
        run_ref[...] = jnp.zeros_like(run_ref)

    xn = _rms_rows(x_ref[...], g_ref[...])
    xn_ref[...] = xn.astype(BF16)
    logits = jnp.dot(xn, r_ref[...], preferred_element_type=F32, precision=lax.Precision.HIGHEST)
    lane = lax.broadcasted_iota(jnp.int32, logits.shape, 1)
    logits = jnp.where(lane < n_experts, logits, -jnp.inf)
    m1 = jnp.max(logits, axis=-1, keepdims=True)
    i1 = jnp.min(jnp.where(logits == m1, lane, LANES), axis=-1, keepdims=True)
    rest = jnp.where(lane == i1, -jnp.inf, logits)
    m2 = jnp.max(rest, axis=-1, keepdims=True)
    i2 = jnp.min(jnp.where(rest == m2, lane, LANES), axis=-1, keepdims=True)
    e2 = jnp.exp(m2 - m1)
    g1 = 1.0 / (1.0 + e2)
    g2 = e2 * g1
    tm = logits.shape[0]
    oh1 = (lane == i1).astype(F32)
    oh2 = (lane == i2).astype(F32)
    both = oh1 + oh2
    row = lax.broadcasted_iota(jnp.int32, (tm, tm), 0)
    col = lax.broadcasted_iota(jnp.int32, (tm, tm), 1)
    before = jnp.dot((col < row).astype(BF16), both.astype(BF16),
                     preferred_element_type=F32) + run_ref[0:1, :]
    r1 = jnp.sum(before * oh1, axis=-1, keepdims=True)
    r2 = jnp.sum(before * oh2, axis=-1, keepdims=True)
    run_ref[...] = run_ref[...] + jnp.sum(both, axis=0, keepdims=True)
    cnt_ref[...] = run_ref[...]
    info = jnp.zeros_like(logits)
    for k, val in enumerate((i1.astype(F32), i2.astype(F32), g1, g2, r1, r2)):
        info = jnp.where(lane == k, val, info)
    info_ref[...] = info


def moe_route(h, gain, router):
    t, d = h.shape
    e = router.shape[1]
    tm = _tile(t, TM_ROUTE, SUBLANES)
    rpad = jnp.zeros((d, LANES), F32).at[:, :e].set(router.astype(F32))
    return pl.pallas_call(
        functools.partial(_route_kernel, n_experts=e),
        out_shape=(jax.ShapeDtypeStruct((t, d), BF16), jax.ShapeDtypeStruct((t, LANES), F32),
                   jax.ShapeDtypeStruct((SUBLANES, LANES), F32)),
        grid=(t // tm,),
        in_specs=[
            pl.BlockSpec((tm, d), lambda i: (i, 0)),
            pl.BlockSpec((1, d), lambda i: (0, 0)),
            pl.BlockSpec((d, LANES), lambda i: (0, 0)),
        ],
        out_specs=(pl.BlockSpec((tm, d), lambda i: (i, 0)),
                   pl.BlockSpec((tm, LANES), lambda i: (i, 0)),
                   pl.BlockSpec((SUBLANES, LANES), lambda i: (0, 0))),
        scratch_shapes=[pltpu.VMEM((SUBLANES, LANES), F32)],
        compiler_params=_params(("arbitrary",)),
        name="moe_route",
    )(h, gain.reshape(1, d), rpad)


def _new_weights(te_ref, i):
    prev = te_ref[jnp.maximum(i - 1, 0)]
    return jnp.logical_or(i == 0, te_ref[i] != prev)


def _moe_gateup_kernel(te_ref, nu_ref, x_ref, wg_ref, wu_ref, o_ref, wgb_ref, wub_ref):
    i = pl.program_id(1)

    @pl.when(_new_weights(te_ref, i))
    def _():
        wgb_ref[...] = wg_ref[0].astype(BF16)
        wub_ref[...] = wu_ref[0].astype(BF16)

    @pl.when(i < nu_ref[0])
    def _():
        x = x_ref[...]
        a = jnp.dot(x, wgb_ref[...], preferred_element_type=F32)
        b = jnp.dot(x, wub_ref[...], preferred_element_type=F32)
        o_ref[...] = (_silu(a) * b).astype(o_ref.dtype)

    @pl.when(i >= nu_ref[0])
    def _():
        o_ref[...] = jnp.zeros_like(o_ref)


def _moe_down_kernel(te_ref, nu_ref, h_ref, wd_ref, o_ref, wdb_ref):
    i = pl.program_id(1)

    @pl.when(_new_weights(te_ref, i))
    def _():
        wdb_ref[...] = wd_ref[0].astype(BF16)

    @pl.when(i < nu_ref[0])
    def _():
        o_ref[...] = jnp.dot(h_ref[...], wdb_ref[...],
                             preferred_element_type=F32).astype(o_ref.dtype)

    @pl.when(i >= nu_ref[0])
    def _():
        o_ref[...] = jnp.zeros_like(o_ref)


def moe_experts(xs, tile_expert, n_used, w_gate, w_up, w_down, tm):
    slots, d = xs.shape
    e, _, f = w_gate.shape
    n_tiles = slots // tm
    tf = _tile(f, TF_FFN)
    hmid = pl.pallas_call(
        _moe_gateup_kernel,
        out_shape=jax.ShapeDtypeStruct((slots, f), BF16),
        grid_spec=pltpu.PrefetchScalarGridSpec(
            num_scalar_prefetch=2,
            grid=(f // tf, n_tiles),
            in_specs=[
                pl.BlockSpec((tm, d), lambda j, i, te, nu: (i, 0)),
                pl.BlockSpec((1, d, tf), lambda j, i, te, nu: (te[i], 0, j)),
                pl.BlockSpec((1, d, tf), lambda j, i, te, nu: (te[i], 0, j)),
            ],
            out_specs=pl.BlockSpec((tm, tf), lambda j, i, te, nu: (i, j)),
            scratch_shapes=[pltpu.VMEM((d, tf), BF16), pltpu.VMEM((d, tf), BF16)],
        ),
        compiler_params=_params(("arbitrary", "arbitrary")),
        name="moe_gateup",
    )(tile_expert, n_used, xs, w_gate, w_up)

    tn = _tile(d, TN_MOE_DOWN)
    return pl.pallas_call(
        _moe_down_kernel,
        out_shape=jax.ShapeDtypeStruct((slots, d), BF16),
        grid_spec=pltpu.PrefetchScalarGridSpec(
            num_scalar_prefetch=2,
            grid=(d // tn, n_tiles),
            in_specs=[
                pl.BlockSpec((tm, f), lambda j, i, te, nu: (i, 0)),
                pl.BlockSpec((1, f, tn), lambda j, i, te, nu: (te[i], 0, j)),
            ],
            out_specs=pl.BlockSpec((tm, tn), lambda j, i, te, nu: (i, j)),
            scratch_shapes=[pltpu.VMEM((f, tn), BF16)],
        ),
        compiler_params=_params(("arbitrary", "arbitrary")),
        name="moe_down",
    )(tile_expert, n_used, hmid, w_down)


def moe_block(h, gain, router, w_gate, w_up, w_down):
    t, d = h.shape
    e = router.shape[1]
    tm = min(TM_MOE, t)
    xn, info, cnt = moe_route(h, gain, router)
    idx = info[:, :TOP_K].astype(jnp.int32)
    gates = info[:, TOP_K:2 * TOP_K]
    rank = info[:, 2 * TOP_K:3 * TOP_K].astype(jnp.int32)
    counts = cnt[0, :e].astype(jnp.int32)

    tiles_per_e = (counts + tm - 1) // tm
    tile_end = jnp.cumsum(tiles_per_e)
    starts = (tile_end - tiles_per_e) * tm
    slot = (jnp.take(starts, idx.reshape(-1)) + rank.reshape(-1))
    n_tiles = (TOP_K * t) // tm + e
    n_slots = n_tiles * tm
    token_of_slot = jnp.zeros((n_slots,), jnp.int32).at[slot].set(
        jnp.arange(TOP_K * t, dtype=jnp.int32) // TOP_K)
    tile_expert = jnp.minimum(
        jnp.searchsorted(tile_end, jnp.arange(n_tiles, dtype=jnp.int32), side="right"),
        e - 1).astype(jnp.int32)
    n_used = tile_end[-1:].astype(jnp.int32)

    xs = jnp.take(xn, token_of_slot, axis=0)
    ys = moe_experts(xs, tile_expert, n_used, w_gate, w_up, w_down, tm)
    slot2 = slot.reshape(t, TOP_K)
    out = h
    for k in range(TOP_K):
        out = out + gates[:, k:k + 1] * jnp.take(ys, slot2[:, k], axis=0).astype(F32)
    return out


def mamba_layer(h, b, s, pre_norm, in_proj, conv_w, conv_b, dt_bias, a_log, d_skip, out_norm,
                out_proj):
    d_inner = out_proj.shape[0]
    d_xbc = conv_w.shape[1]
    heads = dt_bias.shape[0]
    w = in_proj.astype(BF16)
    z = norm_matmul(h, pre_norm, w[:, :d_inner], BF16)
    xbc = norm_matmul(h, pre_norm, w[:, d_inner:d_inner + d_xbc], BF16)
    w_dt = jnp.zeros((h.shape[1], LANES), BF16).at[:, :heads].set(w[:, d_inner + d_xbc:])
    dt = norm_matmul(h, pre_norm, w_dt, F32)[:, :heads]
    dtt = jnp.swapaxes(dt.reshape(b, s, heads), 1, 2)
    y = ssd_core(z.reshape(b, s, d_inner), xbc.reshape(b, s, d_xbc), dtt, conv_w.astype(F32),
                 conv_b.astype(F32), dt_bias.astype(F32), a_log.astype(F32), d_skip, out_norm)
    return matmul_residual(y.reshape(b * s, d_inner), out_proj.astype(BF16), h)


def dense_layer(h, pre_norm, w_gate, w_up, w_down):
    mid = norm_gateup(h, pre_norm, w_gate.astype(BF16), w_up.astype(BF16))
    return matmul_residual(mid, w_down.astype(BF16), h)


def attention_layer(h, b, s, pre_norm, w_qkv, q_norm, k_norm, lq1, lk1, lq2, lk2, sub_norm, w_o,
                    lambda_init):
    qkv = norm_matmul(h, pre_norm, w_qkv.astype(BF16), BF16)
    o = diff_attention_core(qkv.reshape(b, s, -1), q_norm, k_norm, lq1, lk1, lq2, lk2, sub_norm,
                            lambda_init)
    return matmul_residual(o.reshape(b * s, -1), w_o.astype(BF16), h)


def kernel(x, ssm_pre_norm, ssm_in_proj, ssm_conv_w, ssm_conv_b, ssm_dt_bias, ssm_A_log, ssm_D,
           ssm_out_norm, ssm_out_proj, dense_pre_norm, dense_w_gate, dense_w_up, dense_w_down,
           att_pre_norm, att_w_qkv, att_q_norm, att_k_norm, att_lambda_q1, att_lambda_k1,
           att_lambda_q2, att_lambda_k2, att_sub_norm, att_w_o, moe_pre_norm, moe_router,
           moe_w_gate, moe_w_up, moe_w_down):
    b, s, d = x.shape
    h = x.reshape(b * s, d)
    for i in range(DEPTH):
        j = i // N_MIXERS
        if i % N_MIXERS == 0:
            h = mamba_layer(h, b, s, ssm_pre_norm[j], ssm_in_proj[j], ssm_conv_w[j],
                            ssm_conv_b[j], ssm_dt_bias[j], ssm_A_log[j], ssm_D[j],
                            ssm_out_norm[j], ssm_out_proj[j])
            h = dense_layer(h, dense_pre_norm[j], dense_w_gate[j], dense_w_up[j],
                            dense_w_down[j])
        else:
            lambda_init = 0.8 - 0.6 * math.exp(-0.3 * i)
            h = attention_layer(h, b, s, att_pre_norm[j], att_w_qkv[j], att_q_norm[j],
                                att_k_norm[j], att_lambda_q1[j], att_lambda_k1[j],
                                att_lambda_q2[j], att_lambda_k2[j], att_sub_norm[j], att_w_o[j],
                                lambda_init)
            h = moe_block(h, moe_pre_norm[j], moe_router[j], moe_w_gate[j], moe_w_up[j],
                          moe_w_down[j])
    return h.reshape(b, s, d)
```

```python
import functools
import math

import jax
import jax.numpy as jnp
from jax import lax
from jax.experimental import pallas as pl
from jax.experimental.pallas import tpu as pltpu

F32 = jnp.float32
BF16 = jnp.bfloat16

EPS = 1e-5
DEPTH = 2
N_MIXERS = 2

SSM_GROUPS = 8
SSM_STATE = 128
CHUNK = 128
CONV_WIDTH = 4
SSM_HEAD_DIM = 64
CONV_TAIL = 16

ATT_HEAD_DIM = 64
ATT_V_DIM = 2 * ATT_HEAD_DIM
ATT_SUM_ROWS = 16

TOP_K = 2

LOG2E = math.log2(math.e)

LANES = 128
SUBLANES = 8
VMEM_LIMIT_BYTES = 56 * 1024 * 1024

TM_PROJ = 1024
TN_PROJ = 1024
TM_ROUTE = 512
RES_VMEM_BUDGET = 46 * 1024 * 1024
TF_FFN = 512
SSD_CHUNKS_PER_STEP = 8
TM_MOE = 512
TN_MOE_DOWN = 512
TQ_ATT = 512
TK_ATT = 256


def _tile(dim, pref, quantum=LANES):
    if dim <= pref:
        return dim
    t = (pref // quantum) * quantum
    while t >= quantum:
        if dim % t == 0:
            return t
        t -= quantum
    return dim


def _params(sem):
    return pltpu.CompilerParams(dimension_semantics=sem, vmem_limit_bytes=VMEM_LIMIT_BYTES)


def _rms_rows(x, gain):
    return x * lax.rsqrt(jnp.mean(x * x, axis=-1, keepdims=True) + EPS) * gain


def _silu(x):
    h = 0.5 * x
    return h + h * jnp.tanh(h)


def _split3(v):
    hi = v.astype(BF16).astype(F32)
    r = v - hi
    mid = r.astype(BF16).astype(F32)
    return [hi, mid, r - mid]


def _norm_mm_kernel(x_ref, g_ref, w_ref, o_ref, xn_ref):
    @pl.when(pl.program_id(1) == 0)
    def _():
        xn_ref[...] = _rms_rows(x_ref[...], g_ref[...]).astype(BF16)

    o_ref[...] = jnp.dot(xn_ref[...], w_ref[...], preferred_element_type=F32).astype(o_ref.dtype)


def norm_matmul(x, gain, w, out_dtype):
    t, d = x.shape
    n = w.shape[1]
    tm, tn = _tile(t, TM_PROJ, SUBLANES), _tile(n, TN_PROJ)
    return pl.pallas_call(
        _norm_mm_kernel,
        out_shape=jax.ShapeDtypeStruct((t, n), out_dtype),
        grid=(t // tm, n // tn),
        in_specs=[
            pl.BlockSpec((tm, d), lambda i, j: (i, 0)),
            pl.BlockSpec((1, d), lambda i, j: (0, 0)),
            pl.BlockSpec((d, tn), lambda i, j: (0, j)),
        ],
        out_specs=pl.BlockSpec((tm, tn), lambda i, j: (i, j)),
        scratch_shapes=[pltpu.VMEM((tm, d), BF16)],
        compiler_params=_params(("arbitrary", "arbitrary")),
        name="norm_matmul",
    )(x, gain.reshape(1, d), w)


def _norm_gateup_kernel(x_ref, g_ref, wg_ref, wu_ref, o_ref, xn_ref):
    @pl.when(pl.program_id(1) == 0)
    def _():
        xn_ref[...] = _rms_rows(x_ref[...], g_ref[...]).astype(BF16)

    xn = xn_ref[...]
    a = jnp.dot(xn, wg_ref[...], preferred_element_type=F32)
    b = jnp.dot(xn, wu_ref[...], preferred_element_type=F32)
    o_ref[...] = (_silu(a) * b).astype(o_ref.dtype)


def norm_gateup(x, gain, wg, wu):
    t, d = x.shape
    f = wg.shape[1]
    tm, tf = _tile(t, TM_PROJ, SUBLANES), _tile(f, TF_FFN)
    return pl.pallas_call(
        _norm_gateup_kernel,
        out_shape=jax.ShapeDtypeStruct((t, f), BF16),
        grid=(t // tm, f // tf),
        in_specs=[
            pl.BlockSpec((tm, d), lambda i, j: (i, 0)),
            pl.BlockSpec((1, d), lambda i, j: (0, 0)),
            pl.BlockSpec((d, tf), lambda i, j: (0, j)),
            pl.BlockSpec((d, tf), lambda i, j: (0, j)),
        ],
        out_specs=pl.BlockSpec((tm, tf), lambda i, j: (i, j)),
        scratch_shapes=[pltpu.VMEM((tm, d), BF16)],
        compiler_params=_params(("arbitrary", "arbitrary")),
        name="norm_gateup",
    )(x, gain.reshape(1, d), wg, wu)


def _mm_res_kernel(a_ref, w_ref, r_ref, o_ref):
    o_ref[...] = r_ref[...] + jnp.dot(a_ref[...], w_ref[...], preferred_element_type=F32)


def matmul_residual(a, w, res):
    t, k = a.shape
    n = w.shape[1]
    tm = t
    for cand in (1024, 512, 256, 128):
        need = k * n * 2 + 2 * cand * k * 2 + 4 * cand * n * 4 + cand * n * 4
        if t % cand == 0 and need <= RES_VMEM_BUDGET:
            tm = cand
            break
    return pl.pallas_call(
        _mm_res_kernel,
        out_shape=jax.ShapeDtypeStruct((t, n), F32),
        grid=(t // tm,),
        in_specs=[
            pl.BlockSpec((tm, k), lambda i: (i, 0)),
            pl.BlockSpec((k, n), lambda i: (0, 0), pipeline_mode=pl.Buffered(1)),
            pl.BlockSpec((tm, n), lambda i: (i, 0)),
        ],
        out_specs=pl.BlockSpec((tm, n), lambda i: (i, 0)),
        compiler_params=_params(("arbitrary",)),
        name="matmul_residual",
    )(a, w, res)


def _ssd_kernel(x_ref, bm_ref, cm_ref, z_ref, dtt_ref, cwx_ref, cbx_ref, cwb_ref, cbb_ref,
                cwc_ref, cbc_ref, dtb_ref, alog_ref, dx_ref, gn_ref, e_ref, sh_ref,
                o_ref, pad_ref, h_ref, *, hpg, gw, nc):
    L = CHUNK
    n = SSM_STATE
    hd = gw // hpg
    rows = nc * L
    c = pl.program_id(2)
    tail = CONV_TAIL

    @pl.when(c == 0)
    def _():
        pad_ref[0:tail, :] = jnp.zeros((tail, gw + 2 * n), BF16)
        h_ref[...] = jnp.zeros_like(h_ref)

    @pl.when(c > 0)
    def _():
        pad_ref[0:tail, :] = pad_ref[rows:rows + tail, :]

    pad_ref[tail:tail + rows, 0:gw] = x_ref[0]
    pad_ref[tail:tail + rows, gw:gw + n] = bm_ref[0]
    pad_ref[tail:tail + rows, gw + n:gw + 2 * n] = cm_ref[0]

    def conv(shifted, t0, col0, width, w_ref, b_ref):
        acc = b_ref[...] + w_ref[CONV_WIDTH - 1:CONV_WIDTH, :] * pad_ref[
            t0 + tail:t0 + tail + L, col0:col0 + width].astype(F32)
        for k in range(CONV_WIDTH - 1):
            acc = acc + w_ref[k:k + 1, :] * shifted[k * L:(k + 1) * L, col0:col0 + width]
        return _silu(acc)

    raw = dtt_ref[0] + dtb_ref[...]
    dt_all = jnp.maximum(raw, 0.0) + jnp.log1p(jnp.exp(-jnp.abs(raw)))
    a_all = dt_all * (-jnp.exp(alog_ref[...]))
    row = lax.broadcasted_iota(jnp.int32, (L, L), 0)
    col = lax.broadcasted_iota(jnp.int32, (L, L), 1)
    upper = (row <= col).astype(F32)
    causal = row >= col
    lane = lax.broadcasted_iota(jnp.int32, (1, 2 * hd), 1)
    lo = lane < hd

    for ci in range(nc):
        t0 = ci * L
        shifted = jnp.dot(sh_ref[...], pad_ref[t0:t0 + tail + L, :],
                          preferred_element_type=F32)
        xs = conv(shifted, t0, 0, gw, cwx_ref, cbx_ref)
        bc = conv(shifted, t0, gw, n, cwb_ref, cbb_ref)
        cc = conv(shifted, t0, gw + n, n, cwc_ref, cbc_ref)

        dt = dt_all[:, t0:t0 + L]
        acs = jnp.dot(a_all[:, t0:t0 + L], upper, preferred_element_type=F32,
                      precision=lax.Precision.HIGHEST)
        a_last = acs[:, L - 1:L]
        dte = jnp.exp(a_last - acs)
        eacs = jnp.exp(acs)

        parts = _split3(acs) + _split3(dt) + _split3(dt * dte) + _split3(eacs)
        stacked = jnp.concatenate(parts, axis=0).T.astype(BF16)
        spread = jnp.dot(stacked, e_ref[...], preferred_element_type=F32)
        acs_col = spread[:, :hpg * L]
        dt_x = spread[:, hpg * L:hpg * L + gw]
        w_x = spread[:, hpg * L + gw:hpg * L + 2 * gw]
        eacs_x = spread[:, hpg * L + 2 * gw:]

        xdt = xs * dt_x
        bct = bc.T.astype(BF16)
        ccb = cc.astype(BF16)
        cb = jnp.dot(ccb, bct, preferred_element_type=F32)

        y_parts = []
        for q in range(hpg // 2):
            ms = []
            for j in (2 * q, 2 * q + 1):
                seg = acs_col[:, j * L:(j + 1) * L] - acs[j:j + 1, :]
                dec = jnp.exp(jnp.where(causal, seg, -jnp.inf))
                ms.append((cb * dec).astype(BF16))
            mcat = jnp.concatenate(ms, axis=1)
            xq = xdt[:, q * 2 * hd:(q + 1) * 2 * hd]
            rhs = jnp.concatenate([jnp.where(lo, xq, 0.0), jnp.where(lo, 0.0, xq)], axis=0)
            y_parts.append(jnp.dot(mcat, rhs.astype(BF16), preferred_element_type=F32))
        y_diag = jnp.concatenate(y_parts, axis=1)

        h_in = h_ref[...]
        y_off = jnp.dot(ccb, h_in.astype(BF16), preferred_element_type=F32) * eacs_x
        states = jnp.dot(bct, (xs * w_x).astype(BF16), preferred_element_type=F32)
        h_ref[...] = h_in * eacs_x[L - 1:L, :] + states

        y = y_diag + y_off + dx_ref[...] * xs
        gated = y * _silu(z_ref[0, t0:t0 + L, :].astype(F32))
        o_ref[0, t0:t0 + L, :] = _rms_rows(gated, gn_ref[...]).astype(o_ref.dtype)


def ssd_core(z, xbc, dtt, conv_w, conv_b, dt_bias, a_log, d_skip, out_norm):
    b, s, d_inner = z.shape
    g = SSM_GROUPS
    n = SSM_STATE
    heads = dt_bias.shape[0]
    hpg = heads // g
    gw = d_inner // g
    hd = gw // hpg
    L = CHUNK
    nb = d_inner // n
    assert gw % LANES == 0 and hpg % 2 == 0 and 2 * hd == LANES and s % L == 0

    eye = jnp.eye(hpg, dtype=F32)
    widths = [hpg * L, gw, gw, gw]
    mats = [jnp.repeat(eye, L, axis=1)] + [jnp.repeat(eye, hd, axis=1)] * 3
    spread_rows = []
    for qty in range(4):
        blk = jnp.concatenate([mats[qty] if c == qty else jnp.zeros((hpg, widths[c]), F32)
                               for c in range(4)], axis=1)
        spread_rows += [blk] * 3
    spread_mat = jnp.concatenate(spread_rows, axis=0).astype(BF16)
    dx = jnp.repeat(d_skip.astype(F32), hd).reshape(1, d_inner)

    nc = SSD_CHUNKS_PER_STEP if (s // L) % SSD_CHUNKS_PER_STEP == 0 else 1
    rows = nc * L
    out_row = jnp.arange((CONV_WIDTH - 1) * L, dtype=jnp.int32)
    src_row = out_row % L + CONV_TAIL - (CONV_WIDTH - 1) + out_row // L
    shift_mat = (src_row[:, None] == jnp.arange(CONV_TAIL + L, dtype=jnp.int32)[None, :]).astype(BF16)
    kern = functools.partial(_ssd_kernel, hpg=hpg, gw=gw, nc=nc)
    return pl.pallas_call(
        kern,
        out_shape=jax.ShapeDtypeStruct((b, s, d_inner), BF16),
        grid=(b, g, s // rows),
        in_specs=[
            pl.BlockSpec((1, rows, gw), lambda i, j, c: (i, c, j)),
            pl.BlockSpec((1, rows, n), lambda i, j, c: (i, c, nb + j)),
            pl.BlockSpec((1, rows, n), lambda i, j, c: (i, c, nb + g + j)),
            pl.BlockSpec((1, rows, gw), lambda i, j, c: (i, c, j)),
            pl.BlockSpec((1, hpg, rows), lambda i, j, c: (i, j, c)),
            pl.BlockSpec((CONV_WIDTH, gw), lambda i, j, c: (0, j)),
            pl.BlockSpec((1, gw), lambda i, j, c: (0, j)),
            pl.BlockSpec((CONV_WIDTH, n), lambda i, j, c: (0, nb + j)),
            pl.BlockSpec((1, n), lambda i, j, c: (0, nb + j)),
            pl.BlockSpec((CONV_WIDTH, n), lambda i, j, c: (0, nb + g + j)),
            pl.BlockSpec((1, n), lambda i, j, c: (0, nb + g + j)),
            pl.BlockSpec((hpg, 1), lambda i, j, c: (j, 0)),
            pl.BlockSpec((hpg, 1), lambda i, j, c: (j, 0)),
            pl.BlockSpec((1, gw), lambda i, j, c: (0, j)),
            pl.BlockSpec((1, gw), lambda i, j, c: (0, j)),
            pl.BlockSpec((12 * hpg, hpg * L + 3 * gw), lambda i, j, c: (0, 0)),
            pl.BlockSpec(((CONV_WIDTH - 1) * L, CONV_TAIL + L), lambda i, j, c: (0, 0)),
        ],
        out_specs=pl.BlockSpec((1, rows, gw), lambda i, j, c: (i, c, j)),
        scratch_shapes=[pltpu.VMEM((rows + CONV_TAIL, gw + 2 * n), BF16), pltpu.VMEM((n, gw), F32)],
        compiler_params=_params(("arbitrary", "arbitrary", "arbitrary")),
        name="ssd_core",
    )(xbc, xbc, xbc, z, dtt, conv_w, conv_b.reshape(1, -1), conv_w, conv_b.reshape(1, -1),
      conv_w, conv_b.reshape(1, -1), dt_bias.reshape(heads, 1), a_log.reshape(heads, 1),
      dx, out_norm.reshape(1, d_inner), spread_mat, shift_mat)


def _attn_kernel(q_ref, k_ref, v_ref, qg_ref, kg_ref, lq1_ref, lk1_ref, lq2_ref, lk2_ref,
                 sg_ref, o_ref, kn_ref, vt_ref, m_ref, acc_ref, s_ref, p_ref, a_ref,
                 *, tq, t, lambda_init):
    s_len = k_ref.shape[1]
    n_tiles = s_len // t
    hd = ATT_HEAD_DIM
    qi = pl.program_id(2)
    lane = lax.broadcasted_iota(jnp.int32, (1, 2 * hd), 1)
    lo = lane < hd

    def comp_norm(x, gain):
        x2 = x * x
        s_lo = jnp.sum(jnp.where(lo, x2, 0.0), axis=-1, keepdims=True)
        s_hi = jnp.sum(jnp.where(lo, 0.0, x2), axis=-1, keepdims=True)
        inv = jnp.where(lo, lax.rsqrt(s_lo * (1.0 / hd) + EPS), lax.rsqrt(s_hi * (1.0 / hd) + EPS))
        return x * inv * gain

    @pl.when(qi == 0)
    def _():
        def body(i, carry):
            r0 = pl.multiple_of(i * t, t)
            kn = comp_norm(k_ref[0, pl.ds(r0, t), :].astype(F32), kg_ref[...])
            kn_ref[pl.ds(r0, t), :] = kn.astype(BF16)
            vt_ref[i, 0:ATT_V_DIM, :] = v_ref[0, pl.ds(r0, t), :].astype(F32).T.astype(BF16)
            vt_ref[i, ATT_V_DIM:, :] = jnp.ones((ATT_SUM_ROWS, t), BF16)
            return carry

        lax.fori_loop(0, s_len // t, body, 0)

    qn = comp_norm(q_ref[0].astype(F32), qg_ref[...]) * (hd ** -0.5 * LOG2E)
    qc = (jnp.where(lo, qn, 0.0).astype(BF16), jnp.where(lo, 0.0, qn).astype(BF16))

    m_ref[...] = jnp.full(m_ref.shape, -jnp.inf, F32)
    acc_ref[...] = jnp.zeros(acc_ref.shape, F32)

    def scores(tile, slot):
        r0 = pl.multiple_of(jnp.minimum(tile, n_tiles - 1) * t, t)
        kk = kn_ref[pl.ds(r0, t), :]
        for c in range(2):
            s_ref[slot, c] = lax.dot_general(kk, qc[c], (((1,), (1,)), ((), ())),
                                             preferred_element_type=F32)

    def softmax(tile, slot, masked):
        for c in range(2):
            sc = s_ref[slot, c]
            if masked:
                key = tile * t + lax.broadcasted_iota(jnp.int32, (t, tq), 0)
                qry = qi * tq + lax.broadcasted_iota(jnp.int32, (t, tq), 1)
                sc = jnp.where(key <= qry, sc, -jnp.inf)
            m_prev = m_ref[c]
            m_new = jnp.maximum(m_prev, jnp.max(sc, axis=0, keepdims=True))
            alpha = jnp.exp2(m_prev - m_new)
            p = jnp.exp2(sc - m_new)
            m_ref[c] = m_new
            a_ref[slot, c] = alpha
            p_ref[slot, c] = p.astype(BF16)

    def update(tile0, tile1):
        vt0 = vt_ref[jnp.minimum(tile0, n_tiles - 1)]
        vt1 = vt_ref[jnp.minimum(tile1, n_tiles - 1)]
        for c in range(2):
            a1 = a_ref[1, c]
            pv0 = jnp.dot(vt0, p_ref[0, c], preferred_element_type=F32)
            pv1 = jnp.dot(vt1, p_ref[1, c], preferred_element_type=F32)
            acc_ref[c] = (a_ref[0, c] * a1) * acc_ref[c] + a1 * pv0 + pv1

    d0 = 2 * ((qi * tq) // (2 * t))
    n_pairs = d0 // 2 + 1
    scores(d0, 0)
    scores(d0 + 1, 1)
    softmax(d0, 0, True)
    scores(0, 0)
    softmax(d0 + 1, 1, True)

    def pair(u, carry):
        x = 2 * u - 2
        first = u == 1
        update(jnp.where(first, d0, x - 2), jnp.where(first, d0 + 1, x - 1))
        scores(x + 1, 1)
        softmax(x, 0, False)
        scores(x + 2, 0)
        softmax(x + 1, 1, False)
        return carry

    lax.fori_loop(1, n_pairs, pair, 0)
    only = n_pairs == 1
    update(jnp.where(only, d0, d0 - 2), jnp.where(only, d0 + 1, d0 - 1))

    lam = (jnp.exp(jnp.sum(lq1_ref[...] * lk1_ref[...], axis=-1, keepdims=True))
           - jnp.exp(jnp.sum(lq2_ref[...] * lk2_ref[...], axis=-1, keepdims=True)) + lambda_init)
    e = ATT_V_DIM
    o = (acc_ref[0, 0:e, :] / acc_ref[0, e:e + 1, :]
         - lam * (acc_ref[1, 0:e, :] / acc_ref[1, e:e + 1, :]))
    o = o * lax.rsqrt(jnp.mean(o * o, axis=0, keepdims=True) + EPS) * sg_ref[...]
    o_ref[0] = (o * (1.0 - lambda_init)).T.astype(o_ref.dtype)


def diff_attention_core(qkv, q_norm, k_norm, lq1, lk1, lq2, lk2, sub_norm, lambda_init):
    b, s, w3 = qkv.shape
    width = w3 // 3
    heads = width // ATT_V_DIM
    hd = ATT_HEAD_DIM
    tq = _tile(s, TQ_ATT, LANES)
    t = _tile(s, TK_ATT, LANES)
    assert tq % t == 0 and (2 * t) % tq == 0
    vec = lambda a: a.astype(F32).reshape(1, -1)
    qg = jnp.tile(q_norm.astype(F32), 2).reshape(1, 2 * hd)
    kg = jnp.tile(k_norm.astype(F32), 2).reshape(1, 2 * hd)
    small = lambda width_: pl.BlockSpec((1, width_), lambda i, h, j: (0, 0))
    kern = functools.partial(_attn_kernel, tq=tq, t=t, lambda_init=lambda_init)
    return pl.pallas_call(
        kern,
        out_shape=jax.ShapeDtypeStruct((b, s, width), BF16),
        grid=(b, heads, s // tq),
        in_specs=[
            pl.BlockSpec((1, tq, 2 * hd), lambda i, h, j: (i, j, h)),
            pl.BlockSpec((1, s, 2 * hd), lambda i, h, j: (i, 0, heads + h)),
            pl.BlockSpec((1, s, ATT_V_DIM), lambda i, h, j: (i, 0, 2 * heads + h)),
            small(2 * hd), small(2 * hd), small(hd), small(hd), small(hd), small(hd),
            pl.BlockSpec((ATT_V_DIM, 1), lambda i, h, j: (0, 0)),
        ],
        out_specs=pl.BlockSpec((1, tq, ATT_V_DIM), lambda i, h, j: (i, j, h)),
        scratch_shapes=[
            pltpu.VMEM((s, 2 * hd), BF16),
            pltpu.VMEM((s // t, ATT_V_DIM + ATT_SUM_ROWS, t), BF16),
            pltpu.VMEM((2, 1, tq), F32),
            pltpu.VMEM((2, ATT_V_DIM + ATT_SUM_ROWS, tq), F32),
            pltpu.VMEM((2, 2, t, tq), F32), pltpu.VMEM((2, 2, t, tq), BF16),
            pltpu.VMEM((2, 2, 1, tq), F32),
        ],
        compiler_params=_params(("arbitrary", "arbitrary", "arbitrary")),
        name="diff_attention",
    )(qkv, qkv, qkv, qg, kg, vec(lq1), vec(lk1), vec(lq2), vec(lk2),
      sub_norm.astype(F32).reshape(ATT_V_DIM, 1))


def _route_kernel(x_ref, g_ref, r_ref, xn_ref, info_ref, cnt_ref, run_ref, *, n_experts):
    @pl.when(pl.program_id(0) == 0)
    def _():
        run_ref[...] = jnp.zeros_like(run_ref)

    xn = _rms_rows(x_ref[...], g_ref[...])
    xn_ref[...] = xn.astype(BF16)
    xh = xn.astype(BF16)
    xl = (xn - xh.astype(F32)).astype(BF16)
    hi = jnp.dot(xh, r_ref[...], preferred_element_type=F32)
    lo_ = jnp.dot(xl, r_ref[...], preferred_element_type=F32)
    logits = (hi[:, :LANES] + hi[:, LANES:]) + (lo_[:, :LANES] + lo_[:, LANES:])
    lane = lax.broadcasted_iota(jnp.int32, logits.shape, 1)
    logits = jnp.where(lane < n_experts, logits, -jnp.inf)
    m1 = jnp.max(logits, axis=-1, keepdims=True)
    i1 = jnp.min(jnp.where(logits == m1, lane, LANES), axis=-1, keepdims=True)
    rest = jnp.where(lane == i1, -jnp.inf, logits)
    m2 = jnp.max(rest, axis=-1, keepdims=True)
    i2 = jnp.min(jnp.where(rest == m2, lane, LANES), axis=-1, keepdims=True)
    e2 = jnp.exp(m2 - m1)
    g1 = 1.0 / (1.0 + e2)
    g2 = e2 * g1
    tm = logits.shape[0]
    oh1 = (lane == i1).astype(F32)
    oh2 = (lane == i2).astype(F32)
    both = oh1 + oh2
    row = lax.broadcasted_iota(jnp.int32, (tm, tm), 0)
    col = lax.broadcasted_iota(jnp.int32, (tm, tm), 1)
    before = jnp.dot((col < row).astype(BF16), both.astype(BF16),
                     preferred_element_type=F32) + run_ref[0:1, :]
    r1 = jnp.sum(before * oh1, axis=-1, keepdims=True)
    r2 = jnp.sum(before * oh2, axis=-1, keepdims=True)
    run_ref[...] = run_ref[...] + jnp.sum(both, axis=0, keepdims=True)
    cnt_ref[...] = run_ref[...]
    info = jnp.zeros_like(logits)
    for k, val in enumerate((i1.astype(F32), i2.astype(F32), g1, g2, r1, r2)):
        info = jnp.where(lane == k, val, info)
    info_ref[...] = info


def moe_route(h, gain, router):
    t, d = h.shape
    e = router.shape[1]
    tm = _tile(t, TM_ROUTE, SUBLANES)
    rpad = jnp.zeros((d, LANES), F32).at[:, :e].set(router.astype(F32))
    r_hi = rpad.astype(BF16)
    r_parts = jnp.concatenate([r_hi, (rpad - r_hi.astype(F32)).astype(BF16)], axis=1)
    return pl.pallas_call(
        functools.partial(_route_kernel, n_experts=e),
        out_shape=(jax.ShapeDtypeStruct((t, d), BF16), jax.ShapeDtypeStruct((t, LANES), F32),
                   jax.ShapeDtypeStruct((SUBLANES, LANES), F32)),
        grid=(t // tm,),
        in_specs=[
            pl.BlockSpec((tm, d), lambda i: (i, 0)),
            pl.BlockSpec((1, d), lambda i: (0, 0)),
            pl.BlockSpec((d, 2 * LANES), lambda i: (0, 0)),
        ],
        out_specs=(pl.BlockSpec((tm, d), lambda i: (i, 0)),
                   pl.BlockSpec((tm, LANES), lambda i: (i, 0)),
                   pl.BlockSpec((SUBLANES, LANES), lambda i: (0, 0))),
        scratch_shapes=[pltpu.VMEM((SUBLANES, LANES), F32)],
        compiler_params=_params(("arbitrary",)),
        name="moe_route",
    )(h, gain.reshape(1, d), r_parts)


def _new_weights(te_ref, i):
    prev = te_ref[jnp.maximum(i - 1, 0)]
    return jnp.logical_or(i == 0, te_ref[i] != prev)


def _moe_gateup_kernel(te_ref, nu_ref, x_ref, wg_ref, wu_ref, o_ref, wgb_ref, wub_ref):
    i = pl.program_id(1)

    @pl.when(_new_weights(te_ref, i))
    def _():
        wgb_ref[...] = wg_ref[0].astype(BF16)
        wub_ref[...] = wu_ref[0].astype(BF16)

    @pl.when(i < nu_ref[0])
    def _():
        x = x_ref[...]
        a = jnp.dot(x, wgb_ref[...], preferred_element_type=F32)
        b = jnp.dot(x, wub_ref[...], preferred_element_type=F32)
        o_ref[...] = (_silu(a) * b).astype(o_ref.dtype)

    @pl.when(i >= nu_ref[0])
    def _():
        o_ref[...] = jnp.zeros_like(o_ref)


def _moe_down_kernel(te_ref, nu_ref, h_ref, wd_ref, o_ref, wdb_ref):
    i = pl.program_id(1)

    @pl.when(_new_weights(te_ref, i))
    def _():
        wdb_ref[...] = wd_ref[0].astype(BF16)

    @pl.when(i < nu_ref[0])
    def _():
        o_ref[...] = jnp.dot(h_ref[...], wdb_ref[...],
                             preferred_element_type=F32).astype(o_ref.dtype)

    @pl.when(i >= nu_ref[0])
    def _():
        o_ref[...] = jnp.zeros_like(o_ref)


def moe_experts(xs, tile_expert, n_used, w_gate, w_up, w_down, tm):
    slots, d = xs.shape
    e, _, f = w_gate.shape
    n_tiles = slots // tm
    tf = _tile(f, TF_FFN)
    hmid = pl.pallas_call(
        _moe_gateup_kernel,
        out_shape=jax.ShapeDtypeStruct((slots, f), BF16),
        grid_spec=pltpu.PrefetchScalarGridSpec(
            num_scalar_prefetch=2,
            grid=(f // tf, n_tiles),
            in_specs=[
                pl.BlockSpec((tm, d), lambda j, i, te, nu: (i, 0)),
                pl.BlockSpec((1, d, tf), lambda j, i, te, nu: (te[i], 0, j)),
                pl.BlockSpec((1, d, tf), lambda j, i, te, nu: (te[i], 0, j)),
            ],
            out_specs=pl.BlockSpec((tm, tf), lambda j, i, te, nu: (i, j)),
            scratch_shapes=[pltpu.VMEM((d, tf), BF16), pltpu.VMEM((d, tf), BF16)],
        ),
        compiler_params=_params(("arbitrary", "arbitrary")),
        name="moe_gateup",
    )(tile_expert, n_used, xs, w_gate, w_up)

    tn = _tile(d, TN_MOE_DOWN)
    return pl.pallas_call(
        _moe_down_kernel,
        out_shape=jax.ShapeDtypeStruct((slots, d), BF16),
        grid_spec=pltpu.PrefetchScalarGridSpec(
            num_scalar_prefetch=2,
            grid=(d // tn, n_tiles),
            in_specs=[
                pl.BlockSpec((tm, f), lambda j, i, te, nu: (i, 0)),
                pl.BlockSpec((1, f, tn), lambda j, i, te, nu: (te[i], 0, j)),
            ],
            out_specs=pl.BlockSpec((tm, tn), lambda j, i, te, nu: (i, j)),
            scratch_shapes=[pltpu.VMEM((f, tn), BF16)],
        ),
        compiler_params=_params(("arbitrary", "arbitrary")),
        name="moe_down",
    )(tile_expert, n_used, hmid, w_down)


def moe_block(h, gain, router, w_gate, w_up, w_down):
    t, d = h.shape
    e = router.shape[1]
    tm = min(TM_MOE, t)
    xn, info, cnt = moe_route(h, gain, router)
    idx = info[:, :TOP_K].astype(jnp.int32)
    gates = info[:, TOP_K:2 * TOP_K]
    rank = info[:, 2 * TOP_K:3 * TOP_K].astype(jnp.int32)
    counts = cnt[0, :e].astype(jnp.int32)

    tiles_per_e = (counts + tm - 1) // tm
    tile_end = jnp.cumsum(tiles_per_e)
    starts = (tile_end - tiles_per_e) * tm
    slot = (jnp.take(starts, idx.reshape(-1)) + rank.reshape(-1))
    n_tiles = (TOP_K * t) // tm + e
    n_slots = n_tiles * tm
    token_of_slot = jnp.zeros((n_slots,), jnp.int32).at[slot].set(
        jnp.arange(TOP_K * t, dtype=jnp.int32) // TOP_K)
    tile_ids = jnp.arange(n_tiles, dtype=jnp.int32)
    tile_expert = jnp.minimum(jnp.sum((tile_ids[:, None] >= tile_end[None, :]).astype(jnp.int32), axis=1),
                              e - 1)
    n_used = tile_end[-1:].astype(jnp.int32)

    xs = xn.at[token_of_slot].get(mode="promise_in_bounds")
    ys = moe_experts(xs, tile_expert, n_used, w_gate, w_up, w_down, tm)
    slot2 = slot.reshape(t, TOP_K)
    out = h
    for k in range(TOP_K):
        out = out + gates[:, k:k + 1] * ys.at[slot2[:, k]].get(mode="promise_in_bounds").astype(F32)
    return out


def mamba_layer(h, b, s, pre_norm, in_proj, conv_w, conv_b, dt_bias, a_log, d_skip, out_norm,
                out_proj):
    d_inner = out_proj.shape[0]
    d_xbc = conv_w.shape[1]
    heads = dt_bias.shape[0]
    z = norm_matmul(h, pre_norm, in_proj[:, :d_inner].astype(BF16), BF16)
    xbc = norm_matmul(h, pre_norm, in_proj[:, d_inner:d_inner + d_xbc].astype(BF16), BF16)
    w_dt = jnp.zeros((h.shape[1], LANES), BF16).at[:, :heads].set(
        in_proj[:, d_inner + d_xbc:].astype(BF16))
    dt = norm_matmul(h, pre_norm, w_dt, F32)[:, :heads]
    dtt = jnp.swapaxes(dt.reshape(b, s, heads), 1, 2)
    y = ssd_core(z.reshape(b, s, d_inner), xbc.reshape(b, s, d_xbc), dtt, conv_w.astype(F32),
                 conv_b.astype(F32), dt_bias.astype(F32), a_log.astype(F32), d_skip, out_norm)
    return matmul_residual(y.reshape(b * s, d_inner), out_proj.astype(BF16), h)


def dense_layer(h, pre_norm, w_gate, w_up, w_down):
    mid = norm_gateup(h, pre_norm, w_gate.astype(BF16), w_up.astype(BF16))
    return matmul_residual(mid, w_down.astype(BF16), h)


def attention_layer(h, b, s, pre_norm, w_qkv, q_norm, k_norm, lq1, lk1, lq2, lk2, sub_norm, w_o,
                    lambda_init):
    qkv = norm_matmul(h, pre_norm, w_qkv.astype(BF16), BF16)
    o = diff_attention_core(qkv.reshape(b, s, -1), q_norm, k_norm, lq1, lk1, lq2, lk2, sub_norm,
                            lambda_init)
    return matmul_residual(o.reshape(b * s, -1), w_o.astype(BF16), h)


def kernel(x, ssm_pre_norm, ssm_in_proj, ssm_conv_w, ssm_conv_b, ssm_dt_bias, ssm_A_log, ssm_D,
           ssm_out_norm, ssm_out_proj, dense_pre_norm, dense_w_gate, dense_w_up, dense_w_down,
           att_pre_norm, att_w_qkv, att_q_norm, att_k_norm, att_lambda_q1, att_lambda_k1,
           att_lambda_q2, att_lambda_k2, att_sub_norm, att_w_o, moe_pre_norm, moe_router,
           moe_w_gate, moe_w_up, moe_w_down):
    b, s, d = x.shape
    h = x.reshape(b * s, d)
    for i in range(DEPTH):
        j = i // N_MIXERS
        if i % N_MIXERS == 0:
            h = mamba_layer(h, b, s, ssm_pre_norm[j], ssm_in_proj[j], ssm_conv_w[j],
                            ssm_conv_b[j], ssm_dt_bias[j], ssm_A_log[j], ssm_D[j],
                            ssm_out_norm[j], ssm_out_proj[j])
            h = dense_layer(h, dense_pre_norm[j], dense_w_gate[j], dense_w_up[j],
                            dense_w_down[j])
        else:
            lambda_init = 0.8 - 0.6 * math.exp(-0.3 * i)
            h = attention_layer(h, b, s, att_pre_norm[j], att_w_qkv[j], att_q_norm[j],
                                att_k_norm[j], att_lambda_q1[j], att_lambda_k1[j],
                                att_lambda_q2[j], att_lambda_k2[j], att_sub_norm[j], att_w_o[j],
                                lambda_init)
            h = moe_block(h, moe_pre_norm[j], moe_router[j], moe_w_gate[j], moe_w_up[j],
                          moe_w_down[j])
    return h.reshape(b, s, d)
```

```python
import functools
import math

import jax
import jax.numpy as jnp
from jax import lax
from jax.experimental import pallas as pl
from jax.experimental.pallas import tpu as pltpu

F32 = jnp.float32
BF16 = jnp.bfloat16

EPS = 1e-5
DEPTH = 2
N_MIXERS = 2

SSM_GROUPS = 8
SSM_STATE = 128
CHUNK = 128
CONV_WIDTH = 4
SSM_HEAD_DIM = 64
CONV_TAIL = 16

ATT_HEAD_DIM = 64
ATT_V_DIM = 2 * ATT_HEAD_DIM
ATT_SUM_ROWS = 16

TOP_K = 2

LOG2E = math.log2(math.e)

LANES = 128
SUBLANES = 8
VMEM_LIMIT_BYTES = 56 * 1024 * 1024

TM_PROJ = 1024
TN_PROJ = 1024
TM_ROUTE = 512
RES_VMEM_BUDGET = 46 * 1024 * 1024
TF_FFN = 512
SSD_CHUNKS_PER_STEP = 8
TM_MOE = 512
TN_MOE_DOWN = 512
TQ_ATT = 512
TK_ATT = 256


def _tile(dim, pref, quantum=LANES):
    if dim <= pref:
        return dim
    t = (pref // quantum) * quantum
    while t >= quantum:
        if dim % t == 0:
            return t
        t -= quantum
    return dim


def _params(sem):
    return pltpu.CompilerParams(dimension_semantics=sem, vmem_limit_bytes=VMEM_LIMIT_BYTES)


def _rms_rows(x, gain):
    return x * lax.rsqrt(jnp.mean(x * x, axis=-1, keepdims=True) + EPS) * gain


def _silu(x):
    h = 0.5 * x
    return h + h * jnp.tanh(h)


def _split3(v):
    hi = v.astype(BF16).astype(F32)
    r = v - hi
    mid = r.astype(BF16).astype(F32)
    return [hi, mid, r - mid]


def _norm_mm_kernel(x_ref, g_ref, w_ref, o_ref, xn_ref):
    @pl.when(pl.program_id(1) == 0)
    def _():
        xn_ref[...] = _rms_rows(x_ref[...], g_ref[...]).astype(BF16)

    res = jnp.dot(xn_ref[...], w_ref[...], preferred_element_type=F32).astype(o_ref.dtype)
    if len(o_ref.shape) == 2:
        o_ref[...] = res
    else:
        for hh in range(o_ref.shape[0]):
            o_ref[hh] = res[:, hh * LANES:(hh + 1) * LANES]


def norm_matmul(x, gain, w, out_dtype, col0=0, n=None, head_major=False):
    t, d = x.shape
    n = w.shape[1] if n is None else n
    tm, tn = _tile(t, TM_PROJ, SUBLANES), _tile(math.gcd(n, col0) if col0 else n, TN_PROJ)
    assert col0 % tn == 0 and n % tn == 0
    j0 = col0 // tn
    if head_major:
        out_shape = jax.ShapeDtypeStruct((n // LANES, t, LANES), out_dtype)
        out_spec = pl.BlockSpec((tn // LANES, tm, LANES), lambda i, j: (j, i, 0))
    else:
        out_shape = jax.ShapeDtypeStruct((t, n), out_dtype)
        out_spec = pl.BlockSpec((tm, tn), lambda i, j: (i, j))
    return pl.pallas_call(
        _norm_mm_kernel,
        out_shape=out_shape,
        grid=(t // tm, n // tn),
        in_specs=[
            pl.BlockSpec((tm, d), lambda i, j: (i, 0)),
            pl.BlockSpec((1, d), lambda i, j: (0, 0)),
            pl.BlockSpec((d, tn), lambda i, j: (0, j0 + j)),
        ],
        out_specs=out_spec,
        scratch_shapes=[pltpu.VMEM((tm, d), BF16)],
        compiler_params=_params(("arbitrary", "arbitrary")),
        name="norm_matmul",
    )(x, gain.reshape(1, d), w)


def _norm_gateup_kernel(x_ref, g_ref, wg_ref, wu_ref, o_ref, xn_ref):
    @pl.when(pl.program_id(1) == 0)
    def _():
        xn_ref[...] = _rms_rows(x_ref[...], g_ref[...]).astype(BF16)

    xn = xn_ref[...]
    a = jnp.dot(xn, wg_ref[...], preferred_element_type=F32)
    b = jnp.dot(xn, wu_ref[...], preferred_element_type=F32)
    o_ref[...] = (_silu(a) * b).astype(o_ref.dtype)


def norm_gateup(x, gain, wg, wu):
    t, d = x.shape
    f = wg.shape[1]
    tm, tf = _tile(t, TM_PROJ, SUBLANES), _tile(f, TF_FFN)
    return pl.pallas_call(
        _norm_gateup_kernel,
        out_shape=jax.ShapeDtypeStruct((t, f), BF16),
        grid=(t // tm, f // tf),
        in_specs=[
            pl.BlockSpec((tm, d), lambda i, j: (i, 0)),
            pl.BlockSpec((1, d), lambda i, j: (0, 0)),
            pl.BlockSpec((d, tf), lambda i, j: (0, j)),
            pl.BlockSpec((d, tf), lambda i, j: (0, j)),
        ],
        out_specs=pl.BlockSpec((tm, tf), lambda i, j: (i, j)),
        scratch_shapes=[pltpu.VMEM((tm, d), BF16)],
        compiler_params=_params(("arbitrary", "arbitrary")),
        name="norm_gateup",
    )(x, gain.reshape(1, d), wg, wu)


def _mm_res_kernel(a_ref, w_ref, r_ref, o_ref):
    if len(a_ref.shape) == 2:
        a = a_ref[...]
    else:
        a = jnp.concatenate([a_ref[hh] for hh in range(a_ref.shape[0])], axis=1)
    o_ref[...] = r_ref[...] + jnp.dot(a, w_ref[...], preferred_element_type=F32)


def matmul_residual(a, w, res):
    k, n = w.shape
    t = res.shape[0]
    tm = t
    for cand in (1024, 512, 256, 128):
        need = k * n * 2 + 2 * cand * k * 2 + 4 * cand * n * 4 + cand * n * 4
        if t % cand == 0 and need <= RES_VMEM_BUDGET:
            tm = cand
            break
    return pl.pallas_call(
        _mm_res_kernel,
        out_shape=jax.ShapeDtypeStruct((t, n), F32),
        grid=(t // tm,),
        in_specs=[
            (pl.BlockSpec((tm, k), lambda i: (i, 0)) if a.ndim == 2 else
             pl.BlockSpec((a.shape[0], tm, LANES), lambda i: (0, i, 0))),
            pl.BlockSpec((k, n), lambda i: (0, 0), pipeline_mode=pl.Buffered(1)),
            pl.BlockSpec((tm, n), lambda i: (i, 0)),
        ],
        out_specs=pl.BlockSpec((tm, n), lambda i: (i, 0)),
        compiler_params=_params(("arbitrary",)),
        name="matmul_residual",
    )(a, w, res)


def _ssd_kernel(x_ref, bm_ref, cm_ref, z_ref, dtt_ref, cwx_ref, cbx_ref, cwb_ref, cbb_ref,
                cwc_ref, cbc_ref, dtb_ref, alog_ref, dx_ref, gn_ref, e_ref, sh_ref,
                o_ref, pad_ref, h_ref, *, hpg, gw, nc):
    L = CHUNK
    n = SSM_STATE
    hd = gw // hpg
    rows = nc * L
    c = pl.program_id(2)
    tail = CONV_TAIL

    @pl.when(c == 0)
    def _():
        pad_ref[0:tail, :] = jnp.zeros((tail, gw + 2 * n), BF16)
        h_ref[...] = jnp.zeros_like(h_ref)

    @pl.when(c > 0)
    def _():
        pad_ref[0:tail, :] = pad_ref[rows:rows + tail, :]

    pad_ref[tail:tail + rows, 0:gw] = x_ref[0]
    pad_ref[tail:tail + rows, gw:gw + n] = bm_ref[0]
    pad_ref[tail:tail + rows, gw + n:gw + 2 * n] = cm_ref[0]

    def conv(shifted, t0, col0, width, w_ref, b_ref):
        acc = b_ref[...] + w_ref[CONV_WIDTH - 1:CONV_WIDTH, :] * pad_ref[
            t0 + tail:t0 + tail + L, col0:col0 + width].astype(F32)
        for k in range(CONV_WIDTH - 1):
            acc = acc + w_ref[k:k + 1, :] * shifted[k * L:(k + 1) * L, col0:col0 + width]
        return _silu(acc)

    raw = dtt_ref[0] + dtb_ref[...]
    dt_all = jnp.maximum(raw, 0.0) + jnp.log1p(jnp.exp(-jnp.abs(raw)))
    a_all = dt_all * (-jnp.exp(alog_ref[...]))
    row = lax.broadcasted_iota(jnp.int32, (L, L), 0)
    col = lax.broadcasted_iota(jnp.int32, (L, L), 1)
    upper = (row <= col).astype(F32)
    causal = row >= col
    lane = lax.broadcasted_iota(jnp.int32, (1, 2 * hd), 1)
    lo = lane < hd

    for ci in range(nc):
        t0 = ci * L
        shifted = jnp.dot(sh_ref[...], pad_ref[t0:t0 + tail + L, :],
                          preferred_element_type=F32)
        xs = conv(shifted, t0, 0, gw, cwx_ref, cbx_ref)
        bc = conv(shifted, t0, gw, n, cwb_ref, cbb_ref)
        cc = conv(shifted, t0, gw + n, n, cwc_ref, cbc_ref)

        dt = dt_all[:, t0:t0 + L]
        acs = jnp.dot(a_all[:, t0:t0 + L], upper, preferred_element_type=F32,
                      precision=lax.Precision.HIGHEST)
        a_last = acs[:, L - 1:L]
        dte = jnp.exp(a_last - acs)
        eacs = jnp.exp(acs)

        parts = _split3(acs) + _split3(dt) + _split3(dt * dte) + _split3(eacs)
        stacked = jnp.concatenate(parts, axis=0).T.astype(BF16)
        spread = jnp.dot(stacked, e_ref[...], preferred_element_type=F32)
        acs_col = spread[:, :hpg * L]
        dt_x = spread[:, hpg * L:hpg * L + gw]
        w_x = spread[:, hpg * L + gw:hpg * L + 2 * gw]
        eacs_x = spread[:, hpg * L + 2 * gw:]

        xdt = xs * dt_x
        bct = bc.T.astype(BF16)
        ccb = cc.astype(BF16)
        cb = jnp.dot(ccb, bct, preferred_element_type=F32)

        y_parts = []
        for q in range(hpg // 2):
            ms = []
            for j in (2 * q, 2 * q + 1):
                seg = acs_col[:, j * L:(j + 1) * L] - acs[j:j + 1, :]
                dec = jnp.exp(jnp.where(causal, seg, -jnp.inf))
                ms.append((cb * dec).astype(BF16))
            mcat = jnp.concatenate(ms, axis=1)
            xq = xdt[:, q * 2 * hd:(q + 1) * 2 * hd]
            rhs = jnp.concatenate([jnp.where(lo, xq, 0.0), jnp.where(lo, 0.0, xq)], axis=0)
            y_parts.append(jnp.dot(mcat, rhs.astype(BF16), preferred_element_type=F32))
        y_diag = jnp.concatenate(y_parts, axis=1)

        h_in = h_ref[...]
        y_off = jnp.dot(ccb, h_in.astype(BF16), preferred_element_type=F32) * eacs_x
        states = jnp.dot(bct, (xs * w_x).astype(BF16), preferred_element_type=F32)
        h_ref[...] = h_in * eacs_x[L - 1:L, :] + states

        y = y_diag + y_off + dx_ref[...] * xs
        gated = y * _silu(z_ref[0, t0:t0 + L, :].astype(F32))
        o_ref[0, t0:t0 + L, :] = _rms_rows(gated, gn_ref[...]).astype(o_ref.dtype)


def ssd_core(z, xbc, dtt, conv_w, conv_b, dt_bias, a_log, d_skip, out_norm):
    b, s, d_inner = z.shape
    g = SSM_GROUPS
    n = SSM_STATE
    heads = dt_bias.shape[0]
    hpg = heads // g
    gw = d_inner // g
    hd = gw // hpg
    L = CHUNK
    nb = d_inner // n
    assert gw % LANES == 0 and hpg % 2 == 0 and 2 * hd == LANES and s % L == 0

    eye = jnp.eye(hpg, dtype=F32)
    widths = [hpg * L, gw, gw, gw]
    mats = [jnp.repeat(eye, L, axis=1)] + [jnp.repeat(eye, hd, axis=1)] * 3
    spread_rows = []
    for qty in range(4):
        blk = jnp.concatenate([mats[qty] if c == qty else jnp.zeros((hpg, widths[c]), F32)
                               for c in range(4)], axis=1)
        spread_rows += [blk] * 3
    spread_mat = jnp.concatenate(spread_rows, axis=0).astype(BF16)
    dx = jnp.repeat(d_skip.astype(F32), hd).reshape(1, d_inner)

    nc = SSD_CHUNKS_PER_STEP if (s // L) % SSD_CHUNKS_PER_STEP == 0 else 1
    rows = nc * L
    out_row = jnp.arange((CONV_WIDTH - 1) * L, dtype=jnp.int32)
    src_row = out_row % L + CONV_TAIL - (CONV_WIDTH - 1) + out_row // L
    shift_mat = (src_row[:, None] == jnp.arange(CONV_TAIL + L, dtype=jnp.int32)[None, :]).astype(BF16)
    kern = functools.partial(_ssd_kernel, hpg=hpg, gw=gw, nc=nc)
    return pl.pallas_call(
        kern,
        out_shape=jax.ShapeDtypeStruct((b, s, d_inner), BF16),
        grid=(b, g, s // rows),
        in_specs=[
            pl.BlockSpec((1, rows, gw), lambda i, j, c: (i, c, j)),
            pl.BlockSpec((1, rows, n), lambda i, j, c: (i, c, nb + j)),
            pl.BlockSpec((1, rows, n), lambda i, j, c: (i, c, nb + g + j)),
            pl.BlockSpec((1, rows, gw), lambda i, j, c: (i, c, j)),
            pl.BlockSpec((1, hpg, rows), lambda i, j, c: (i, j, c)),
            pl.BlockSpec((CONV_WIDTH, gw), lambda i, j, c: (0, j)),
            pl.BlockSpec((1, gw), lambda i, j, c: (0, j)),
            pl.BlockSpec((CONV_WIDTH, n), lambda i, j, c: (0, nb + j)),
            pl.BlockSpec((1, n), lambda i, j, c: (0, nb + j)),
            pl.BlockSpec((CONV_WIDTH, n), lambda i, j, c: (0, nb + g + j)),
            pl.BlockSpec((1, n), lambda i, j, c: (0, nb + g + j)),
            pl.BlockSpec((hpg, 1), lambda i, j, c: (j, 0)),
            pl.BlockSpec((hpg, 1), lambda i, j, c: (j, 0)),
            pl.BlockSpec((1, gw), lambda i, j, c: (0, j)),
            pl.BlockSpec((1, gw), lambda i, j, c: (0, j)),
            pl.BlockSpec((12 * hpg, hpg * L + 3 * gw), lambda i, j, c: (0, 0)),
            pl.BlockSpec(((CONV_WIDTH - 1) * L, CONV_TAIL + L), lambda i, j, c: (0, 0)),
        ],
        out_specs=pl.BlockSpec((1, rows, gw), lambda i, j, c: (i, c, j)),
        scratch_shapes=[pltpu.VMEM((rows + CONV_TAIL, gw + 2 * n), BF16), pltpu.VMEM((n, gw), F32)],
        compiler_params=_params(("arbitrary", "arbitrary", "arbitrary")),
        name="ssd_core",
    )(xbc, xbc, xbc, z, dtt, conv_w, conv_b.reshape(1, -1), conv_w, conv_b.reshape(1, -1),
      conv_w, conv_b.reshape(1, -1), dt_bias.reshape(heads, 1), a_log.reshape(heads, 1),
      dx, out_norm.reshape(1, d_inner), spread_mat, shift_mat)


def _attn_kernel(q_ref, k_ref, v_ref, qg_ref, kg_ref, lq1_ref, lk1_ref, lq2_ref, lk2_ref,
                 sg_ref, o_ref, kn_ref, vt_ref, m_ref, acc_ref, s_ref, p_ref, a_ref,
                 *, tq, t, lambda_init):
    s_len = k_ref.shape[1]
    n_tiles = s_len // t
    hd = ATT_HEAD_DIM
    qi = pl.program_id(2)
    lane = lax.broadcasted_iota(jnp.int32, (1, 2 * hd), 1)
    lo = lane < hd

    def comp_norm(x, gain):
        x2 = x * x
        s_lo = jnp.sum(jnp.where(lo, x2, 0.0), axis=-1, keepdims=True)
        s_hi = jnp.sum(jnp.where(lo, 0.0, x2), axis=-1, keepdims=True)
        inv = jnp.where(lo, lax.rsqrt(s_lo * (1.0 / hd) + EPS), lax.rsqrt(s_hi * (1.0 / hd) + EPS))
        return x * inv * gain

    @pl.when(qi == 0)
    def _():
        def body(i, carry):
            r0 = pl.multiple_of(i * t, t)
            kn = comp_norm(k_ref[0, pl.ds(r0, t), :].astype(F32), kg_ref[...])
            kn_ref[pl.ds(r0, t), :] = kn.astype(BF16)
            vt_ref[i, 0:ATT_V_DIM, :] = v_ref[0, pl.ds(r0, t), :].astype(F32).T.astype(BF16)
            vt_ref[i, ATT_V_DIM:, :] = jnp.ones((ATT_SUM_ROWS, t), BF16)
            return carry

        lax.fori_loop(0, s_len // t, body, 0)

    qn = comp_norm(q_ref[0].astype(F32), qg_ref[...]) * (hd ** -0.5 * LOG2E)
    qc = (jnp.where(lo, qn, 0.0).astype(BF16), jnp.where(lo, 0.0, qn).astype(BF16))

    m_ref[...] = jnp.full(m_ref.shape, -jnp.inf, F32)
    acc_ref[...] = jnp.zeros(acc_ref.shape, F32)

    def scores(tile, slot):
        r0 = pl.multiple_of(jnp.minimum(tile, n_tiles - 1) * t, t)
        kk = kn_ref[pl.ds(r0, t), :]
        for c in range(2):
            s_ref[slot, c] = lax.dot_general(kk, qc[c], (((1,), (1,)), ((), ())),
                                             preferred_element_type=F32)

    def softmax(tile, slot, masked):
        for c in range(2):
            sc = s_ref[slot, c]
            if masked:
                key = tile * t + lax.broadcasted_iota(jnp.int32, (t, tq), 0)
                qry = qi * tq + lax.broadcasted_iota(jnp.int32, (t, tq), 1)
                sc = jnp.where(key <= qry, sc, -jnp.inf)
            m_prev = m_ref[c]
            m_new = jnp.maximum(m_prev, jnp.max(sc, axis=0, keepdims=True))
            alpha = jnp.exp2(m_prev - m_new)
            p = jnp.exp2(sc - m_new)
            m_ref[c] = m_new
            a_ref[slot, c] = alpha
            p_ref[slot, c] = p.astype(BF16)

    def update(tile0, tile1):
        vt0 = vt_ref[jnp.minimum(tile0, n_tiles - 1)]
        vt1 = vt_ref[jnp.minimum(tile1, n_tiles - 1)]
        for c in range(2):
            a1 = a_ref[1, c]
            pv0 = jnp.dot(vt0, p_ref[0, c], preferred_element_type=F32)
            pv1 = jnp.dot(vt1, p_ref[1, c], preferred_element_type=F32)
            acc_ref[c] = (a_ref[0, c] * a1) * acc_ref[c] + a1 * pv0 + pv1

    d0 = 2 * ((qi * tq) // (2 * t))
    n_pairs = d0 // 2 + 1
    scores(d0, 0)
    scores(d0 + 1, 1)
    softmax(d0, 0, True)
    scores(0, 0)
    softmax(d0 + 1, 1, True)

    def pair(u, carry):
        x = 2 * u - 2
        first = u == 1
        update(jnp.where(first, d0, x - 2), jnp.where(first, d0 + 1, x - 1))
        scores(x + 1, 1)
        softmax(x, 0, False)
        scores(x + 2, 0)
        softmax(x + 1, 1, False)
        return carry

    lax.fori_loop(1, n_pairs, pair, 0)
    only = n_pairs == 1
    update(jnp.where(only, d0, d0 - 2), jnp.where(only, d0 + 1, d0 - 1))

    lam = (jnp.exp(jnp.sum(lq1_ref[...] * lk1_ref[...], axis=-1, keepdims=True))
           - jnp.exp(jnp.sum(lq2_ref[...] * lk2_ref[...], axis=-1, keepdims=True)) + lambda_init)
    e = ATT_V_DIM
    o = (acc_ref[0, 0:e, :] / acc_ref[0, e:e + 1, :]
         - lam * (acc_ref[1, 0:e, :] / acc_ref[1, e:e + 1, :]))
    o = o * lax.rsqrt(jnp.mean(o * o, axis=0, keepdims=True) + EPS) * sg_ref[...]
    o_ref[0] = (o * (1.0 - lambda_init)).T.astype(o_ref.dtype)


def diff_attention_core(qkv, b, s, q_norm, k_norm, lq1, lk1, lq2, lk2, sub_norm, lambda_init):
    heads = qkv.shape[0] // 3
    hd = ATT_HEAD_DIM
    tq = _tile(s, TQ_ATT, LANES)
    t = _tile(s, TK_ATT, LANES)
    assert tq % t == 0 and (2 * t) % tq == 0
    vec = lambda a: a.astype(F32).reshape(1, -1)
    qg = jnp.tile(q_norm.astype(F32), 2).reshape(1, 2 * hd)
    kg = jnp.tile(k_norm.astype(F32), 2).reshape(1, 2 * hd)
    small = lambda width_: pl.BlockSpec((1, width_), lambda i, h, j: (0, 0))
    kern = functools.partial(_attn_kernel, tq=tq, t=t, lambda_init=lambda_init)
    return pl.pallas_call(
        kern,
        out_shape=jax.ShapeDtypeStruct((heads, b * s, ATT_V_DIM), BF16),
        grid=(b, heads, s // tq),
        in_specs=[
            pl.BlockSpec((1, tq, 2 * hd), lambda i, h, j: (h, i * (s // tq) + j, 0)),
            pl.BlockSpec((1, s, 2 * hd), lambda i, h, j: (heads + h, i, 0)),
            pl.BlockSpec((1, s, ATT_V_DIM), lambda i, h, j: (2 * heads + h, i, 0)),
            small(2 * hd), small(2 * hd), small(hd), small(hd), small(hd), small(hd),
            pl.BlockSpec((ATT_V_DIM, 1), lambda i, h, j: (0, 0)),
        ],
        out_specs=pl.BlockSpec((1, tq, ATT_V_DIM), lambda i, h, j: (h, i * (s // tq) + j, 0)),
        scratch_shapes=[
            pltpu.VMEM((s, 2 * hd), BF16),
            pltpu.VMEM((s // t, ATT_V_DIM + ATT_SUM_ROWS, t), BF16),
            pltpu.VMEM((2, 1, tq), F32),
            pltpu.VMEM((2, ATT_V_DIM + ATT_SUM_ROWS, tq), F32),
            pltpu.VMEM((2, 2, t, tq), F32), pltpu.VMEM((2, 2, t, tq), BF16),
            pltpu.VMEM((2, 2, 1, tq), F32),
        ],
        compiler_params=_params(("arbitrary", "arbitrary", "arbitrary")),
        name="diff_attention",
    )(qkv, qkv, qkv, qg, kg, vec(lq1), vec(lk1), vec(lq2), vec(lk2),
      sub_norm.astype(F32).reshape(ATT_V_DIM, 1))


def _route_kernel(x_ref, g_ref, r_ref, xn_ref, info_ref, cnt_ref, run_ref, *, n_experts):
    @pl.when(pl.program_id(0) == 0)
    def _():
        run_ref[...] = jnp.zeros_like(run_ref)

    xn = _rms_rows(x_ref[...], g_ref[...])
    xn_ref[...] = xn.astype(BF16)
    xh = xn.astype(BF16)
    xl = (xn - xh.astype(F32)).astype(BF16)
    hi = jnp.dot(xh, r_ref[...], preferred_element_type=F32)
    lo_ = jnp.dot(xl, r_ref[...], preferred_element_type=F32)
    logits = (hi[:, :LANES] + hi[:, LANES:]) + (lo_[:, :LANES] + lo_[:, LANES:])
    lane = lax.broadcasted_iota(jnp.int32, logits.shape, 1)
    logits = jnp.where(lane < n_experts, logits, -jnp.inf)
    m1 = jnp.max(logits, axis=-1, keepdims=True)
    i1 = jnp.min(jnp.where(logits == m1, lane, LANES), axis=-1, keepdims=True)
    rest = jnp.where(lane == i1, -jnp.inf, logits)
    m2 = jnp.max(rest, axis=-1, keepdims=True)
    i2 = jnp.min(jnp.where(rest == m2, lane, LANES), axis=-1, keepdims=True)
    e2 = jnp.exp(m2 - m1)
    g1 = 1.0 / (1.0 + e2)
    g2 = e2 * g1
    tm = logits.shape[0]
    oh1 = (lane == i1).astype(F32)
    oh2 = (lane == i2).astype(F32)
    both = oh1 + oh2
    row = lax.broadcasted_iota(jnp.int32, (tm, tm), 0)
    col = lax.broadcasted_iota(jnp.int32, (tm, tm), 1)
    before = jnp.dot((col < row).astype(BF16), both.astype(BF16),
                     preferred_element_type=F32) + run_ref[0:1, :]
    r1 = jnp.sum(before * oh1, axis=-1, keepdims=True)
    r2 = jnp.sum(before * oh2, axis=-1, keepdims=True)
    run_ref[...] = run_ref[...] + jnp.sum(both, axis=0, keepdims=True)
    cnt_ref[...] = run_ref[...]
    info = jnp.zeros_like(logits)
    for k, val in enumerate((i1.astype(F32), i2.astype(F32), g1, g2, r1, r2)):
        info = jnp.where(lane == k, val, info)
    info_ref[...] = info


def moe_route(h, gain, router):
    t, d = h.shape
    e = router.shape[1]
    tm = _tile(t, TM_ROUTE, SUBLANES)
    rpad = jnp.zeros((d, LANES), F32).at[:, :e].set(router.astype(F32))
    r_hi = rpad.astype(BF16)
    r_parts = jnp.concatenate([r_hi, (rpad - r_hi.astype(F32)).astype(BF16)], axis=1)
    return pl.pallas_call(
        functools.partial(_route_kernel, n_experts=e),
        out_shape=(jax.ShapeDtypeStruct((t, d), BF16), jax.ShapeDtypeStruct((t, LANES), F32),
                   jax.ShapeDtypeStruct((SUBLANES, LANES), F32)),
        grid=(t // tm,),
        in_specs=[
            pl.BlockSpec((tm, d), lambda i: (i, 0)),
            pl.BlockSpec((1, d), lambda i: (0, 0)),
            pl.BlockSpec((d, 2 * LANES), lambda i: (0, 0)),
        ],
        out_specs=(pl.BlockSpec((tm, d), lambda i: (i, 0)),
                   pl.BlockSpec((tm, LANES), lambda i: (i, 0)),
                   pl.BlockSpec((SUBLANES, LANES), lambda i: (0, 0))),
        scratch_shapes=[pltpu.VMEM((SUBLANES, LANES), F32)],
        compiler_params=_params(("arbitrary",)),
        name="moe_route",
    )(h, gain.reshape(1, d), r_parts)


def _new_weights(te_ref, i):
    prev = te_ref[jnp.maximum(i - 1, 0)]
    return jnp.logical_or(i == 0, te_ref[i] != prev)


def _moe_gateup_kernel(te_ref, nu_ref, x_ref, wg_ref, wu_ref, o_ref, wgb_ref, wub_ref):
    i = pl.program_id(1)

    @pl.when(_new_weights(te_ref, i))
    def _():
        wgb_ref[...] = wg_ref[0].astype(BF16)
        wub_ref[...] = wu_ref[0].astype(BF16)

    @pl.when(i < nu_ref[0])
    def _():
        x = x_ref[...]
        a = jnp.dot(x, wgb_ref[...], preferred_element_type=F32)
        b = jnp.dot(x, wub_ref[...], preferred_element_type=F32)
        o_ref[...] = (_silu(a) * b).astype(o_ref.dtype)

    @pl.when(i >= nu_ref[0])
    def _():
        o_ref[...] = jnp.zeros_like(o_ref)


def _moe_down_kernel(te_ref, nu_ref, h_ref, wd_ref, o_ref, wdb_ref):
    i = pl.program_id(1)

    @pl.when(_new_weights(te_ref, i))
    def _():
        wdb_ref[...] = wd_ref[0].astype(BF16)

    @pl.when(i < nu_ref[0])
    def _():
        o_ref[...] = jnp.dot(h_ref[...], wdb_ref[...],
                             preferred_element_type=F32).astype(o_ref.dtype)

    @pl.when(i >= nu_ref[0])
    def _():
        o_ref[...] = jnp.zeros_like(o_ref)


def moe_experts(xs, tile_expert, n_used, w_gate, w_up, w_down, tm):
    slots, d = xs.shape
    e, _, f = w_gate.shape
    n_tiles = slots // tm
    tf = _tile(f, TF_FFN)
    hmid = pl.pallas_call(
        _moe_gateup_kernel,
        out_shape=jax.ShapeDtypeStruct((slots, f), BF16),
        grid_spec=pltpu.PrefetchScalarGridSpec(
            num_scalar_prefetch=2,
            grid=(f // tf, n_tiles),
            in_specs=[
                pl.BlockSpec((tm, d), lambda j, i, te, nu: (i, 0)),
                pl.BlockSpec((1, d, tf), lambda j, i, te, nu: (te[i], 0, j)),
                pl.BlockSpec((1, d, tf), lambda j, i, te, nu: (te[i], 0, j)),
            ],
            out_specs=pl.BlockSpec((tm, tf), lambda j, i, te, nu: (i, j)),
            scratch_shapes=[pltpu.VMEM((d, tf), BF16), pltpu.VMEM((d, tf), BF16)],
        ),
        compiler_params=_params(("arbitrary", "arbitrary")),
        name="moe_gateup",
    )(tile_expert, n_used, xs, w_gate, w_up)

    tn = _tile(d, TN_MOE_DOWN)
    return pl.pallas_call(
        _moe_down_kernel,
        out_shape=jax.ShapeDtypeStruct((slots, d), BF16),
        grid_spec=pltpu.PrefetchScalarGridSpec(
            num_scalar_prefetch=2,
            grid=(d // tn, n_tiles),
            in_specs=[
                pl.BlockSpec((tm, f), lambda j, i, te, nu: (i, 0)),
                pl.BlockSpec((1, f, tn), lambda j, i, te, nu: (te[i], 0, j)),
            ],
            out_specs=pl.BlockSpec((tm, tn), lambda j, i, te, nu: (i, j)),
            scratch_shapes=[pltpu.VMEM((f, tn), BF16)],
        ),
        compiler_params=_params(("arbitrary", "arbitrary")),
        name="moe_down",
    )(tile_expert, n_used, hmid, w_down)


def moe_block(h, gain, router, w_gate, w_up, w_down):
    t, d = h.shape
    e = router.shape[1]
    tm = min(TM_MOE, t)
    xn, info, cnt = moe_route(h, gain, router)
    idx = info[:, :TOP_K].astype(jnp.int32)
    gates = info[:, TOP_K:2 * TOP_K]
    rank = info[:, 2 * TOP_K:3 * TOP_K].astype(jnp.int32)
    counts = cnt[0, :e].astype(jnp.int32)

    tiles_per_e = (counts + tm - 1) // tm
    tile_end = jnp.cumsum(tiles_per_e)
    starts = (tile_end - tiles_per_e) * tm
    slot = (jnp.take(starts, idx.reshape(-1)) + rank.reshape(-1))
    n_tiles = (TOP_K * t) // tm + e
    n_slots = n_tiles * tm
    token_of_slot = jnp.zeros((n_slots,), jnp.int32).at[slot].set(
        jnp.arange(TOP_K * t, dtype=jnp.int32) // TOP_K)
    tile_ids = jnp.arange(n_tiles, dtype=jnp.int32)
    tile_expert = jnp.minimum(jnp.sum((tile_ids[:, None] >= tile_end[None, :]).astype(jnp.int32), axis=1),
                              e - 1)
    n_used = tile_end[-1:].astype(jnp.int32)

    xs = xn.at[token_of_slot].get(mode="promise_in_bounds")
    ys = moe_experts(xs, tile_expert, n_used, w_gate, w_up, w_down, tm)
    slot2 = slot.reshape(t, TOP_K)
    out = h
    for k in range(TOP_K):
        out = out + gates[:, k:k + 1] * ys.at[slot2[:, k]].get(mode="promise_in_bounds").astype(F32)
    return out


def mamba_layer(h, b, s, pre_norm, in_proj, conv_w, conv_b, dt_bias, a_log, d_skip, out_norm,
                out_proj):
    d_inner = out_proj.shape[0]
    d_xbc = conv_w.shape[1]
    heads = dt_bias.shape[0]
    w = in_proj.astype(BF16)
    z = norm_matmul(h, pre_norm, w, BF16, col0=0, n=d_inner)
    xbc = norm_matmul(h, pre_norm, w, BF16, col0=d_inner, n=d_xbc)
    w_dt = jnp.zeros((h.shape[1], LANES), BF16).at[:, :heads].set(w[:, d_inner + d_xbc:])
    dt = norm_matmul(h, pre_norm, w_dt, F32)[:, :heads]
    dtt = jnp.swapaxes(dt.reshape(b, s, heads), 1, 2)
    y = ssd_core(z.reshape(b, s, d_inner), xbc.reshape(b, s, d_xbc), dtt, conv_w.astype(F32),
                 conv_b.astype(F32), dt_bias.astype(F32), a_log.astype(F32), d_skip, out_norm)
    return matmul_residual(y.reshape(b * s, d_inner), out_proj.astype(BF16), h)


def dense_layer(h, pre_norm, w_gate, w_up, w_down):
    mid = norm_gateup(h, pre_norm, w_gate.astype(BF16), w_up.astype(BF16))
    return matmul_residual(mid, w_down.astype(BF16), h)


def attention_layer(h, b, s, pre_norm, w_qkv, q_norm, k_norm, lq1, lk1, lq2, lk2, sub_norm, w_o,
                    lambda_init):
    qkv = norm_matmul(h, pre_norm, w_qkv.astype(BF16), BF16, head_major=True)
    o = diff_attention_core(qkv, b, s, q_norm, k_norm, lq1, lk1, lq2, lk2, sub_norm, lambda_init)
    return matmul_residual(o, w_o.astype(BF16), h)


def kernel(x, ssm_pre_norm, ssm_in_proj, ssm_conv_w, ssm_conv_b, ssm_dt_bias, ssm_A_log, ssm_D,
           ssm_out_norm, ssm_out_proj, dense_pre_norm, dense_w_gate, dense_w_up, dense_w_down,
           att_pre_norm, att_w_qkv, att_q_norm, att_k_norm, att_lambda_q1, att_lambda_k1,
           att_lambda_q2, att_lambda_k2, att_sub_norm, att_w_o, moe_pre_norm, moe_router,
           moe_w_gate, moe_w_up, moe_w_down):
    b, s, d = x.shape
    h = x.reshape(b * s, d)
    for i in range(DEPTH):
        j = i // N_MIXERS
        if i % N_MIXERS == 0:
            h = mamba_layer(h, b, s, ssm_pre_norm[j], ssm_in_proj[j], ssm_conv_w[j],
                            ssm_conv_b[j], ssm_dt_bias[j], ssm_A_log[j], ssm_D[j],
                            ssm_out_norm[j], ssm_out_proj[j])
            h = dense_layer(h, dense_pre_norm[j], dense_w_gate[j], dense_w_up[j],
                            dense_w_down[j])
        else:
            lambda_init = 0.8 - 0.6 * math.exp(-0.3 * i)
            h = attention_layer(h, b, s, att_pre_norm[j], att_w_qkv[j], att_q_norm[j],
                                att_k_norm[j], att_lambda_q1[j], att_lambda_k1[j],
                                att_lambda_q2[j], att_lambda_k2[j], att_sub_norm[j], att_w_o[j],
                                lambda_init)
            h = moe_block(h, moe_pre_norm[j], moe_router[j], moe_w_gate[j], moe_w_up[j],
                          moe_w_down[j])
    return h.reshape(b, s, d)
```

```python
import functools
import math

import jax
import jax.numpy as jnp
from jax import lax
from jax.experimental import pallas as pl
from jax.experimental.pallas import tpu as pltpu

F32 = jnp.float32
BF16 = jnp.bfloat16

EPS = 1e-5
DEPTH = 2
N_MIXERS = 2

SSM_GROUPS = 8
SSM_STATE = 128
CHUNK = 128
CONV_WIDTH = 4
SSM_HEAD_DIM = 64
CONV_TAIL = 16

ATT_HEAD_DIM = 64
ATT_V_DIM = 2 * ATT_HEAD_DIM
ATT_SUM_ROWS = 16

TOP_K = 2

LOG2E = math.log2(math.e)

LANES = 128
SUBLANES = 8
VMEM_LIMIT_BYTES = 56 * 1024 * 1024

TM_PROJ = 1024
TN_PROJ = 1024
TM_ROUTE = 512
RES_VMEM_BUDGET = 46 * 1024 * 1024
TF_FFN = 512
SSD_CHUNKS_PER_STEP = 8
TM_MOE = 512
TN_MOE_DOWN = 1024
TQ_ATT = 512
TK_ATT = 256
ATT_Q_TILES_PER_STEP = 4


def _tile(dim, pref, quantum=LANES):
    if dim <= pref:
        return dim
    t = (pref // quantum) * quantum
    while t >= quantum:
        if dim % t == 0:
            return t
        t -= quantum
    return dim


def _params(sem):
    return pltpu.CompilerParams(dimension_semantics=sem, vmem_limit_bytes=VMEM_LIMIT_BYTES)


def _rms_rows(x, gain):
    return x * lax.rsqrt(jnp.mean(x * x, axis=-1, keepdims=True) + EPS) * gain


def _silu(x):
    h = 0.5 * x
    return h + h * jnp.tanh(h)


def _split3(v):
    hi = v.astype(BF16).astype(F32)
    r = v - hi
    mid = r.astype(BF16).astype(F32)
    return [hi, mid, r - mid]


def _norm_mm_kernel(x_ref, g_ref, w_ref, o_ref, xn_ref):
    @pl.when(pl.program_id(1) == 0)
    def _():
        xn_ref[...] = _rms_rows(x_ref[...], g_ref[...]).astype(BF16)

    o_ref[...] = jnp.dot(xn_ref[...], w_ref[...], preferred_element_type=F32).astype(o_ref.dtype)


def norm_matmul(x, gain, w, out_dtype):
    t, d = x.shape
    n = w.shape[1]
    tm, tn = _tile(t, TM_PROJ, SUBLANES), _tile(n, TN_PROJ)
    return pl.pallas_call(
        _norm_mm_kernel,
        out_shape=jax.ShapeDtypeStruct((t, n), out_dtype),
        grid=(t // tm, n // tn),
        in_specs=[
            pl.BlockSpec((tm, d), lambda i, j: (i, 0)),
            pl.BlockSpec((1, d), lambda i, j: (0, 0)),
            pl.BlockSpec((d, tn), lambda i, j: (0, j)),
        ],
        out_specs=pl.BlockSpec((tm, tn), lambda i, j: (i, j)),
        scratch_shapes=[pltpu.VMEM((tm, d), BF16)],
        compiler_params=_params(("arbitrary", "arbitrary")),
        name="norm_matmul",
    )(x, gain.reshape(1, d), w)


def _norm_gateup_kernel(x_ref, g_ref, wg_ref, wu_ref, o_ref, xn_ref):
    @pl.when(pl.program_id(1) == 0)
    def _():
        xn_ref[...] = _rms_rows(x_ref[...], g_ref[...]).astype(BF16)

    xn = xn_ref[...]
    a = jnp.dot(xn, wg_ref[...], preferred_element_type=F32)
    b = jnp.dot(xn, wu_ref[...], preferred_element_type=F32)
    o_ref[...] = (_silu(a) * b).astype(o_ref.dtype)


def norm_gateup(x, gain, wg, wu):
    t, d = x.shape
    f = wg.shape[1]
    tm, tf = _tile(t, TM_PROJ, SUBLANES), _tile(f, TF_FFN)
    return pl.pallas_call(
        _norm_gateup_kernel,
        out_shape=jax.ShapeDtypeStruct((t, f), BF16),
        grid=(t // tm, f // tf),
        in_specs=[
            pl.BlockSpec((tm, d), lambda i, j: (i, 0)),
            pl.BlockSpec((1, d), lambda i, j: (0, 0)),
            pl.BlockSpec((d, tf), lambda i, j: (0, j)),
            pl.BlockSpec((d, tf), lambda i, j: (0, j)),
        ],
        out_specs=pl.BlockSpec((tm, tf), lambda i, j: (i, j)),
        scratch_shapes=[pltpu.VMEM((tm, d), BF16)],
        compiler_params=_params(("arbitrary", "arbitrary")),
        name="norm_gateup",
    )(x, gain.reshape(1, d), wg, wu)


def _mm_res_kernel(a_ref, w_ref, r_ref, o_ref):
    o_ref[...] = r_ref[...] + jnp.dot(a_ref[...], w_ref[...], preferred_element_type=F32)


def matmul_residual(a, w, res):
    t, k = a.shape
    n = w.shape[1]
    tm = t
    for cand in (1024, 512, 256, 128):
        need = k * n * 2 + 2 * cand * k * 2 + 4 * cand * n * 4 + cand * n * 4
        if t % cand == 0 and need <= RES_VMEM_BUDGET:
            tm = cand
            break
    return pl.pallas_call(
        _mm_res_kernel,
        out_shape=jax.ShapeDtypeStruct((t, n), F32),
        grid=(t // tm,),
        in_specs=[
            pl.BlockSpec((tm, k), lambda i: (i, 0)),
            pl.BlockSpec((k, n), lambda i: (0, 0), pipeline_mode=pl.Buffered(1)),
            pl.BlockSpec((tm, n), lambda i: (i, 0)),
        ],
        out_specs=pl.BlockSpec((tm, n), lambda i: (i, 0)),
        compiler_params=_params(("arbitrary",)),
        name="matmul_residual",
    )(a, w, res)


def _ssd_kernel(x_ref, bm_ref, cm_ref, z_ref, dtt_ref, cwx_ref, cbx_ref, cwb_ref, cbb_ref,
                cwc_ref, cbc_ref, dtb_ref, alog_ref, dx_ref, gn_ref, e_ref, sh_ref,
                o_ref, pad_ref, h_ref, *, hpg, gw, nc):
    L = CHUNK
    n = SSM_STATE
    hd = gw // hpg
    rows = nc * L
    c = pl.program_id(2)
    tail = CONV_TAIL

    @pl.when(c == 0)
    def _():
        pad_ref[0:tail, :] = jnp.zeros((tail, gw + 2 * n), BF16)
        h_ref[...] = jnp.zeros_like(h_ref)

    @pl.when(c > 0)
    def _():
        pad_ref[0:tail, :] = pad_ref[rows:rows + tail, :]

    pad_ref[tail:tail + rows, 0:gw] = x_ref[0]
    pad_ref[tail:tail + rows, gw:gw + n] = bm_ref[0]
    pad_ref[tail:tail + rows, gw + n:gw + 2 * n] = cm_ref[0]

    def conv(shifted, t0, col0, width, w_ref, b_ref):
        acc = b_ref[...] + w_ref[CONV_WIDTH - 1:CONV_WIDTH, :] * pad_ref[
            t0 + tail:t0 + tail + L, col0:col0 + width].astype(F32)
        for k in range(CONV_WIDTH - 1):
            acc = acc + w_ref[k:k + 1, :] * shifted[k * L:(k + 1) * L, col0:col0 + width]
        return _silu(acc)

    raw = dtt_ref[0] + dtb_ref[...]
    dt_all = jnp.maximum(raw, 0.0) + jnp.log1p(jnp.exp(-jnp.abs(raw)))
    a_all = dt_all * (-jnp.exp(alog_ref[...]))
    row = lax.broadcasted_iota(jnp.int32, (L, L), 0)
    col = lax.broadcasted_iota(jnp.int32, (L, L), 1)
    upper = (row <= col).astype(F32)
    causal = row >= col
    lane = lax.broadcasted_iota(jnp.int32, (1, 2 * hd), 1)
    lo = lane < hd

    for ci in range(nc):
        t0 = ci * L
        shifted = jnp.dot(sh_ref[...], pad_ref[t0:t0 + tail + L, :],
                          preferred_element_type=F32)
        xs = conv(shifted, t0, 0, gw, cwx_ref, cbx_ref)
        bc = conv(shifted, t0, gw, n, cwb_ref, cbb_ref)
        cc = conv(shifted, t0, gw + n, n, cwc_ref, cbc_ref)

        dt = dt_all[:, t0:t0 + L]
        acs = jnp.dot(a_all[:, t0:t0 + L], upper, preferred_element_type=F32,
                      precision=lax.Precision.HIGHEST)
        a_last = acs[:, L - 1:L]
        dte = jnp.exp(a_last - acs)
        eacs = jnp.exp(acs)

        parts = _split3(acs) + _split3(dt) + _split3(dt * dte) + _split3(eacs)
        stacked = jnp.concatenate(parts, axis=0).T.astype(BF16)
        spread = jnp.dot(stacked, e_ref[...], preferred_element_type=F32)
        acs_col = spread[:, :hpg * L]
        dt_x = spread[:, hpg * L:hpg * L + gw]
        w_x = spread[:, hpg * L + gw:hpg * L + 2 * gw]
        eacs_x = spread[:, hpg * L + 2 * gw:]

        xdt = xs * dt_x
        bct = bc.T.astype(BF16)
        ccb = cc.astype(BF16)
        cb = jnp.dot(ccb, bct, preferred_element_type=F32)

        y_parts = []
        for q in range(hpg // 2):
            ms = []
            for j in (2 * q, 2 * q + 1):
                seg = acs_col[:, j * L:(j + 1) * L] - acs[j:j + 1, :]
                dec = jnp.exp(jnp.where(causal, seg, -jnp.inf))
                ms.append((cb * dec).astype(BF16))
            mcat = jnp.concatenate(ms, axis=1)
            xq = xdt[:, q * 2 * hd:(q + 1) * 2 * hd]
            rhs = jnp.concatenate([jnp.where(lo, xq, 0.0), jnp.where(lo, 0.0, xq)], axis=0)
            y_parts.append(jnp.dot(mcat, rhs.astype(BF16), preferred_element_type=F32))
        y_diag = jnp.concatenate(y_parts, axis=1)

        h_in = h_ref[...]
        y_off = jnp.dot(ccb, h_in.astype(BF16), preferred_element_type=F32) * eacs_x
        states = jnp.dot(bct, (xs * w_x).astype(BF16), preferred_element_type=F32)
        h_ref[...] = h_in * eacs_x[L - 1:L, :] + states

        y = y_diag + y_off + dx_ref[...] * xs
        gated = y * _silu(z_ref[0, t0:t0 + L, :].astype(F32))
        o_ref[0, t0:t0 + L, :] = _rms_rows(gated, gn_ref[...]).astype(o_ref.dtype)


def ssd_core(z, xbc, dtt, conv_w, conv_b, dt_bias, a_log, d_skip, out_norm):
    b, s, d_inner = z.shape
    g = SSM_GROUPS
    n = SSM_STATE
    heads = dt_bias.shape[0]
    hpg = heads // g
    gw = d_inner // g
    hd = gw // hpg
    L = CHUNK
    nb = d_inner // n
    assert gw % LANES == 0 and hpg % 2 == 0 and 2 * hd == LANES and s % L == 0

    eye = jnp.eye(hpg, dtype=F32)
    widths = [hpg * L, gw, gw, gw]
    mats = [jnp.repeat(eye, L, axis=1)] + [jnp.repeat(eye, hd, axis=1)] * 3
    spread_rows = []
    for qty in range(4):
        blk = jnp.concatenate([mats[qty] if c == qty else jnp.zeros((hpg, widths[c]), F32)
                               for c in range(4)], axis=1)
        spread_rows += [blk] * 3
    spread_mat = jnp.concatenate(spread_rows, axis=0).astype(BF16)
    dx = jnp.repeat(d_skip.astype(F32), hd).reshape(1, d_inner)

    nc = SSD_CHUNKS_PER_STEP if (s // L) % SSD_CHUNKS_PER_STEP == 0 else 1
    rows = nc * L
    out_row = jnp.arange((CONV_WIDTH - 1) * L, dtype=jnp.int32)
    src_row = out_row % L + CONV_TAIL - (CONV_WIDTH - 1) + out_row // L
    shift_mat = (src_row[:, None] == jnp.arange(CONV_TAIL + L, dtype=jnp.int32)[None, :]).astype(BF16)
    kern = functools.partial(_ssd_kernel, hpg=hpg, gw=gw, nc=nc)
    return pl.pallas_call(
        kern,
        out_shape=jax.ShapeDtypeStruct((b, s, d_inner), BF16),
        grid=(b, g, s // rows),
        in_specs=[
            pl.BlockSpec((1, rows, gw), lambda i, j, c: (i, c, j)),
            pl.BlockSpec((1, rows, n), lambda i, j, c: (i, c, nb + j)),
            pl.BlockSpec((1, rows, n), lambda i, j, c: (i, c, nb + g + j)),
            pl.BlockSpec((1, rows, gw), lambda i, j, c: (i, c, j)),
            pl.BlockSpec((1, hpg, rows), lambda i, j, c: (i, j, c)),
            pl.BlockSpec((CONV_WIDTH, gw), lambda i, j, c: (0, j)),
            pl.BlockSpec((1, gw), lambda i, j, c: (0, j)),
            pl.BlockSpec((CONV_WIDTH, n), lambda i, j, c: (0, nb + j)),
            pl.BlockSpec((1, n), lambda i, j, c: (0, nb + j)),
            pl.BlockSpec((CONV_WIDTH, n), lambda i, j, c: (0, nb + g + j)),
            pl.BlockSpec((1, n), lambda i, j, c: (0, nb + g + j)),
            pl.BlockSpec((hpg, 1), lambda i, j, c: (j, 0)),
            pl.BlockSpec((hpg, 1), lambda i, j, c: (j, 0)),
            pl.BlockSpec((1, gw), lambda i, j, c: (0, j)),
            pl.BlockSpec((1, gw), lambda i, j, c: (0, j)),
            pl.BlockSpec((12 * hpg, hpg * L + 3 * gw), lambda i, j, c: (0, 0)),
            pl.BlockSpec(((CONV_WIDTH - 1) * L, CONV_TAIL + L), lambda i, j, c: (0, 0)),
        ],
        out_specs=pl.BlockSpec((1, rows, gw), lambda i, j, c: (i, c, j)),
        scratch_shapes=[pltpu.VMEM((rows + CONV_TAIL, gw + 2 * n), BF16), pltpu.VMEM((n, gw), F32)],
        compiler_params=_params(("arbitrary", "arbitrary", "arbitrary")),
        name="ssd_core",
    )(xbc, xbc, xbc, z, dtt, conv_w, conv_b.reshape(1, -1), conv_w, conv_b.reshape(1, -1),
      conv_w, conv_b.reshape(1, -1), dt_bias.reshape(heads, 1), a_log.reshape(heads, 1),
      dx, out_norm.reshape(1, d_inner), spread_mat, shift_mat)


def _attn_kernel(q_ref, k_ref, v_ref, qg_ref, kg_ref, lq1_ref, lk1_ref, lq2_ref, lk2_ref,
                 sg_ref, o_ref, kn_ref, vt_ref, m_ref, acc_ref, s_ref, p_ref, a_ref,
                 *, tq, t, nsub, lambda_init):
    s_len = k_ref.shape[1]
    n_tiles = s_len // t
    hd = ATT_HEAD_DIM
    step = pl.program_id(2)
    lane = lax.broadcasted_iota(jnp.int32, (1, 2 * hd), 1)
    lo = lane < hd

    def comp_norm(x, gain):
        x2 = x * x
        s_lo = jnp.sum(jnp.where(lo, x2, 0.0), axis=-1, keepdims=True)
        s_hi = jnp.sum(jnp.where(lo, 0.0, x2), axis=-1, keepdims=True)
        inv = jnp.where(lo, lax.rsqrt(s_lo * (1.0 / hd) + EPS), lax.rsqrt(s_hi * (1.0 / hd) + EPS))
        return x * inv * gain

    @pl.when(step == 0)
    def _():
        def body(i, carry):
            r0 = pl.multiple_of(i * t, t)
            kn = comp_norm(k_ref[0, pl.ds(r0, t), :].astype(F32), kg_ref[...])
            kn_ref[pl.ds(r0, t), :] = kn.astype(BF16)
            vt_ref[i, 0:ATT_V_DIM, :] = v_ref[0, pl.ds(r0, t), :].astype(F32).T.astype(BF16)
            vt_ref[i, ATT_V_DIM:, :] = jnp.ones((ATT_SUM_ROWS, t), BF16)
            return carry

        lax.fori_loop(0, s_len // t, body, 0)

    lam = (jnp.exp(jnp.sum(lq1_ref[...] * lk1_ref[...], axis=-1, keepdims=True))
           - jnp.exp(jnp.sum(lq2_ref[...] * lk2_ref[...], axis=-1, keepdims=True)) + lambda_init)

    for sub in range(nsub):
        _attn_query_tile(step * nsub + sub, q_ref, sub * tq, o_ref, qg_ref, sg_ref, kn_ref, vt_ref,
                         m_ref.at[sub], acc_ref.at[sub], s_ref.at[sub], p_ref.at[sub],
                         a_ref.at[sub], comp_norm, lo, lam, tq=tq, t=t, n_tiles=n_tiles,
                         lambda_init=lambda_init)


def _attn_query_tile(qi, q_ref, q0, o_ref, qg_ref, sg_ref, kn_ref, vt_ref, m_ref, acc_ref, s_ref,
                     p_ref, a_ref, comp_norm, lo, lam, *, tq, t, n_tiles, lambda_init):
    hd = ATT_HEAD_DIM
    qn = comp_norm(q_ref[0, q0:q0 + tq, :].astype(F32), qg_ref[...]) * (hd ** -0.5 * LOG2E)
    qc = (jnp.where(lo, qn, 0.0).astype(BF16), jnp.where(lo, 0.0, qn).astype(BF16))

    m_ref[...] = jnp.full(m_ref.shape, -jnp.inf, F32)
    acc_ref[...] = jnp.zeros(acc_ref.shape, F32)

    def scores(tile, slot):
        r0 = pl.multiple_of(jnp.minimum(tile, n_tiles - 1) * t, t)
        kk = kn_ref[pl.ds(r0, t), :]
        for c in range(2):
            s_ref[slot, c] = lax.dot_general(kk, qc[c], (((1,), (1,)), ((), ())),
                                             preferred_element_type=F32)

    def softmax(tile, slot, masked):
        for c in range(2):
            sc = s_ref[slot, c]
            if masked:
                key = tile * t + lax.broadcasted_iota(jnp.int32, (t, tq), 0)
                qry = qi * tq + lax.broadcasted_iota(jnp.int32, (t, tq), 1)
                sc = jnp.where(key <= qry, sc, -jnp.inf)
            m_prev = m_ref[c]
            m_new = jnp.maximum(m_prev, jnp.max(sc, axis=0, keepdims=True))
            alpha = jnp.exp2(m_prev - m_new)
            p = jnp.exp2(sc - m_new)
            m_ref[c] = m_new
            a_ref[slot, c] = alpha
            p_ref[slot, c] = p.astype(BF16)

    def update(tile0, tile1):
        vt0 = vt_ref[jnp.minimum(tile0, n_tiles - 1)]
        vt1 = vt_ref[jnp.minimum(tile1, n_tiles - 1)]
        for c in range(2):
            a1 = a_ref[1, c]
            pv0 = jnp.dot(vt0, p_ref[0, c], preferred_element_type=F32)
            pv1 = jnp.dot(vt1, p_ref[1, c], preferred_element_type=F32)
            acc_ref[c] = (a_ref[0, c] * a1) * acc_ref[c] + a1 * pv0 + pv1

    d0 = 2 * ((qi * tq) // (2 * t))
    n_pairs = d0 // 2 + 1
    scores(d0, 0)
    scores(d0 + 1, 1)
    softmax(d0, 0, True)
    scores(0, 0)
    softmax(d0 + 1, 1, True)

    def pair(u, carry):
        x = 2 * u - 2
        first = u == 1
        update(jnp.where(first, d0, x - 2), jnp.where(first, d0 + 1, x - 1))
        scores(x + 1, 1)
        softmax(x, 0, False)
        scores(x + 2, 0)
        softmax(x + 1, 1, False)
        return carry

    lax.fori_loop(1, n_pairs, pair, 0)
    only = n_pairs == 1
    update(jnp.where(only, d0, d0 - 2), jnp.where(only, d0 + 1, d0 - 1))

    e = ATT_V_DIM
    o = (acc_ref[0, 0:e, :] / acc_ref[0, e:e + 1, :]
         - lam * (acc_ref[1, 0:e, :] / acc_ref[1, e:e + 1, :]))
    o = o * lax.rsqrt(jnp.mean(o * o, axis=0, keepdims=True) + EPS) * sg_ref[...]
    o_ref[0, q0:q0 + tq, :] = (o * (1.0 - lambda_init)).T.astype(o_ref.dtype)


def diff_attention_core(qkv, q_norm, k_norm, lq1, lk1, lq2, lk2, sub_norm, lambda_init):
    b, s, w3 = qkv.shape
    width = w3 // 3
    heads = width // ATT_V_DIM
    hd = ATT_HEAD_DIM
    tq = _tile(s, TQ_ATT, LANES)
    t = _tile(s, TK_ATT, LANES)
    assert tq % t == 0 and (2 * t) % tq == 0
    vec = lambda a: a.astype(F32).reshape(1, -1)
    qg = jnp.tile(q_norm.astype(F32), 2).reshape(1, 2 * hd)
    kg = jnp.tile(k_norm.astype(F32), 2).reshape(1, 2 * hd)
    small = lambda width_: pl.BlockSpec((1, width_), lambda i, h, j: (0, 0))
    nsub = ATT_Q_TILES_PER_STEP if (s // tq) % ATT_Q_TILES_PER_STEP == 0 else 1
    rows = nsub * tq
    kern = functools.partial(_attn_kernel, tq=tq, t=t, nsub=nsub, lambda_init=lambda_init)
    return pl.pallas_call(
        kern,
        out_shape=jax.ShapeDtypeStruct((b, s, width), BF16),
        grid=(b, heads, s // rows),
        in_specs=[
            pl.BlockSpec((1, rows, 2 * hd), lambda i, h, j: (i, j, h)),
            pl.BlockSpec((1, s, 2 * hd), lambda i, h, j: (i, 0, heads + h)),
            pl.BlockSpec((1, s, ATT_V_DIM), lambda i, h, j: (i, 0, 2 * heads + h)),
            small(2 * hd), small(2 * hd), small(hd), small(hd), small(hd), small(hd),
            pl.BlockSpec((ATT_V_DIM, 1), lambda i, h, j: (0, 0)),
        ],
        out_specs=pl.BlockSpec((1, rows, ATT_V_DIM), lambda i, h, j: (i, j, h)),
        scratch_shapes=[
            pltpu.VMEM((s, 2 * hd), BF16),
            pltpu.VMEM((s // t, ATT_V_DIM + ATT_SUM_ROWS, t), BF16),
            pltpu.VMEM((nsub, 2, 1, tq), F32),
            pltpu.VMEM((nsub, 2, ATT_V_DIM + ATT_SUM_ROWS, tq), F32),
            pltpu.VMEM((nsub, 2, 2, t, tq), F32), pltpu.VMEM((nsub, 2, 2, t, tq), BF16),
            pltpu.VMEM((nsub, 2, 2, 1, tq), F32),
        ],
        compiler_params=_params(("arbitrary", "arbitrary", "arbitrary")),
        name="diff_attention",
    )(qkv, qkv, qkv, qg, kg, vec(lq1), vec(lk1), vec(lq2), vec(lk2),
      sub_norm.astype(F32).reshape(ATT_V_DIM, 1))


def _route_kernel(x_ref, g_ref, r_ref, xn_ref, info_ref, cnt_ref, run_ref, *, n_experts):
    @pl.when(pl.program_id(0) == 0)
    def _():
        run_ref[...] = jnp.zeros_like(run_ref)

    xn = _rms_rows(x_ref[...], g_ref[...])
    xn_ref[...] = xn.astype(BF16)
    xh = xn.astype(BF16)
    xl = (xn - xh.astype(F32)).astype(BF16)
    hi = jnp.dot(xh, r_ref[...], preferred_element_type=F32)
    lo_ = jnp.dot(xl, r_ref[...], preferred_element_type=F32)
    logits = (hi[:, :LANES] + hi[:, LANES:]) + (lo_[:, :LANES] + lo_[:, LANES:])
    lane = lax.broadcasted_iota(jnp.int32, logits.shape, 1)
    logits = jnp.where(lane < n_experts, logits, -jnp.inf)
    m1 = jnp.max(logits, axis=-1, keepdims=True)
    i1 = jnp.min(jnp.where(logits == m1, lane, LANES), axis=-1, keepdims=True)
    rest = jnp.where(lane == i1, -jnp.inf, logits)
    m2 = jnp.max(rest, axis=-1, keepdims=True)
    i2 = jnp.min(jnp.where(rest == m2, lane, LANES), axis=-1, keepdims=True)
    e2 = jnp.exp(m2 - m1)
    g1 = 1.0 / (1.0 + e2)
    g2 = e2 * g1
    tm = logits.shape[0]
    oh1 = (lane == i1).astype(F32)
    oh2 = (lane == i2).astype(F32)
    both = oh1 + oh2
    row = lax.broadcasted_iota(jnp.int32, (tm, tm), 0)
    col = lax.broadcasted_iota(jnp.int32, (tm, tm), 1)
    before = jnp.dot((col < row).astype(BF16), both.astype(BF16),
                     preferred_element_type=F32) + run_ref[0:1, :]
    r1 = jnp.sum(before * oh1, axis=-1, keepdims=True)
    r2 = jnp.sum(before * oh2, axis=-1, keepdims=True)
    run_ref[...] = run_ref[...] + jnp.sum(both, axis=0, keepdims=True)
    cnt_ref[...] = run_ref[...]
    info = jnp.zeros_like(logits)
    for k, val in enumerate((i1.astype(F32), i2.astype(F32), g1, g2, r1, r2)):
        info = jnp.where(lane == k, val, info)
    info_ref[...] = info


def moe_route(h, gain, router):
    t, d = h.shape
    e = router.shape[1]
    tm = _tile(t, TM_ROUTE, SUBLANES)
    rpad = jnp.zeros((d, LANES), F32).at[:, :e].set(router.astype(F32))
    r_hi = rpad.astype(BF16)
    r_parts = jnp.concatenate([r_hi, (rpad - r_hi.astype(F32)).astype(BF16)], axis=1)
    return pl.pallas_call(
        functools.partial(_route_kernel, n_experts=e),
        out_shape=(jax.ShapeDtypeStruct((t, d), BF16), jax.ShapeDtypeStruct((t, LANES), F32),
                   jax.ShapeDtypeStruct((SUBLANES, LANES), F32)),
        grid=(t // tm,),
        in_specs=[
            pl.BlockSpec((tm, d), lambda i: (i, 0)),
            pl.BlockSpec((1, d), lambda i: (0, 0)),
            pl.BlockSpec((d, 2 * LANES), lambda i: (0, 0)),
        ],
        out_specs=(pl.BlockSpec((tm, d), lambda i: (i, 0)),
                   pl.BlockSpec((tm, LANES), lambda i: (i, 0)),
                   pl.BlockSpec((SUBLANES, LANES), lambda i: (0, 0))),
        scratch_shapes=[pltpu.VMEM((SUBLANES, LANES), F32)],
        compiler_params=_params(("arbitrary",)),
        name="moe_route",
    )(h, gain.reshape(1, d), r_parts)


def _new_weights(te_ref, i):
    prev = te_ref[jnp.maximum(i - 1, 0)]
    return jnp.logical_or(i == 0, te_ref[i] != prev)


def _moe_gateup_kernel(te_ref, nu_ref, x_ref, wg_ref, wu_ref, o_ref, wgb_ref, wub_ref):
    i = pl.program_id(1)

    @pl.when(_new_weights(te_ref, i))
    def _():
        wgb_ref[...] = wg_ref[0].astype(BF16)
        wub_ref[...] = wu_ref[0].astype(BF16)

    @pl.when(i < nu_ref[0])
    def _():
        x = x_ref[...]
        a = jnp.dot(x, wgb_ref[...], preferred_element_type=F32)
        b = jnp.dot(x, wub_ref[...], preferred_element_type=F32)
        o_ref[...] = (_silu(a) * b).astype(o_ref.dtype)

    @pl.when(i >= nu_ref[0])
    def _():
        o_ref[...] = jnp.zeros_like(o_ref)


def _moe_down_kernel(te_ref, nu_ref, h_ref, wd_ref, o_ref, wdb_ref):
    i = pl.program_id(1)

    @pl.when(_new_weights(te_ref, i))
    def _():
        wdb_ref[...] = wd_ref[0].astype(BF16)

    @pl.when(i < nu_ref[0])
    def _():
        o_ref[...] = jnp.dot(h_ref[...], wdb_ref[...],
                             preferred_element_type=F32).astype(o_ref.dtype)

    @pl.when(i >= nu_ref[0])
    def _():
        o_ref[...] = jnp.zeros_like(o_ref)


def moe_experts(xs, tile_expert, n_used, w_gate, w_up, w_down, tm):
    slots, d = xs.shape
    e, _, f = w_gate.shape
    n_tiles = slots // tm
    tf = _tile(f, TF_FFN)
    hmid = pl.pallas_call(
        _moe_gateup_kernel,
        out_shape=jax.ShapeDtypeStruct((slots, f), BF16),
        grid_spec=pltpu.PrefetchScalarGridSpec(
            num_scalar_prefetch=2,
            grid=(f // tf, n_tiles),
            in_specs=[
                pl.BlockSpec((tm, d), lambda j, i, te, nu: (i, 0)),
                pl.BlockSpec((1, d, tf), lambda j, i, te, nu: (te[i], 0, j)),
                pl.BlockSpec((1, d, tf), lambda j, i, te, nu: (te[i], 0, j)),
            ],
            out_specs=pl.BlockSpec((tm, tf), lambda j, i, te, nu: (i, j)),
            scratch_shapes=[pltpu.VMEM((d, tf), BF16), pltpu.VMEM((d, tf), BF16)],
        ),
        compiler_params=_params(("arbitrary", "arbitrary")),
        name="moe_gateup",
    )(tile_expert, n_used, xs, w_gate, w_up)

    tn = _tile(d, TN_MOE_DOWN)
    return pl.pallas_call(
        _moe_down_kernel,
        out_shape=jax.ShapeDtypeStruct((slots, d), BF16),
        grid_spec=pltpu.PrefetchScalarGridSpec(
            num_scalar_prefetch=2,
            grid=(d // tn, n_tiles),
            in_specs=[
                pl.BlockSpec((tm, f), lambda j, i, te, nu: (i, 0)),
                pl.BlockSpec((1, f, tn), lambda j, i, te, nu: (te[i], 0, j),
                             pipeline_mode=pl.Buffered(1)),
            ],
            out_specs=pl.BlockSpec((tm, tn), lambda j, i, te, nu: (i, j)),
            scratch_shapes=[pltpu.VMEM((f, tn), BF16)],
        ),
        compiler_params=_params(("arbitrary", "arbitrary")),
        name="moe_down",
    )(tile_expert, n_used, hmid, w_down)


def moe_block(h, gain, router, w_gate, w_up, w_down):
    t, d = h.shape
    e = router.shape[1]
    tm = min(TM_MOE, t)
    xn, info, cnt = moe_route(h, gain, router)
    idx = info[:, :TOP_K].astype(jnp.int32)
    gates = info[:, TOP_K:2 * TOP_K]
    rank = info[:, 2 * TOP_K:3 * TOP_K].astype(jnp.int32)
    counts = cnt[0, :e].astype(jnp.int32)

    tiles_per_e = (counts + tm - 1) // tm
    tile_end = jnp.cumsum(tiles_per_e)
    starts = (tile_end - tiles_per_e) * tm
    slot = (jnp.take(starts, idx.reshape(-1)) + rank.reshape(-1))
    n_tiles = (TOP_K * t) // tm + e
    n_slots = n_tiles * tm
    token_of_slot = jnp.zeros((n_slots,), jnp.int32).at[slot].set(
        jnp.arange(TOP_K * t, dtype=jnp.int32) // TOP_K)
    tile_ids = jnp.arange(n_tiles, dtype=jnp.int32)
    tile_expert = jnp.minimum(jnp.sum((tile_ids[:, None] >= tile_end[None, :]).astype(jnp.int32), axis=1),
                              e - 1)
    n_used = tile_end[-1:].astype(jnp.int32)

    xs = xn.at[token_of_slot].get(mode="promise_in_bounds")
    ys = moe_experts(xs, tile_expert, n_used, w_gate, w_up, w_down, tm)
    slot2 = slot.reshape(t, TOP_K)
    out = h
    for k in range(TOP_K):
        out = out + gates[:, k:k + 1] * ys.at[slot2[:, k]].get(mode="promise_in_bounds").astype(F32)
    return out


def mamba_layer(h, b, s, pre_norm, in_proj, conv_w, conv_b, dt_bias, a_log, d_skip, out_norm,
                out_proj):
    d_inner = out_proj.shape[0]
    d_xbc = conv_w.shape[1]
    heads = dt_bias.shape[0]
    z = norm_matmul(h, pre_norm, in_proj[:, :d_inner].astype(BF16), BF16)
    xbc = norm_matmul(h, pre_norm, in_proj[:, d_inner:d_inner + d_xbc].astype(BF16), BF16)
    w_dt = jnp.zeros((h.shape[1], LANES), BF16).at[:, :heads].set(
        in_proj[:, d_inner + d_xbc:].astype(BF16))
    dt = norm_matmul(h, pre_norm, w_dt, F32)[:, :heads]
    dtt = jnp.swapaxes(dt.reshape(b, s, heads), 1, 2)
    y = ssd_core(z.reshape(b, s, d_inner), xbc.reshape(b, s, d_xbc), dtt, conv_w.astype(F32),
                 conv_b.astype(F32), dt_bias.astype(F32), a_log.astype(F32), d_skip, out_norm)
    return matmul_residual(y.reshape(b * s, d_inner), out_proj.astype(BF16), h)


def dense_layer(h, pre_norm, w_gate, w_up, w_down):
    mid = norm_gateup(h, pre_norm, w_gate.astype(BF16), w_up.astype(BF16))
    return matmul_residual(mid, w_down.astype(BF16), h)


def attention_layer(h, b, s, pre_norm, w_qkv, q_norm, k_norm, lq1, lk1, lq2, lk2, sub_norm, w_o,
                    lambda_init):
    qkv = norm_matmul(h, pre_norm, w_qkv.astype(BF16), BF16)
    o = diff_attention_core(qkv.reshape(b, s, -1), q_norm, k_norm, lq1, lk1, lq2, lk2, sub_norm,
                            lambda_init)
    return matmul_residual(o.reshape(b * s, -1), w_o.astype(BF16), h)


def kernel(x, ssm_pre_norm, ssm_in_proj, ssm_conv_w, ssm_conv_b, ssm_dt_bias, ssm_A_log, ssm_D,
           ssm_out_norm, ssm_out_proj, dense_pre_norm, dense_w_gate, dense_w_up, dense_w_down,
           att_pre_norm, att_w_qkv, att_q_norm, att_k_norm, att_lambda_q1, att_lambda_k1,
           att_lambda_q2, att_lambda_k2, att_sub_norm, att_w_o, moe_pre_norm, moe_router,
           moe_w_gate, moe_w_up, moe_w_down):
    b, s, d = x.shape
    h = x.reshape(b * s, d)
    for i in range(DEPTH):
        j = i // N_MIXERS
        if i % N_MIXERS == 0:
            h = mamba_layer(h, b, s, ssm_pre_norm[j], ssm_in_proj[j], ssm_conv_w[j],
                            ssm_conv_b[j], ssm_dt_bias[j], ssm_A_log[j], ssm_D[j],
                            ssm_out_norm[j], ssm_out_proj[j])
            h = dense_layer(h, dense_pre_norm[j], dense_w_gate[j], dense_w_up[j],
                            dense_w_down[j])
        else:
            lambda_init = 0.8 - 0.6 * math.exp(-0.3 * i)
            h = attention_layer(h, b, s, att_pre_norm[j], att_w_qkv[j], att_q_norm[j],
                                att_k_norm[j], att_lambda_q1[j], att_lambda_k1[j],
                                att_lambda_q2[j], att_lambda_k2[j], att_sub_norm[j], att_w_o[j],
                                lambda_init)
            h = moe_block(h, moe_pre_norm[j], moe_router[j], moe_w_gate[j], moe_w_up[j],
                          moe_w_down[j])
    return h.reshape(b, s, d)
```

```python
import functools
import math

import jax
import jax.numpy as jnp
from jax import lax
from jax.experimental import pallas as pl
from jax.experimental.pallas import tpu as pltpu

F32 = jnp.float32
BF16 = jnp.bfloat16

EPS = 1e-5
DEPTH = 2
N_MIXERS = 2

SSM_GROUPS = 8
SSM_STATE = 128
CHUNK = 128
CONV_WIDTH = 4
SSM_HEAD_DIM = 64
CONV_TAIL = 16

ATT_HEAD_DIM = 64
ATT_V_DIM = 2 * ATT_HEAD_DIM
ATT_SUM_ROWS = 16

TOP_K = 2

LOG2E = math.log2(math.e)

LANES = 128
SUBLANES = 8
VMEM_LIMIT_BYTES = 56 * 1024 * 1024

TM_PROJ = 1024
TN_PROJ = 1024
TM_ROUTE = 512
RES_VMEM_BUDGET = 46 * 1024 * 1024
TF_FFN = 512
SSD_CHUNKS_PER_STEP = 8
TM_MOE = 512
TN_MOE_DOWN = 1024
TQ_ATT = 512
TK_ATT = 256
ATT_Q_TILES_PER_STEP = 8


def _tile(dim, pref, quantum=LANES):
    if dim <= pref:
        return dim
    t = (pref // quantum) * quantum
    while t >= quantum:
        if dim % t == 0:
            return t
        t -= quantum
    return dim


def _params(sem):
    return pltpu.CompilerParams(dimension_semantics=sem, vmem_limit_bytes=VMEM_LIMIT_BYTES)


def _rms_rows(x, gain):
    return x * lax.rsqrt(jnp.mean(x * x, axis=-1, keepdims=True) + EPS) * gain


def _silu(x):
    h = 0.5 * x
    return h + h * jnp.tanh(h)


def _split3(v):
    hi = v.astype(BF16).astype(F32)
    r = v - hi
    mid = r.astype(BF16).astype(F32)
    return [hi, mid, r - mid]


def _norm_mm_kernel(x_ref, g_ref, w_ref, o_ref, xn_ref):
    @pl.when(pl.program_id(1) == 0)
    def _():
        xn_ref[...] = _rms_rows(x_ref[...], g_ref[...]).astype(BF16)

    o_ref[...] = jnp.dot(xn_ref[...], w_ref[...], preferred_element_type=F32).astype(o_ref.dtype)


def norm_matmul(x, gain, w, out_dtype):
    t, d = x.shape
    n = w.shape[1]
    tm, tn = _tile(t, TM_PROJ, SUBLANES), _tile(n, TN_PROJ)
    return pl.pallas_call(
        _norm_mm_kernel,
        out_shape=jax.ShapeDtypeStruct((t, n), out_dtype),
        grid=(t // tm, n // tn),
        in_specs=[
            pl.BlockSpec((tm, d), lambda i, j: (i, 0)),
            pl.BlockSpec((1, d), lambda i, j: (0, 0)),
            pl.BlockSpec((d, tn), lambda i, j: (0, j)),
        ],
        out_specs=pl.BlockSpec((tm, tn), lambda i, j: (i, j)),
        scratch_shapes=[pltpu.VMEM((tm, d), BF16)],
        compiler_params=_params(("arbitrary", "arbitrary")),
        name="norm_matmul",
    )(x, gain.reshape(1, d), w)


def _norm_gateup_kernel(x_ref, g_ref, wg_ref, wu_ref, o_ref, xn_ref):
    @pl.when(pl.program_id(1) == 0)
    def _():
        xn_ref[...] = _rms_rows(x_ref[...], g_ref[...]).astype(BF16)

    xn = xn_ref[...]
    a = jnp.dot(xn, wg_ref[...], preferred_element_type=F32)
    b = jnp.dot(xn, wu_ref[...], preferred_element_type=F32)
    o_ref[...] = (_silu(a) * b).astype(o_ref.dtype)


def norm_gateup(x, gain, wg, wu):
    t, d = x.shape
    f = wg.shape[1]
    tm, tf = _tile(t, TM_PROJ, SUBLANES), _tile(f, TF_FFN)
    return pl.pallas_call(
        _norm_gateup_kernel,
        out_shape=jax.ShapeDtypeStruct((t, f), BF16),
        grid=(t // tm, f // tf),
        in_specs=[
            pl.BlockSpec((tm, d), lambda i, j: (i, 0)),
            pl.BlockSpec((1, d), lambda i, j: (0, 0)),
            pl.BlockSpec((d, tf), lambda i, j: (0, j)),
            pl.BlockSpec((d, tf), lambda i, j: (0, j)),
        ],
        out_specs=pl.BlockSpec((tm, tf), lambda i, j: (i, j)),
        scratch_shapes=[pltpu.VMEM((tm, d), BF16)],
        compiler_params=_params(("arbitrary", "arbitrary")),
        name="norm_gateup",
    )(x, gain.reshape(1, d), wg, wu)


def _mm_res_kernel(a_ref, w_ref, r_ref, o_ref):
    o_ref[...] = r_ref[...] + jnp.dot(a_ref[...], w_ref[...], preferred_element_type=F32)


def matmul_residual(a, w, res):
    t, k = a.shape
    n = w.shape[1]
    tm = t
    for cand in (1024, 512, 256, 128):
        need = k * n * 2 + 2 * cand * k * 2 + 4 * cand * n * 4 + cand * n * 4
        if t % cand == 0 and need <= RES_VMEM_BUDGET:
            tm = cand
            break
    return pl.pallas_call(
        _mm_res_kernel,
        out_shape=jax.ShapeDtypeStruct((t, n), F32),
        grid=(t // tm,),
        in_specs=[
            pl.BlockSpec((tm, k), lambda i: (i, 0)),
            pl.BlockSpec((k, n), lambda i: (0, 0), pipeline_mode=pl.Buffered(1)),
            pl.BlockSpec((tm, n), lambda i: (i, 0)),
        ],
        out_specs=pl.BlockSpec((tm, n), lambda i: (i, 0)),
        compiler_params=_params(("arbitrary",)),
        name="matmul_residual",
    )(a, w, res)


def _ssd_kernel(x_ref, bm_ref, cm_ref, z_ref, dtt_ref, cwx_ref, cbx_ref, cwb_ref, cbb_ref,
                cwc_ref, cbc_ref, dtb_ref, alog_ref, dx_ref, gn_ref, e_ref, sh_ref,
                o_ref, pad_ref, h_ref, *, hpg, gw, nc):
    L = CHUNK
    n = SSM_STATE
    hd = gw // hpg
    rows = nc * L
    c = pl.program_id(2)
    tail = CONV_TAIL

    @pl.when(c == 0)
    def _():
        pad_ref[0:tail, :] = jnp.zeros((tail, gw + 2 * n), BF16)
        h_ref[...] = jnp.zeros_like(h_ref)

    @pl.when(c > 0)
    def _():
        pad_ref[0:tail, :] = pad_ref[rows:rows + tail, :]

    pad_ref[tail:tail + rows, 0:gw] = x_ref[0]
    pad_ref[tail:tail + rows, gw:gw + n] = bm_ref[0]
    pad_ref[tail:tail + rows, gw + n:gw + 2 * n] = cm_ref[0]

    def conv(shifted, t0, col0, width, w_ref, b_ref):
        acc = b_ref[...] + w_ref[CONV_WIDTH - 1:CONV_WIDTH, :] * pad_ref[
            t0 + tail:t0 + tail + L, col0:col0 + width].astype(F32)
        for k in range(CONV_WIDTH - 1):
            acc = acc + w_ref[k:k + 1, :] * shifted[k * L:(k + 1) * L, col0:col0 + width]
        return _silu(acc)

    raw = dtt_ref[0] + dtb_ref[...]
    dt_all = jnp.maximum(raw, 0.0) + jnp.log1p(jnp.exp(-jnp.abs(raw)))
    a_all = dt_all * (-jnp.exp(alog_ref[...]))
    row = lax.broadcasted_iota(jnp.int32, (L, L), 0)
    col = lax.broadcasted_iota(jnp.int32, (L, L), 1)
    upper = (row <= col).astype(F32)
    causal = row >= col
    lane = lax.broadcasted_iota(jnp.int32, (1, 2 * hd), 1)
    lo = lane < hd

    for ci in range(nc):
        t0 = ci * L
        shifted = jnp.dot(sh_ref[...], pad_ref[t0:t0 + tail + L, :],
                          preferred_element_type=F32)
        xs = conv(shifted, t0, 0, gw, cwx_ref, cbx_ref)
        bc = conv(shifted, t0, gw, n, cwb_ref, cbb_ref)
        cc = conv(shifted, t0, gw + n, n, cwc_ref, cbc_ref)

        dt = dt_all[:, t0:t0 + L]
        acs = jnp.dot(a_all[:, t0:t0 + L], upper, preferred_element_type=F32,
                      precision=lax.Precision.HIGHEST)
        a_last = acs[:, L - 1:L]
        dte = jnp.exp(a_last - acs)
        eacs = jnp.exp(acs)

        parts = _split3(acs) + _split3(dt) + _split3(dt * dte) + _split3(eacs)
        stacked = jnp.concatenate(parts, axis=0).T.astype(BF16)
        spread = jnp.dot(stacked, e_ref[...], preferred_element_type=F32)
        acs_col = spread[:, :hpg * L]
        dt_x = spread[:, hpg * L:hpg * L + gw]
        w_x = spread[:, hpg * L + gw:hpg * L + 2 * gw]
        eacs_x = spread[:, hpg * L + 2 * gw:]

        xdt = xs * dt_x
        bct = bc.T.astype(BF16)
        ccb = cc.astype(BF16)
        cb = jnp.dot(ccb, bct, preferred_element_type=F32)

        y_parts = []
        for q in range(hpg // 2):
            ms = []
            for j in (2 * q, 2 * q + 1):
                seg = acs_col[:, j * L:(j + 1) * L] - acs[j:j + 1, :]
                dec = jnp.exp(jnp.where(causal, seg, -jnp.inf))
                ms.append((cb * dec).astype(BF16))
            mcat = jnp.concatenate(ms, axis=1)
            xq = xdt[:, q * 2 * hd:(q + 1) * 2 * hd]
            rhs = jnp.concatenate([jnp.where(lo, xq, 0.0), jnp.where(lo, 0.0, xq)], axis=0)
            y_parts.append(jnp.dot(mcat, rhs.astype(BF16), preferred_element_type=F32))
        y_diag = jnp.concatenate(y_parts, axis=1)

        h_in = h_ref[...]
        y_off = jnp.dot(ccb, h_in.astype(BF16), preferred_element_type=F32) * eacs_x
        states = jnp.dot(bct, (xs * w_x).astype(BF16), preferred_element_type=F32)
        h_ref[...] = h_in * eacs_x[L - 1:L, :] + states

        y = y_diag + y_off + dx_ref[...] * xs
        gated = y * _silu(z_ref[0, t0:t0 + L, :].astype(F32))
        o_ref[0, t0:t0 + L, :] = _rms_rows(gated, gn_ref[...]).astype(o_ref.dtype)


def ssd_core(z, xbc, dtt, conv_w, conv_b, dt_bias, a_log, d_skip, out_norm):
    b, s, d_inner = z.shape
    g = SSM_GROUPS
    n = SSM_STATE
    heads = dt_bias.shape[0]
    hpg = heads // g
    gw = d_inner // g
    hd = gw // hpg
    L = CHUNK
    nb = d_inner // n
    assert gw % LANES == 0 and hpg % 2 == 0 and 2 * hd == LANES and s % L == 0

    eye = jnp.eye(hpg, dtype=F32)
    widths = [hpg * L, gw, gw, gw]
    mats = [jnp.repeat(eye, L, axis=1)] + [jnp.repeat(eye, hd, axis=1)] * 3
    spread_rows = []
    for qty in range(4):
        blk = jnp.concatenate([mats[qty] if c == qty else jnp.zeros((hpg, widths[c]), F32)
                               for c in range(4)], axis=1)
        spread_rows += [blk] * 3
    spread_mat = jnp.concatenate(spread_rows, axis=0).astype(BF16)
    dx = jnp.repeat(d_skip.astype(F32), hd).reshape(1, d_inner)

    nc = SSD_CHUNKS_PER_STEP if (s // L) % SSD_CHUNKS_PER_STEP == 0 else 1
    rows = nc * L
    out_row = jnp.arange((CONV_WIDTH - 1) * L, dtype=jnp.int32)
    src_row = out_row % L + CONV_TAIL - (CONV_WIDTH - 1) + out_row // L
    shift_mat = (src_row[:, None] == jnp.arange(CONV_TAIL + L, dtype=jnp.int32)[None, :]).astype(BF16)
    kern = functools.partial(_ssd_kernel, hpg=hpg, gw=gw, nc=nc)
    return pl.pallas_call(
        kern,
        out_shape=jax.ShapeDtypeStruct((b, s, d_inner), BF16),
        grid=(b, g, s // rows),
        in_specs=[
            pl.BlockSpec((1, rows, gw), lambda i, j, c: (i, c, j)),
            pl.BlockSpec((1, rows, n), lambda i, j, c: (i, c, nb + j)),
            pl.BlockSpec((1, rows, n), lambda i, j, c: (i, c, nb + g + j)),
            pl.BlockSpec((1, rows, gw), lambda i, j, c: (i, c, j)),
            pl.BlockSpec((1, hpg, rows), lambda i, j, c: (i, j, c)),
            pl.BlockSpec((CONV_WIDTH, gw), lambda i, j, c: (0, j)),
            pl.BlockSpec((1, gw), lambda i, j, c: (0, j)),
            pl.BlockSpec((CONV_WIDTH, n), lambda i, j, c: (0, nb + j)),
            pl.BlockSpec((1, n), lambda i, j, c: (0, nb + j)),
            pl.BlockSpec((CONV_WIDTH, n), lambda i, j, c: (0, nb + g + j)),
            pl.BlockSpec((1, n), lambda i, j, c: (0, nb + g + j)),
            pl.BlockSpec((hpg, 1), lambda i, j, c: (j, 0)),
            pl.BlockSpec((hpg, 1), lambda i, j, c: (j, 0)),
            pl.BlockSpec((1, gw), lambda i, j, c: (0, j)),
            pl.BlockSpec((1, gw), lambda i, j, c: (0, j)),
            pl.BlockSpec((12 * hpg, hpg * L + 3 * gw), lambda i, j, c: (0, 0)),
            pl.BlockSpec(((CONV_WIDTH - 1) * L, CONV_TAIL + L), lambda i, j, c: (0, 0)),
        ],
        out_specs=pl.BlockSpec((1, rows, gw), lambda i, j, c: (i, c, j)),
        scratch_shapes=[pltpu.VMEM((rows + CONV_TAIL, gw + 2 * n), BF16), pltpu.VMEM((n, gw), F32)],
        compiler_params=_params(("arbitrary", "arbitrary", "arbitrary")),
        name="ssd_core",
    )(xbc, xbc, xbc, z, dtt, conv_w, conv_b.reshape(1, -1), conv_w, conv_b.reshape(1, -1),
      conv_w, conv_b.reshape(1, -1), dt_bias.reshape(heads, 1), a_log.reshape(heads, 1),
      dx, out_norm.reshape(1, d_inner), spread_mat, shift_mat)


def _attn_kernel(q_ref, k_ref, v_ref, qg_ref, kg_ref, lq1_ref, lk1_ref, lq2_ref, lk2_ref,
                 sg_ref, avg_ref, o_ref, kn_ref, vt_ref, m_ref, acc_ref, s_ref, p_ref, a_ref,
                 *, tq, t, nsub, lambda_init):
    s_len = k_ref.shape[1]
    n_tiles = s_len // t
    hd = ATT_HEAD_DIM
    step = pl.program_id(2)
    lane = lax.broadcasted_iota(jnp.int32, (1, 2 * hd), 1)
    lo = lane < hd

    def comp_norm(x, gain):
        x2 = x * x
        hi = x2.astype(BF16)
        lo_part = (x2 - hi.astype(F32)).astype(BF16)
        ms = jnp.dot(jnp.concatenate([hi, lo_part], axis=1), avg_ref[...],
                     preferred_element_type=F32)
        return x * lax.rsqrt(ms + EPS) * gain

    @pl.when(step == 0)
    def _():
        def body(i, carry):
            r0 = pl.multiple_of(i * t, t)
            kn = comp_norm(k_ref[0, pl.ds(r0, t), :].astype(F32), kg_ref[...])
            kn_ref[pl.ds(r0, t), :] = kn.astype(BF16)
            vt_ref[i, 0:ATT_V_DIM, :] = v_ref[0, pl.ds(r0, t), :].astype(F32).T.astype(BF16)
            vt_ref[i, ATT_V_DIM:, :] = jnp.ones((ATT_SUM_ROWS, t), BF16)
            return carry

        lax.fori_loop(0, s_len // t, body, 0)

    lam = (jnp.exp(jnp.sum(lq1_ref[...] * lk1_ref[...], axis=-1, keepdims=True))
           - jnp.exp(jnp.sum(lq2_ref[...] * lk2_ref[...], axis=-1, keepdims=True)) + lambda_init)

    for sub in range(nsub):
        _attn_query_tile(step * nsub + sub, q_ref, sub * tq, o_ref, qg_ref, sg_ref, kn_ref, vt_ref,
                         m_ref.at[sub], acc_ref.at[sub], s_ref.at[sub], p_ref.at[sub],
                         a_ref.at[sub], comp_norm, lo, lam, tq=tq, t=t, n_tiles=n_tiles,
                         lambda_init=lambda_init)


def _attn_query_tile(qi, q_ref, q0, o_ref, qg_ref, sg_ref, kn_ref, vt_ref, m_ref, acc_ref, s_ref,
                     p_ref, a_ref, comp_norm, lo, lam, *, tq, t, n_tiles, lambda_init):
    hd = ATT_HEAD_DIM
    qn = comp_norm(q_ref[0, q0:q0 + tq, :].astype(F32), qg_ref[...]) * (hd ** -0.5 * LOG2E)
    qc = (jnp.where(lo, qn, 0.0).astype(BF16), jnp.where(lo, 0.0, qn).astype(BF16))

    m_ref[...] = jnp.full(m_ref.shape, -jnp.inf, F32)
    acc_ref[...] = jnp.zeros(acc_ref.shape, F32)

    def scores(tile, slot):
        r0 = pl.multiple_of(jnp.minimum(tile, n_tiles - 1) * t, t)
        kk = kn_ref[pl.ds(r0, t), :]
        for c in range(2):
            s_ref[slot, c] = lax.dot_general(kk, qc[c], (((1,), (1,)), ((), ())),
                                             preferred_element_type=F32)

    def softmax(tile, slot, masked):
        for c in range(2):
            sc = s_ref[slot, c]
            if masked:
                key = tile * t + lax.broadcasted_iota(jnp.int32, (t, tq), 0)
                qry = qi * tq + lax.broadcasted_iota(jnp.int32, (t, tq), 1)
                sc = jnp.where(key <= qry, sc, -jnp.inf)
            m_prev = m_ref[c]
            m_new = jnp.maximum(m_prev, jnp.max(sc, axis=0, keepdims=True))
            alpha = jnp.exp2(m_prev - m_new)
            p = jnp.exp2(sc - m_new)
            m_ref[c] = m_new
            a_ref[slot, c] = alpha
            p_ref[slot, c] = p.astype(BF16)

    def update(tile0, tile1):
        vt0 = vt_ref[jnp.minimum(tile0, n_tiles - 1)]
        vt1 = vt_ref[jnp.minimum(tile1, n_tiles - 1)]
        for c in range(2):
            a1 = a_ref[1, c]
            pv0 = jnp.dot(vt0, p_ref[0, c], preferred_element_type=F32)
            pv1 = jnp.dot(vt1, p_ref[1, c], preferred_element_type=F32)
            acc_ref[c] = (a_ref[0, c] * a1) * acc_ref[c] + a1 * pv0 + pv1

    d0 = 2 * ((qi * tq) // (2 * t))
    n_pairs = d0 // 2 + 1
    scores(d0, 0)
    scores(d0 + 1, 1)
    softmax(d0, 0, True)
    scores(0, 0)
    softmax(d0 + 1, 1, True)

    def pair(u, carry):
        x = 2 * u - 2
        first = u == 1
        update(jnp.where(first, d0, x - 2), jnp.where(first, d0 + 1, x - 1))
        scores(x + 1, 1)
        softmax(x, 0, False)
        scores(x + 2, 0)
        softmax(x + 1, 1, False)
        return carry

    lax.fori_loop(1, n_pairs, pair, 0)
    only = n_pairs == 1
    update(jnp.where(only, d0, d0 - 2), jnp.where(only, d0 + 1, d0 - 1))

    e = ATT_V_DIM
    o = (acc_ref[0, 0:e, :] / acc_ref[0, e:e + 1, :]
         - lam * (acc_ref[1, 0:e, :] / acc_ref[1, e:e + 1, :]))
    o = o * lax.rsqrt(jnp.mean(o * o, axis=0, keepdims=True) + EPS) * sg_ref[...]
    o_ref[0, q0:q0 + tq, :] = (o * (1.0 - lambda_init)).T.astype(o_ref.dtype)


def diff_attention_core(qkv, q_norm, k_norm, lq1, lk1, lq2, lk2, sub_norm, lambda_init):
    b, s, w3 = qkv.shape
    width = w3 // 3
    heads = width // ATT_V_DIM
    hd = ATT_HEAD_DIM
    tq = _tile(s, TQ_ATT, LANES)
    t = _tile(s, TK_ATT, LANES)
    assert tq % t == 0 and (2 * t) % tq == 0
    vec = lambda a: a.astype(F32).reshape(1, -1)
    qg = jnp.tile(q_norm.astype(F32), 2).reshape(1, 2 * hd)
    kg = jnp.tile(k_norm.astype(F32), 2).reshape(1, 2 * hd)
    small = lambda width_: pl.BlockSpec((1, width_), lambda i, h, j: (0, 0))
    comp = jnp.arange(2 * hd, dtype=jnp.int32) // hd
    avg = jnp.tile((comp[:, None] == comp[None, :]).astype(F32) / hd, (2, 1)).astype(BF16)
    nsub = ATT_Q_TILES_PER_STEP if (s // tq) % ATT_Q_TILES_PER_STEP == 0 else 1
    rows = nsub * tq
    kern = functools.partial(_attn_kernel, tq=tq, t=t, nsub=nsub, lambda_init=lambda_init)
    return pl.pallas_call(
        kern,
        out_shape=jax.ShapeDtypeStruct((b, s, width), BF16),
        grid=(b, heads, s // rows),
        in_specs=[
            pl.BlockSpec((1, rows, 2 * hd), lambda i, h, j: (i, j, h)),
            pl.BlockSpec((1, s, 2 * hd), lambda i, h, j: (i, 0, heads + h)),
            pl.BlockSpec((1, s, ATT_V_DIM), lambda i, h, j: (i, 0, 2 * heads + h)),
            small(2 * hd), small(2 * hd), small(hd), small(hd), small(hd), small(hd),
            pl.BlockSpec((ATT_V_DIM, 1), lambda i, h, j: (0, 0)),
            pl.BlockSpec((4 * hd, 2 * hd), lambda i, h, j: (0, 0)),
        ],
        out_specs=pl.BlockSpec((1, rows, ATT_V_DIM), lambda i, h, j: (i, j, h)),
        scratch_shapes=[
            pltpu.VMEM((s, 2 * hd), BF16),
            pltpu.VMEM((s // t, ATT_V_DIM + ATT_SUM_ROWS, t), BF16),
            pltpu.VMEM((nsub, 2, 1, tq), F32),
            pltpu.VMEM((nsub, 2, ATT_V_DIM + ATT_SUM_ROWS, tq), F32),
            pltpu.VMEM((nsub, 2, 2, t, tq), F32), pltpu.VMEM((nsub, 2, 2, t, tq), BF16),
            pltpu.VMEM((nsub, 2, 2, 1, tq), F32),
        ],
        compiler_params=_params(("arbitrary", "arbitrary", "arbitrary")),
        name="diff_attention",
    )(qkv, qkv, qkv, qg, kg, vec(lq1), vec(lk1), vec(lq2), vec(lk2),
      sub_norm.astype(F32).reshape(ATT_V_DIM, 1), avg)


def _route_kernel(x_ref, g_ref, r_ref, xn_ref, info_ref, cnt_ref, run_ref, *, n_experts):
    @pl.when(pl.program_id(0) == 0)
    def _():
        run_ref[...] = jnp.zeros_like(run_ref)

    xn = _rms_rows(x_ref[...], g_ref[...])
    xn_ref[...] = xn.astype(BF16)
    xh = xn.astype(BF16)
    xl = (xn - xh.astype(F32)).astype(BF16)
    hi = jnp.dot(xh, r_ref[...], preferred_element_type=F32)
    lo_ = jnp.dot(xl, r_ref[...], preferred_element_type=F32)
    logits = (hi[:, :LANES] + hi[:, LANES:]) + (lo_[:, :LANES] + lo_[:, LANES:])
    lane = lax.broadcasted_iota(jnp.int32, logits.shape, 1)
    logits = jnp.where(lane < n_experts, logits, -jnp.inf)
    m1 = jnp.max(logits, axis=-1, keepdims=True)
    i1 = jnp.min(jnp.where(logits == m1, lane, LANES), axis=-1, keepdims=True)
    rest = jnp.where(lane == i1, -jnp.inf, logits)
    m2 = jnp.max(rest, axis=-1, keepdims=True)
    i2 = jnp.min(jnp.where(rest == m2, lane, LANES), axis=-1, keepdims=True)
    e2 = jnp.exp(m2 - m1)
    g1 = 1.0 / (1.0 + e2)
    g2 = e2 * g1
    tm = logits.shape[0]
    oh1 = (lane == i1).astype(F32)
    oh2 = (lane == i2).astype(F32)
    both = oh1 + oh2
    row = lax.broadcasted_iota(jnp.int32, (tm, tm), 0)
    col = lax.broadcasted_iota(jnp.int32, (tm, tm), 1)
    before = jnp.dot((col < row).astype(BF16), both.astype(BF16),
                     preferred_element_type=F32) + run_ref[0:1, :]
    r1 = jnp.sum(before * oh1, axis=-1, keepdims=True)
    r2 = jnp.sum(before * oh2, axis=-1, keepdims=True)
    run_ref[...] = run_ref[...] + jnp.sum(both, axis=0, keepdims=True)
    cnt_ref[...] = run_ref[...]
    info = jnp.zeros_like(logits)
    for k, val in enumerate((i1.astype(F32), i2.astype(F32), g1, g2, r1, r2)):
        info = jnp.where(lane == k, val, info)
    info_ref[...] = info


def moe_route(h, gain, router):
    t, d = h.shape
    e = router.shape[1]
    tm = _tile(t, TM_ROUTE, SUBLANES)
    rpad = jnp.zeros((d, LANES), F32).at[:, :e].set(router.astype(F32))
    r_hi = rpad.astype(BF16)
    r_parts = jnp.concatenate([r_hi, (rpad - r_hi.astype(F32)).astype(BF16)], axis=1)
    return pl.pallas_call(
        functools.partial(_route_kernel, n_experts=e),
        out_shape=(jax.ShapeDtypeStruct((t, d), BF16), jax.ShapeDtypeStruct((t, LANES), F32),
                   jax.ShapeDtypeStruct((SUBLANES, LANES), F32)),
        grid=(t // tm,),
        in_specs=[
            pl.BlockSpec((tm, d), lambda i: (i, 0)),
            pl.BlockSpec((1, d), lambda i: (0, 0)),
            pl.BlockSpec((d, 2 * LANES), lambda i: (0, 0)),
        ],
        out_specs=(pl.BlockSpec((tm, d), lambda i: (i, 0)),
                   pl.BlockSpec((tm, LANES), lambda i: (i, 0)),
                   pl.BlockSpec((SUBLANES, LANES), lambda i: (0, 0))),
        scratch_shapes=[pltpu.VMEM((SUBLANES, LANES), F32)],
        compiler_params=_params(("arbitrary",)),
        name="moe_route",
    )(h, gain.reshape(1, d), r_parts)


def _new_weights(te_ref, i):
    prev = te_ref[jnp.maximum(i - 1, 0)]
    return jnp.logical_or(i == 0, te_ref[i] != prev)


def _moe_gateup_kernel(te_ref, nu_ref, x_ref, wg_ref, wu_ref, o_ref, wgb_ref, wub_ref):
    i = pl.program_id(1)

    @pl.when(_new_weights(te_ref, i))
    def _():
        wgb_ref[...] = wg_ref[0].astype(BF16)
        wub_ref[...] = wu_ref[0].astype(BF16)

    @pl.when(i < nu_ref[0])
    def _():
        x = x_ref[...]
        a = jnp.dot(x, wgb_ref[...], preferred_element_type=F32)
        b = jnp.dot(x, wub_ref[...], preferred_element_type=F32)
        o_ref[...] = (_silu(a) * b).astype(o_ref.dtype)

    @pl.when(i >= nu_ref[0])
    def _():
        o_ref[...] = jnp.zeros_like(o_ref)


def _moe_down_kernel(te_ref, nu_ref, h_ref, wd_ref, o_ref, wdb_ref):
    i = pl.program_id(1)

    @pl.when(_new_weights(te_ref, i))
    def _():
        wdb_ref[...] = wd_ref[0].astype(BF16)

    @pl.when(i < nu_ref[0])
    def _():
        o_ref[...] = jnp.dot(h_ref[...], wdb_ref[...],
                             preferred_element_type=F32).astype(o_ref.dtype)

    @pl.when(i >= nu_ref[0])
    def _():
        o_ref[...] = jnp.zeros_like(o_ref)


def moe_experts(xs, tile_expert, n_used, w_gate, w_up, w_down, tm):
    slots, d = xs.shape
    e, _, f = w_gate.shape
    n_tiles = slots // tm
    tf = _tile(f, TF_FFN)
    hmid = pl.pallas_call(
        _moe_gateup_kernel,
        out_shape=jax.ShapeDtypeStruct((slots, f), BF16),
        grid_spec=pltpu.PrefetchScalarGridSpec(
            num_scalar_prefetch=2,
            grid=(f // tf, n_tiles),
            in_specs=[
                pl.BlockSpec((tm, d), lambda j, i, te, nu: (i, 0)),
                pl.BlockSpec((1, d, tf), lambda j, i, te, nu: (te[i], 0, j)),
                pl.BlockSpec((1, d, tf), lambda j, i, te, nu: (te[i], 0, j)),
            ],
            out_specs=pl.BlockSpec((tm, tf), lambda j, i, te, nu: (i, j)),
            scratch_shapes=[pltpu.VMEM((d, tf), BF16), pltpu.VMEM((d, tf), BF16)],
        ),
        compiler_params=_params(("arbitrary", "arbitrary")),
        name="moe_gateup",
    )(tile_expert, n_used, xs, w_gate, w_up)

    tn = _tile(d, TN_MOE_DOWN)
    return pl.pallas_call(
        _moe_down_kernel,
        out_shape=jax.ShapeDtypeStruct((slots, d), BF16),
        grid_spec=pltpu.PrefetchScalarGridSpec(
            num_scalar_prefetch=2,
            grid=(d // tn, n_tiles),
            in_specs=[
                pl.BlockSpec((tm, f), lambda j, i, te, nu: (i, 0)),
                pl.BlockSpec((1, f, tn), lambda j, i, te, nu: (te[i], 0, j),
                             pipeline_mode=pl.Buffered(1)),
            ],
            out_specs=pl.BlockSpec((tm, tn), lambda j, i, te, nu: (i, j)),
            scratch_shapes=[pltpu.VMEM((f, tn), BF16)],
        ),
        compiler_params=_params(("arbitrary", "arbitrary")),
        name="moe_down",
    )(tile_expert, n_used, hmid, w_down)


def moe_block(h, gain, router, w_gate, w_up, w_down):
    t, d = h.shape
    e = router.shape[1]
    tm = min(TM_MOE, t)
    xn, info, cnt = moe_route(h, gain, router)
    idx = info[:, :TOP_K].astype(jnp.int32)
    gates = info[:, TOP_K:2 * TOP_K]
    rank = info[:, 2 * TOP_K:3 * TOP_K].astype(jnp.int32)
    counts = cnt[0, :e].astype(jnp.int32)

    tiles_per_e = (counts + tm - 1) // tm
    tile_end = jnp.cumsum(tiles_per_e)
    starts = (tile_end - tiles_per_e) * tm
    slot = (jnp.take(starts, idx.reshape(-1)) + rank.reshape(-1))
    n_tiles = (TOP_K * t) // tm + e
    n_slots = n_tiles * tm
    token_of_slot = jnp.zeros((n_slots,), jnp.int32).at[slot].set(
        jnp.arange(TOP_K * t, dtype=jnp.int32) // TOP_K)
    tile_ids = jnp.arange(n_tiles, dtype=jnp.int32)
    tile_expert = jnp.minimum(jnp.sum((tile_ids[:, None] >= tile_end[None, :]).astype(jnp.int32), axis=1),
                              e - 1)
    n_used = tile_end[-1:].astype(jnp.int32)

    xs = xn.at[token_of_slot].get(mode="promise_in_bounds")
    ys = moe_experts(xs, tile_expert, n_used, w_gate, w_up, w_down, tm)
    slot2 = slot.reshape(t, TOP_K)
    out = h
    for k in range(TOP_K):
        out = out + gates[:, k:k + 1] * ys.at[slot2[:, k]].get(mode="promise_in_bounds").astype(F32)
    return out


def mamba_layer(h, b, s, pre_norm, in_proj, conv_w, conv_b, dt_bias, a_log, d_skip, out_norm,
                out_proj):
    d_inner = out_proj.shape[0]
    d_xbc = conv_w.shape[1]
    heads = dt_bias.shape[0]
    z = norm_matmul(h, pre_norm, in_proj[:, :d_inner].astype(BF16), BF16)
    xbc = norm_matmul(h, pre_norm, in_proj[:, d_inner:d_inner + d_xbc].astype(BF16), BF16)
    w_dt = jnp.zeros((h.shape[1], LANES), BF16).at[:, :heads].set(
        in_proj[:, d_inner + d_xbc:].astype(BF16))
    dt = norm_matmul(h, pre_norm, w_dt, F32)[:, :heads]
    dtt = jnp.swapaxes(dt.reshape(b, s, heads), 1, 2)
    y = ssd_core(z.reshape(b, s, d_inner), xbc.reshape(b, s, d_xbc), dtt, conv_w.astype(F32),
                 conv_b.astype(F32), dt_bias.astype(F32), a_log.astype(F32), d_skip, out_norm)
    return matmul_residual(y.reshape(b * s, d_inner), out_proj.astype(BF16), h)


def dense_layer(h, pre_norm, w_gate, w_up, w_down):
    mid = norm_gateup(h, pre_norm, w_gate.astype(BF16), w_up.astype(BF16))
    return matmul_residual(mid, w_down.astype(BF16), h)


def attention_layer(h, b, s, pre_norm, w_qkv, q_norm, k_norm, lq1, lk1, lq2, lk2, sub_norm, w_o,
                    lambda_init):
    qkv = norm_matmul(h, pre_norm, w_qkv.astype(BF16), BF16)
    o = diff_attention_core(qkv.reshape(b, s, -1), q_norm, k_norm, lq1, lk1, lq2, lk2, sub_norm,
                            lambda_init)
    return matmul_residual(o.reshape(b * s, -1), w_o.astype(BF16), h)


def kernel(x, ssm_pre_norm, ssm_in_proj, ssm_conv_w, ssm_conv_b, ssm_dt_bias, ssm_A_log, ssm_D,
           ssm_out_norm, ssm_out_proj, dense_pre_norm, dense_w_gate, dense_w_up, dense_w_down,
           att_pre_norm, att_w_qkv, att_q_norm, att_k_norm, att_lambda_q1, att_lambda_k1,
           att_lambda_q2, att_lambda_k2, att_sub_norm, att_w_o, moe_pre_norm, moe_router,
           moe_w_gate, moe_w_up, moe_w_down):
    b, s, d = x.shape
    h = x.reshape(b * s, d)
    for i in range(DEPTH):
        j = i // N_MIXERS
        if i % N_MIXERS == 0:
            h = mamba_layer(h, b, s, ssm_pre_norm[j], ssm_in_proj[j], ssm_conv_w[j],
                            ssm_conv_b[j], ssm_dt_bias[j], ssm_A_log[j], ssm_D[j],
                            ssm_out_norm[j], ssm_out_proj[j])
            h = dense_layer(h, dense_pre_norm[j], dense_w_gate[j], dense_w_up[j],
                            dense_w_down[j])
        else:
            lambda_init = 0.8 - 0.6 * math.exp(-0.3 * i)
            h = attention_layer(h, b, s, att_pre_norm[j], att_w_qkv[j], att_q_norm[j],
                                att_k_norm[j], att_lambda_q1[j], att_lambda_k1[j],
                                att_lambda_q2[j], att_lambda_k2[j], att_sub_norm[j], att_w_o[j],
                                lambda_init)
            h = moe_block(h, moe_pre_norm[j], moe_router[j], moe_w_gate[j], moe_w_up[j],
                          moe_w_down[j])
    return h.reshape(b, s, d)
```

```python
import functools
import math

import jax
import jax.numpy as jnp
from jax import lax
from jax.experimental import pallas as pl
from jax.experimental.pallas import tpu as pltpu

F32 = jnp.float32
BF16 = jnp.bfloat16

EPS = 1e-5
DEPTH = 2
N_MIXERS = 2

SSM_GROUPS = 8
SSM_STATE = 128
CHUNK = 128
CONV_WIDTH = 4
SSM_HEAD_DIM = 64
CONV_TAIL = 16

ATT_HEAD_DIM = 64
ATT_V_DIM = 2 * ATT_HEAD_DIM
ATT_SUM_ROWS = 16

TOP_K = 2

LOG2E = math.log2(math.e)

LANES = 128
SUBLANES = 8
VMEM_LIMIT_BYTES = 56 * 1024 * 1024

TM_PROJ = 1024
TN_PROJ = 1024
TM_ROUTE = 512
RES_VMEM_BUDGET = 46 * 1024 * 1024
TF_FFN = 512
SSD_CHUNKS_PER_STEP = 8
TM_MOE = 512
TN_MOE_DOWN = 1024
TQ_ATT = 512
TK_ATT = 256
ATT_Q_TILES_PER_STEP = 8


def _tile(dim, pref, quantum=LANES):
    if dim <= pref:
        return dim
    t = (pref // quantum) * quantum
    while t >= quantum:
        if dim % t == 0:
            return t
        t -= quantum
    return dim


def _params(sem):
    return pltpu.CompilerParams(dimension_semantics=sem, vmem_limit_bytes=VMEM_LIMIT_BYTES)


def _rms_rows(x, gain):
    return x * lax.rsqrt(jnp.mean(x * x, axis=-1, keepdims=True) + EPS) * gain


def _silu(x):
    h = 0.5 * x
    return h + h * jnp.tanh(h)


def _split3(v):
    hi = v.astype(BF16).astype(F32)
    r = v - hi
    mid = r.astype(BF16).astype(F32)
    return [hi, mid, r - mid]


def _norm_mm_kernel(x_ref, g_ref, w_ref, o_ref, xn_ref):
    @pl.when(pl.program_id(1) == 0)
    def _():
        xn_ref[...] = _rms_rows(x_ref[...], g_ref[...]).astype(BF16)

    o_ref[...] = jnp.dot(xn_ref[...], w_ref[...], preferred_element_type=F32).astype(o_ref.dtype)


def norm_matmul(x, gain, w, out_dtype):
    t, d = x.shape
    n = w.shape[1]
    tm, tn = _tile(t, TM_PROJ, SUBLANES), _tile(n, TN_PROJ)
    return pl.pallas_call(
        _norm_mm_kernel,
        out_shape=jax.ShapeDtypeStruct((t, n), out_dtype),
        grid=(t // tm, n // tn),
        in_specs=[
            pl.BlockSpec((tm, d), lambda i, j: (i, 0)),
            pl.BlockSpec((1, d), lambda i, j: (0, 0)),
            pl.BlockSpec((d, tn), lambda i, j: (0, j)),
        ],
        out_specs=pl.BlockSpec((tm, tn), lambda i, j: (i, j)),
        scratch_shapes=[pltpu.VMEM((tm, d), BF16)],
        compiler_params=_params(("arbitrary", "arbitrary")),
        name="norm_matmul",
    )(x, gain.reshape(1, d), w)


def _norm_gateup_kernel(x_ref, g_ref, wg_ref, wu_ref, o_ref, xn_ref):
    @pl.when(pl.program_id(1) == 0)
    def _():
        xn_ref[...] = _rms_rows(x_ref[...], g_ref[...]).astype(BF16)

    xn = xn_ref[...]
    a = jnp.dot(xn, wg_ref[...], preferred_element_type=F32)
    b = jnp.dot(xn, wu_ref[...], preferred_element_type=F32)
    o_ref[...] = (_silu(a) * b).astype(o_ref.dtype)


def norm_gateup(x, gain, wg, wu):
    t, d = x.shape
    f = wg.shape[1]
    tm, tf = _tile(t, TM_PROJ, SUBLANES), _tile(f, TF_FFN)
    return pl.pallas_call(
        _norm_gateup_kernel,
        out_shape=jax.ShapeDtypeStruct((t, f), BF16),
        grid=(t // tm, f // tf),
        in_specs=[
            pl.BlockSpec((tm, d), lambda i, j: (i, 0)),
            pl.BlockSpec((1, d), lambda i, j: (0, 0)),
            pl.BlockSpec((d, tf), lambda i, j: (0, j)),
            pl.BlockSpec((d, tf), lambda i, j: (0, j)),
        ],
        out_specs=pl.BlockSpec((tm, tf), lambda i, j: (i, j)),
        scratch_shapes=[pltpu.VMEM((tm, d), BF16)],
        compiler_params=_params(("arbitrary", "arbitrary")),
        name="norm_gateup",
    )(x, gain.reshape(1, d), wg, wu)


def _mm_res_kernel(a_ref, w_ref, r_ref, o_ref):
    o_ref[...] = r_ref[...] + jnp.dot(a_ref[...], w_ref[...], preferred_element_type=F32)


def matmul_residual(a, w, res):
    t, k = a.shape
    n = w.shape[1]
    tm = t
    for cand in (1024, 512, 256, 128):
        need = k * n * 2 + 2 * cand * k * 2 + 4 * cand * n * 4 + cand * n * 4
        if t % cand == 0 and need <= RES_VMEM_BUDGET:
            tm = cand
            break
    return pl.pallas_call(
        _mm_res_kernel,
        out_shape=jax.ShapeDtypeStruct((t, n), F32),
        grid=(t // tm,),
        in_specs=[
            pl.BlockSpec((tm, k), lambda i: (i, 0)),
            pl.BlockSpec((k, n), lambda i: (0, 0), pipeline_mode=pl.Buffered(1)),
            pl.BlockSpec((tm, n), lambda i: (i, 0)),
        ],
        out_specs=pl.BlockSpec((tm, n), lambda i: (i, 0)),
        compiler_params=_params(("arbitrary",)),
        name="matmul_residual",
    )(a, w, res)


def _ssd_kernel(x_ref, bm_ref, cm_ref, z_ref, dtt_ref, cwx_ref, cbx_ref, cwb_ref, cbb_ref,
                cwc_ref, cbc_ref, dtb_ref, alog_ref, dx_ref, gn_ref, e_ref, sh_ref,
                o_ref, pad_ref, h_ref, *, hpg, gw, nc):
    L = CHUNK
    n = SSM_STATE
    hd = gw // hpg
    rows = nc * L
    c = pl.program_id(2)
    tail = CONV_TAIL

    @pl.when(c == 0)
    def _():
        pad_ref[0:tail, :] = jnp.zeros((tail, gw + 2 * n), BF16)
        h_ref[...] = jnp.zeros_like(h_ref)

    @pl.when(c > 0)
    def _():
        pad_ref[0:tail, :] = pad_ref[rows:rows + tail, :]

    pad_ref[tail:tail + rows, 0:gw] = x_ref[0]
    pad_ref[tail:tail + rows, gw:gw + n] = bm_ref[0]
    pad_ref[tail:tail + rows, gw + n:gw + 2 * n] = cm_ref[0]

    def conv(shifted, t0, col0, width, w_ref, b_ref):
        acc = b_ref[...] + w_ref[CONV_WIDTH - 1:CONV_WIDTH, :] * pad_ref[
            t0 + tail:t0 + tail + L, col0:col0 + width].astype(F32)
        for k in range(CONV_WIDTH - 1):
            acc = acc + w_ref[k:k + 1, :] * shifted[k * L:(k + 1) * L, col0:col0 + width]
        return _silu(acc)

    raw = dtt_ref[0] + dtb_ref[...]
    dt_all = jnp.maximum(raw, 0.0) + jnp.log1p(jnp.exp(-jnp.abs(raw)))
    a_all = dt_all * (-jnp.exp(alog_ref[...]))
    row = lax.broadcasted_iota(jnp.int32, (L, L), 0)
    col = lax.broadcasted_iota(jnp.int32, (L, L), 1)
    upper = (row <= col).astype(F32)
    causal = row >= col
    lane = lax.broadcasted_iota(jnp.int32, (1, 2 * hd), 1)
    lo = lane < hd

    for ci in range(nc):
        t0 = ci * L
        shifted = jnp.dot(sh_ref[...], pad_ref[t0:t0 + tail + L, :],
                          preferred_element_type=F32)
        xs = conv(shifted, t0, 0, gw, cwx_ref, cbx_ref)
        bc = conv(shifted, t0, gw, n, cwb_ref, cbb_ref)
        cc = conv(shifted, t0, gw + n, n, cwc_ref, cbc_ref)

        dt = dt_all[:, t0:t0 + L]
        acs = jnp.dot(a_all[:, t0:t0 + L], upper, preferred_element_type=F32,
                      precision=lax.Precision.HIGHEST)
        a_last = acs[:, L - 1:L]
        dte = jnp.exp(a_last - acs)
        eacs = jnp.exp(acs)

        parts = _split3(acs) + _split3(dt) + _split3(dt * dte) + _split3(eacs)
        stacked = jnp.concatenate(parts, axis=0).T.astype(BF16)
        spread = jnp.dot(stacked, e_ref[...], preferred_element_type=F32)
        acs_col = spread[:, :hpg * L]
        dt_x = spread[:, hpg * L:hpg * L + gw]
        w_x = spread[:, hpg * L + gw:hpg * L + 2 * gw]
        eacs_x = spread[:, hpg * L + 2 * gw:]

        xdt = xs * dt_x
        bct = bc.T.astype(BF16)
        ccb = cc.astype(BF16)
        cb = jnp.dot(ccb, bct, preferred_element_type=F32)

        y_parts = []
        for q in range(hpg // 2):
            ms = []
            for j in (2 * q, 2 * q + 1):
                seg = acs_col[:, j * L:(j + 1) * L] - acs[j:j + 1, :]
                dec = jnp.exp(jnp.where(causal, seg, -jnp.inf))
                ms.append((cb * dec).astype(BF16))
            mcat = jnp.concatenate(ms, axis=1)
            xq = xdt[:, q * 2 * hd:(q + 1) * 2 * hd]
            rhs = jnp.concatenate([jnp.where(lo, xq, 0.0), jnp.where(lo, 0.0, xq)], axis=0)
            y_parts.append(jnp.dot(mcat, rhs.astype(BF16), preferred_element_type=F32))
        y_diag = jnp.concatenate(y_parts, axis=1)

        h_in = h_ref[...]
        y_off = jnp.dot(ccb, h_in.astype(BF16), preferred_element_type=F32) * eacs_x
        states = jnp.dot(bct, (xs * w_x).astype(BF16), preferred_element_type=F32)
        h_ref[...] = h_in * eacs_x[L - 1:L, :] + states

        y = y_diag + y_off + dx_ref[...] * xs
        gated = y * _silu(z_ref[0, t0:t0 + L, :].astype(F32))
        o_ref[0, t0:t0 + L, :] = _rms_rows(gated, gn_ref[...]).astype(o_ref.dtype)


def ssd_core(z, xbc, dtt, conv_w, conv_b, dt_bias, a_log, d_skip, out_norm):
    b, s, d_inner = z.shape
    g = SSM_GROUPS
    n = SSM_STATE
    heads = dt_bias.shape[0]
    hpg = heads // g
    gw = d_inner // g
    hd = gw // hpg
    L = CHUNK
    nb = d_inner // n
    assert gw % LANES == 0 and hpg % 2 == 0 and 2 * hd == LANES and s % L == 0

    eye = jnp.eye(hpg, dtype=F32)
    widths = [hpg * L, gw, gw, gw]
    mats = [jnp.repeat(eye, L, axis=1)] + [jnp.repeat(eye, hd, axis=1)] * 3
    spread_rows = []
    for qty in range(4):
        blk = jnp.concatenate([mats[qty] if c == qty else jnp.zeros((hpg, widths[c]), F32)
                               for c in range(4)], axis=1)
        spread_rows += [blk] * 3
    spread_mat = jnp.concatenate(spread_rows, axis=0).astype(BF16)
    dx = jnp.repeat(d_skip.astype(F32), hd).reshape(1, d_inner)

    nc = SSD_CHUNKS_PER_STEP if (s // L) % SSD_CHUNKS_PER_STEP == 0 else 1
    rows = nc * L
    out_row = jnp.arange((CONV_WIDTH - 1) * L, dtype=jnp.int32)
    src_row = out_row % L + CONV_TAIL - (CONV_WIDTH - 1) + out_row // L
    shift_mat = (src_row[:, None] == jnp.arange(CONV_TAIL + L, dtype=jnp.int32)[None, :]).astype(BF16)
    kern = functools.partial(_ssd_kernel, hpg=hpg, gw=gw, nc=nc)
    return pl.pallas_call(
        kern,
        out_shape=jax.ShapeDtypeStruct((b, s, d_inner), BF16),
        grid=(b, g, s // rows),
        in_specs=[
            pl.BlockSpec((1, rows, gw), lambda i, j, c: (i, c, j)),
            pl.BlockSpec((1, rows, n), lambda i, j, c: (i, c, nb + j)),
            pl.BlockSpec((1, rows, n), lambda i, j, c: (i, c, nb + g + j)),
            pl.BlockSpec((1, rows, gw), lambda i, j, c: (i, c, j)),
            pl.BlockSpec((1, hpg, rows), lambda i, j, c: (i, j, c)),
            pl.BlockSpec((CONV_WIDTH, gw), lambda i, j, c: (0, j)),
            pl.BlockSpec((1, gw), lambda i, j, c: (0, j)),
            pl.BlockSpec((CONV_WIDTH, n), lambda i, j, c: (0, nb + j)),
            pl.BlockSpec((1, n), lambda i, j, c: (0, nb + j)),
            pl.BlockSpec((CONV_WIDTH, n), lambda i, j, c: (0, nb + g + j)),
            pl.BlockSpec((1, n), lambda i, j, c: (0, nb + g + j)),
            pl.BlockSpec((hpg, 1), lambda i, j, c: (j, 0)),
            pl.BlockSpec((hpg, 1), lambda i, j, c: (j, 0)),
            pl.BlockSpec((1, gw), lambda i, j, c: (0, j)),
            pl.BlockSpec((1, gw), lambda i, j, c: (0, j)),
            pl.BlockSpec((12 * hpg, hpg * L + 3 * gw), lambda i, j, c: (0, 0)),
            pl.BlockSpec(((CONV_WIDTH - 1) * L, CONV_TAIL + L), lambda i, j, c: (0, 0)),
        ],
        out_specs=pl.BlockSpec((1, rows, gw), lambda i, j, c: (i, c, j)),
        scratch_shapes=[pltpu.VMEM((rows + CONV_TAIL, gw + 2 * n), BF16), pltpu.VMEM((n, gw), F32)],
        compiler_params=_params(("arbitrary", "arbitrary", "arbitrary")),
        name="ssd_core",
    )(xbc, xbc, xbc, z, dtt, conv_w, conv_b.reshape(1, -1), conv_w, conv_b.reshape(1, -1),
      conv_w, conv_b.reshape(1, -1), dt_bias.reshape(heads, 1), a_log.reshape(heads, 1),
      dx, out_norm.reshape(1, d_inner), spread_mat, shift_mat)


def _attn_kernel(q_ref, k_ref, v_ref, qg_ref, kg_ref, lq1_ref, lk1_ref, lq2_ref, lk2_ref,
                 sg_ref, avg_ref, o_ref, kn_ref, vt_ref, m_ref, acc_ref, s_ref, p_ref, a_ref,
                 *, tq, t, nsub, lambda_init):
    s_len = k_ref.shape[1]
    n_tiles = s_len // t
    hd = ATT_HEAD_DIM
    step = pl.program_id(2)
    lane = lax.broadcasted_iota(jnp.int32, (1, 2 * hd), 1)
    lo = lane < hd

    def comp_norm(x, gain):
        x2 = x * x
        hi = x2.astype(BF16)
        lo_part = (x2 - hi.astype(F32)).astype(BF16)
        ms = jnp.dot(jnp.concatenate([hi, lo_part], axis=1), avg_ref[...],
                     preferred_element_type=F32)
        return x * lax.rsqrt(ms + EPS) * gain

    @pl.when(step == 0)
    def _():
        def body(i, carry):
            r0 = pl.multiple_of(i * t, t)
            kn = comp_norm(k_ref[0, pl.ds(r0, t), :].astype(F32), kg_ref[...])
            kn_ref[pl.ds(r0, t), :] = kn.astype(BF16)
            vt_ref[i, 0:ATT_V_DIM, :] = v_ref[0, pl.ds(r0, t), :].astype(F32).T.astype(BF16)
            vt_ref[i, ATT_V_DIM:, :] = jnp.ones((ATT_SUM_ROWS, t), BF16)
            return carry

        lax.fori_loop(0, s_len // t, body, 0)

    lam = (jnp.exp(jnp.sum(lq1_ref[...] * lk1_ref[...], axis=-1, keepdims=True))
           - jnp.exp(jnp.sum(lq2_ref[...] * lk2_ref[...], axis=-1, keepdims=True)) + lambda_init)

    whole_head = nsub * tq == s_len
    for sub in range(nsub):
        _attn_query_tile(sub if whole_head else step * nsub + sub, q_ref, sub * tq, o_ref, qg_ref, sg_ref, kn_ref, vt_ref,
                         m_ref.at[sub], acc_ref.at[sub], s_ref.at[sub], p_ref.at[sub],
                         a_ref.at[sub], comp_norm, lo, lam, tq=tq, t=t, n_tiles=n_tiles,
                         lambda_init=lambda_init)


def _attn_query_tile(qi, q_ref, q0, o_ref, qg_ref, sg_ref, kn_ref, vt_ref, m_ref, acc_ref, s_ref,
                     p_ref, a_ref, comp_norm, lo, lam, *, tq, t, n_tiles, lambda_init):
    hd = ATT_HEAD_DIM
    qn = comp_norm(q_ref[0, q0:q0 + tq, :].astype(F32), qg_ref[...]) * (hd ** -0.5 * LOG2E)
    qc = (jnp.where(lo, qn, 0.0).astype(BF16), jnp.where(lo, 0.0, qn).astype(BF16))

    m_ref[...] = jnp.full(m_ref.shape, -jnp.inf, F32)
    acc_ref[...] = jnp.zeros(acc_ref.shape, F32)

    def scores(tile, slot):
        r0 = pl.multiple_of(jnp.minimum(tile, n_tiles - 1) * t, t)
        kk = kn_ref[pl.ds(r0, t), :]
        for c in range(2):
            s_ref[slot, c] = lax.dot_general(kk, qc[c], (((1,), (1,)), ((), ())),
                                             preferred_element_type=F32)

    def softmax(tile, slot, masked):
        for c in range(2):
            sc = s_ref[slot, c]
            if masked:
                key = tile * t + lax.broadcasted_iota(jnp.int32, (t, tq), 0)
                qry = qi * tq + lax.broadcasted_iota(jnp.int32, (t, tq), 1)
                sc = jnp.where(key <= qry, sc, -jnp.inf)
            m_prev = m_ref[c]
            m_new = jnp.maximum(m_prev, jnp.max(sc, axis=0, keepdims=True))
            alpha = jnp.exp2(m_prev - m_new)
            p = jnp.exp2(sc - m_new)
            m_ref[c] = m_new
            a_ref[slot, c] = alpha
            p_ref[slot, c] = p.astype(BF16)

    def update(tile0, tile1):
        vt0 = vt_ref[jnp.minimum(tile0, n_tiles - 1)]
        vt1 = vt_ref[jnp.minimum(tile1, n_tiles - 1)]
        for c in range(2):
            a1 = a_ref[1, c]
            pv0 = jnp.dot(vt0, p_ref[0, c], preferred_element_type=F32)
            pv1 = jnp.dot(vt1, p_ref[1, c], preferred_element_type=F32)
            acc_ref[c] = (a_ref[0, c] * a1) * acc_ref[c] + a1 * pv0 + pv1

    d0 = 2 * ((qi * tq) // (2 * t))
    n_pairs = d0 // 2 + 1
    scores(d0, 0)
    scores(d0 + 1, 1)
    softmax(d0, 0, True)
    scores(0, 0)
    softmax(d0 + 1, 1, True)

    def pair(u, carry):
        x = 2 * u - 2
        first = u == 1
        update(jnp.where(first, d0, x - 2), jnp.where(first, d0 + 1, x - 1))
        scores(x + 1, 1)
        softmax(x, 0, False)
        scores(x + 2, 0)
        softmax(x + 1, 1, False)
        return carry

    if isinstance(qi, int):
        for u in range(1, n_pairs):
            pair(u, 0)
    else:
        lax.fori_loop(1, n_pairs, pair, 0)
    only = n_pairs == 1
    update(jnp.where(only, d0, d0 - 2), jnp.where(only, d0 + 1, d0 - 1))

    e = ATT_V_DIM
    o = (acc_ref[0, 0:e, :] / acc_ref[0, e:e + 1, :]
         - lam * (acc_ref[1, 0:e, :] / acc_ref[1, e:e + 1, :]))
    o = o * lax.rsqrt(jnp.mean(o * o, axis=0, keepdims=True) + EPS) * sg_ref[...]
    o_ref[0, q0:q0 + tq, :] = (o * (1.0 - lambda_init)).T.astype(o_ref.dtype)


def diff_attention_core(qkv, q_norm, k_norm, lq1, lk1, lq2, lk2, sub_norm, lambda_init):
    b, s, w3 = qkv.shape
    width = w3 // 3
    heads = width // ATT_V_DIM
    hd = ATT_HEAD_DIM
    tq = _tile(s, TQ_ATT, LANES)
    t = _tile(s, TK_ATT, LANES)
    assert tq % t == 0 and (2 * t) % tq == 0
    vec = lambda a: a.astype(F32).reshape(1, -1)
    qg = jnp.tile(q_norm.astype(F32), 2).reshape(1, 2 * hd)
    kg = jnp.tile(k_norm.astype(F32), 2).reshape(1, 2 * hd)
    small = lambda width_: pl.BlockSpec((1, width_), lambda i, h, j: (0, 0))
    comp = jnp.arange(2 * hd, dtype=jnp.int32) // hd
    avg = jnp.tile((comp[:, None] == comp[None, :]).astype(F32) / hd, (2, 1)).astype(BF16)
    nsub = ATT_Q_TILES_PER_STEP if (s // tq) % ATT_Q_TILES_PER_STEP == 0 else 1
    rows = nsub * tq
    kern = functools.partial(_attn_kernel, tq=tq, t=t, nsub=nsub, lambda_init=lambda_init)
    return pl.pallas_call(
        kern,
        out_shape=jax.ShapeDtypeStruct((b, s, width), BF16),
        grid=(b, heads, s // rows),
        in_specs=[
            pl.BlockSpec((1, rows, 2 * hd), lambda i, h, j: (i, j, h)),
            pl.BlockSpec((1, s, 2 * hd), lambda i, h, j: (i, 0, heads + h)),
            pl.BlockSpec((1, s, ATT_V_DIM), lambda i, h, j: (i, 0, 2 * heads + h)),
            small(2 * hd), small(2 * hd), small(hd), small(hd), small(hd), small(hd),
            pl.BlockSpec((ATT_V_DIM, 1), lambda i, h, j: (0, 0)),
            pl.BlockSpec((4 * hd, 2 * hd), lambda i, h, j: (0, 0)),
        ],
        out_specs=pl.BlockSpec((1, rows, ATT_V_DIM), lambda i, h, j: (i, j, h)),
        scratch_shapes=[
            pltpu.VMEM((s, 2 * hd), BF16),
            pltpu.VMEM((s // t, ATT_V_DIM + ATT_SUM_ROWS, t), BF16),
            pltpu.VMEM((nsub, 2, 1, tq), F32),
            pltpu.VMEM((nsub, 2, ATT_V_DIM + ATT_SUM_ROWS, tq), F32),
            pltpu.VMEM((nsub, 2, 2, t, tq), F32), pltpu.VMEM((nsub, 2, 2, t, tq), BF16),
            pltpu.VMEM((nsub, 2, 2, 1, tq), F32),
        ],
        compiler_params=_params(("arbitrary", "arbitrary", "arbitrary")),
        name="diff_attention",
    )(qkv, qkv, qkv, qg, kg, vec(lq1), vec(lk1), vec(lq2), vec(lk2),
      sub_norm.astype(F32).reshape(ATT_V_DIM, 1), avg)


def _route_kernel(x_ref, g_ref, r_ref, xn_ref, info_ref, cnt_ref, run_ref, *, n_experts):
    @pl.when(pl.program_id(0) == 0)
    def _():
        run_ref[...] = jnp.zeros_like(run_ref)

    xn = _rms_rows(x_ref[...], g_ref[...])
    xn_ref[...] = xn.astype(BF16)
    xh = xn.astype(BF16)
    xl = (xn - xh.astype(F32)).astype(BF16)
    hi = jnp.dot(xh, r_ref[...], preferred_element_type=F32)
    lo_ = jnp.dot(xl, r_ref[...], preferred_element_type=F32)
    logits = (hi[:, :LANES] + hi[:, LANES:]) + (lo_[:, :LANES] + lo_[:, LANES:])
    lane = lax.broadcasted_iota(jnp.int32, logits.shape, 1)
    logits = jnp.where(lane < n_experts, logits, -jnp.inf)
    m1 = jnp.max(logits, axis=-1, keepdims=True)
    i1 = jnp.min(jnp.where(logits == m1, lane, LANES), axis=-1, keepdims=True)
    rest = jnp.where(lane == i1, -jnp.inf, logits)
    m2 = jnp.max(rest, axis=-1, keepdims=True)
    i2 = jnp.min(jnp.where(rest == m2, lane, LANES), axis=-1, keepdims=True)
    e2 = jnp.exp(m2 - m1)
    g1 = 1.0 / (1.0 + e2)
    g2 = e2 * g1
    tm = logits.shape[0]
    oh1 = (lane == i1).astype(F32)
    oh2 = (lane == i2).astype(F32)
    both = oh1 + oh2
    row = lax.broadcasted_iota(jnp.int32, (tm, tm), 0)
    col = lax.broadcasted_iota(jnp.int32, (tm, tm), 1)
    before = jnp.dot((col < row).astype(BF16), both.astype(BF16),
                     preferred_element_type=F32) + run_ref[0:1, :]
    r1 = jnp.sum(before * oh1, axis=-1, keepdims=True)
    r2 = jnp.sum(before * oh2, axis=-1, keepdims=True)
    run_ref[...] = run_ref[...] + jnp.sum(both, axis=0, keepdims=True)
    cnt_ref[...] = run_ref[...]
    info = jnp.zeros_like(logits)
    for k, val in enumerate((i1.astype(F32), i2.astype(F32), g1, g2, r1, r2)):
        info = jnp.where(lane == k, val, info)
    info_ref[...] = info


def moe_route(h, gain, router):
    t, d = h.shape
    e = router.shape[1]
    tm = _tile(t, TM_ROUTE, SUBLANES)
    rpad = jnp.zeros((d, LANES), F32).at[:, :e].set(router.astype(F32))
    r_hi = rpad.astype(BF16)
    r_parts = jnp.concatenate([r_hi, (rpad - r_hi.astype(F32)).astype(BF16)], axis=1)
    return pl.pallas_call(
        functools.partial(_route_kernel, n_experts=e),
        out_shape=(jax.ShapeDtypeStruct((t, d), BF16), jax.ShapeDtypeStruct((t, LANES), F32),
                   jax.ShapeDtypeStruct((SUBLANES, LANES), F32)),
        grid=(t // tm,),
        in_specs=[
            pl.BlockSpec((tm, d), lambda i: (i, 0)),
            pl.BlockSpec((1, d), lambda i: (0, 0)),
            pl.BlockSpec((d, 2 * LANES), lambda i: (0, 0)),
        ],
        out_specs=(pl.BlockSpec((tm, d), lambda i: (i, 0)),
                   pl.BlockSpec((tm, LANES), lambda i: (i, 0)),
                   pl.BlockSpec((SUBLANES, LANES), lambda i: (0, 0))),
        scratch_shapes=[pltpu.VMEM((SUBLANES, LANES), F32)],
        compiler_params=_params(("arbitrary",)),
        name="moe_route",
    )(h, gain.reshape(1, d), r_parts)


def _new_weights(te_ref, i):
    prev = te_ref[jnp.maximum(i - 1, 0)]
    return jnp.logical_or(i == 0, te_ref[i] != prev)


def _moe_gateup_kernel(te_ref, nu_ref, x_ref, wg_ref, wu_ref, o_ref, wgb_ref, wub_ref):
    i = pl.program_id(1)

    @pl.when(_new_weights(te_ref, i))
    def _():
        wgb_ref[...] = wg_ref[0].astype(BF16)
        wub_ref[...] = wu_ref[0].astype(BF16)

    @pl.when(i < nu_ref[0])
    def _():
        x = x_ref[...]
        a = jnp.dot(x, wgb_ref[...], preferred_element_type=F32)
        b = jnp.dot(x, wub_ref[...], preferred_element_type=F32)
        o_ref[...] = (_silu(a) * b).astype(o_ref.dtype)

    @pl.when(i >= nu_ref[0])
    def _():
        o_ref[...] = jnp.zeros_like(o_ref)


def _moe_down_kernel(te_ref, nu_ref, h_ref, wd_ref, o_ref, wdb_ref):
    i = pl.program_id(1)

    @pl.when(_new_weights(te_ref, i))
    def _():
        wdb_ref[...] = wd_ref[0].astype(BF16)

    @pl.when(i < nu_ref[0])
    def _():
        o_ref[...] = jnp.dot(h_ref[...], wdb_ref[...],
                             preferred_element_type=F32).astype(o_ref.dtype)

    @pl.when(i >= nu_ref[0])
    def _():
        o_ref[...] = jnp.zeros_like(o_ref)


def moe_experts(xs, tile_expert, n_used, w_gate, w_up, w_down, tm):
    slots, d = xs.shape
    e, _, f = w_gate.shape
    n_tiles = slots // tm
    tf = _tile(f, TF_FFN)
    hmid = pl.pallas_call(
        _moe_gateup_kernel,
        out_shape=jax.ShapeDtypeStruct((slots, f), BF16),
        grid_spec=pltpu.PrefetchScalarGridSpec(
            num_scalar_prefetch=2,
            grid=(f // tf, n_tiles),
            in_specs=[
                pl.BlockSpec((tm, d), lambda j, i, te, nu: (i, 0)),
                pl.BlockSpec((1, d, tf), lambda j, i, te, nu: (te[i], 0, j)),
                pl.BlockSpec((1, d, tf), lambda j, i, te, nu: (te[i], 0, j)),
            ],
            out_specs=pl.BlockSpec((tm, tf), lambda j, i, te, nu: (i, j)),
            scratch_shapes=[pltpu.VMEM((d, tf), BF16), pltpu.VMEM((d, tf), BF16)],
        ),
        compiler_params=_params(("arbitrary", "arbitrary")),
        name="moe_gateup",
    )(tile_expert, n_used, xs, w_gate, w_up)

    tn = _tile(d, TN_MOE_DOWN)
    return pl.pallas_call(
        _moe_down_kernel,
        out_shape=jax.ShapeDtypeStruct((slots, d), BF16),
        grid_spec=pltpu.PrefetchScalarGridSpec(
            num_scalar_prefetch=2,
            grid=(d // tn, n_tiles),
            in_specs=[
                pl.BlockSpec((tm, f), lambda j, i, te, nu: (i, 0)),
                pl.BlockSpec((1, f, tn), lambda j, i, te, nu: (te[i], 0, j),
                             pipeline_mode=pl.Buffered(1)),
            ],
            out_specs=pl.BlockSpec((tm, tn), lambda j, i, te, nu: (i, j)),
            scratch_shapes=[pltpu.VMEM((f, tn), BF16)],
        ),
        compiler_params=_params(("arbitrary", "arbitrary")),
        name="moe_down",
    )(tile_expert, n_used, hmid, w_down)


def moe_block(h, gain, router, w_gate, w_up, w_down):
    t, d = h.shape
    e = router.shape[1]
    tm = min(TM_MOE, t)
    xn, info, cnt = moe_route(h, gain, router)
    idx = info[:, :TOP_K].astype(jnp.int32)
    gates = info[:, TOP_K:2 * TOP_K]
    rank = info[:, 2 * TOP_K:3 * TOP_K].astype(jnp.int32)
    counts = cnt[0, :e].astype(jnp.int32)

    tiles_per_e = (counts + tm - 1) // tm
    tile_end = jnp.cumsum(tiles_per_e)
    starts = (tile_end - tiles_per_e) * tm
    slot = (jnp.take(starts, idx.reshape(-1)) + rank.reshape(-1))
    n_tiles = (TOP_K * t) // tm + e
    n_slots = n_tiles * tm
    token_of_slot = jnp.zeros((n_slots,), jnp.int32).at[slot].set(
        jnp.arange(TOP_K * t, dtype=jnp.int32) // TOP_K)
    tile_ids = jnp.arange(n_tiles, dtype=jnp.int32)
    tile_expert = jnp.minimum(jnp.sum((tile_ids[:, None] >= tile_end[None, :]).astype(jnp.int32), axis=1),
                              e - 1)
    n_used = tile_end[-1:].astype(jnp.int32)

    xs = xn.at[token_of_slot].get(mode="promise_in_bounds")
    ys = moe_experts(xs, tile_expert, n_used, w_gate, w_up, w_down, tm)
    slot2 = slot.reshape(t, TOP_K)
    out = h
    for k in range(TOP_K):
        out = out + gates[:, k:k + 1] * ys.at[slot2[:, k]].get(mode="promise_in_bounds").astype(F32)
    return out


def mamba_layer(h, b, s, pre_norm, in_proj, conv_w, conv_b, dt_bias, a_log, d_skip, out_norm,
                out_proj):
    d_inner = out_proj.shape[0]
    d_xbc = conv_w.shape[1]
    heads = dt_bias.shape[0]
    z = norm_matmul(h, pre_norm, in_proj[:, :d_inner].astype(BF16), BF16)
    xbc = norm_matmul(h, pre_norm, in_proj[:, d_inner:d_inner + d_xbc].astype(BF16), BF16)
    w_dt = jnp.zeros((h.shape[1], LANES), BF16).at[:, :heads].set(
        in_proj[:, d_inner + d_xbc:].astype(BF16))
    dt = norm_matmul(h, pre_norm, w_dt, F32)[:, :heads]
    dtt = jnp.swapaxes(dt.reshape(b, s, heads), 1, 2)
    y = ssd_core(z.reshape(b, s, d_inner), xbc.reshape(b, s, d_xbc), dtt, conv_w.astype(F32),
                 conv_b.astype(F32), dt_bias.astype(F32), a_log.astype(F32), d_skip, out_norm)
    return matmul_residual(y.reshape(b * s, d_inner), out_proj.astype(BF16), h)


def dense_layer(h, pre_norm, w_gate, w_up, w_down):
    mid = norm_gateup(h, pre_norm, w_gate.astype(BF16), w_up.astype(BF16))
    return matmul_residual(mid, w_down.astype(BF16), h)


def attention_layer(h, b, s, pre_norm, w_qkv, q_norm, k_norm, lq1, lk1, lq2, lk2, sub_norm, w_o,
                    lambda_init):
    qkv = norm_matmul(h, pre_norm, w_qkv.astype(BF16), BF16)
    o = diff_attention_core(qkv.reshape(b, s, -1), q_norm, k_norm, lq1, lk1, lq2, lk2, sub_norm,
                            lambda_init)
    return matmul_residual(o.reshape(b * s, -1), w_o.astype(BF16), h)


def kernel(x, ssm_pre_norm, ssm_in_proj, ssm_conv_w, ssm_conv_b, ssm_dt_bias, ssm_A_log, ssm_D,
           ssm_out_norm, ssm_out_proj, dense_pre_norm, dense_w_gate, dense_w_up, dense_w_down,
           att_pre_norm, att_w_qkv, att_q_norm, att_k_norm, att_lambda_q1, att_lambda_k1,
           att_lambda_q2, att_lambda_k2, att_sub_norm, att_w_o, moe_pre_norm, moe_router,
           moe_w_gate, moe_w_up, moe_w_down):
    b, s, d = x.shape
    h = x.reshape(b * s, d)
    for i in range(DEPTH):
        j = i // N_MIXERS
        if i % N_MIXERS == 0:
            h = mamba_layer(h, b, s, ssm_pre_norm[j], ssm_in_proj[j], ssm_conv_w[j],
                            ssm_conv_b[j], ssm_dt_bias[j], ssm_A_log[j], ssm_D[j],
                            ssm_out_norm[j], ssm_out_proj[j])
            h = dense_layer(h, dense_pre_norm[j], dense_w_gate[j], dense_w_up[j],
                            dense_w_down[j])
        else:
            lambda_init = 0.8 - 0.6 * math.exp(-0.3 * i)
            h = attention_layer(h, b, s, att_pre_norm[j], att_w_qkv[j], att_q_norm[j],
                                att_k_norm[j], att_lambda_q1[j], att_lambda_k1[j],
                                att_lambda_q2[j], att_lambda_k2[j], att_sub_norm[j], att_w_o[j],
                                lambda_init)
            h = moe_block(h, moe_pre_norm[j], moe_router[j], moe_w_gate[j], moe_w_up[j],
                          moe_w_down[j])
    return h.reshape(b, s, d)
```

```python
import functools
import math

import jax
import jax.numpy as jnp
from jax import lax
from jax.experimental import pallas as pl
from jax.experimental.pallas import tpu as pltpu

F32 = jnp.float32
BF16 = jnp.bfloat16

EPS = 1e-5
DEPTH = 2
N_MIXERS = 2

SSM_GROUPS = 8
SSM_STATE = 128
CHUNK = 128
CONV_WIDTH = 4
SSM_HEAD_DIM = 64
CONV_TAIL = 16

ATT_HEAD_DIM = 64
ATT_V_DIM = 2 * ATT_HEAD_DIM
ATT_SUM_ROWS = 16

TOP_K = 2

LOG2E = math.log2(math.e)

LANES = 128
SUBLANES = 8
VMEM_LIMIT_BYTES = 56 * 1024 * 1024

TM_PROJ = 1024
TN_PROJ = 1024
TM_ROUTE = 512
RES_VMEM_BUDGET = 46 * 1024 * 1024
TF_FFN = 512
SSD_CHUNKS_PER_STEP = 8
TM_MOE = 512
TN_MOE_DOWN = 1024
TQ_ATT = 512
TK_ATT = 256
ATT_Q_TILES_PER_STEP = 8


def _tile(dim, pref, quantum=LANES):
    if dim <= pref:
        return dim
    t = (pref // quantum) * quantum
    while t >= quantum:
        if dim % t == 0:
            return t
        t -= quantum
    return dim


def _params(sem):
    return pltpu.CompilerParams(dimension_semantics=sem, vmem_limit_bytes=VMEM_LIMIT_BYTES)


def _rms_rows(x, gain):
    return x * lax.rsqrt(jnp.mean(x * x, axis=-1, keepdims=True) + EPS) * gain


def _silu(x):
    h = 0.5 * x
    return h + h * jnp.tanh(h)


def _split3(v):
    hi = v.astype(BF16).astype(F32)
    r = v - hi
    mid = r.astype(BF16).astype(F32)
    return [hi, mid, r - mid]


def _norm_mm_kernel(x_ref, g_ref, w_ref, o_ref, xn_ref):
    @pl.when(pl.program_id(1) == 0)
    def _():
        xn_ref[...] = _rms_rows(x_ref[...], g_ref[...]).astype(BF16)

    o_ref[...] = jnp.dot(xn_ref[...], w_ref[...], preferred_element_type=F32).astype(o_ref.dtype)


def norm_matmul(x, gain, w, out_dtype):
    t, d = x.shape
    n = w.shape[1]
    tm, tn = _tile(t, TM_PROJ, SUBLANES), _tile(n, TN_PROJ)
    return pl.pallas_call(
        _norm_mm_kernel,
        out_shape=jax.ShapeDtypeStruct((t, n), out_dtype),
        grid=(t // tm, n // tn),
        in_specs=[
            pl.BlockSpec((tm, d), lambda i, j: (i, 0)),
            pl.BlockSpec((1, d), lambda i, j: (0, 0)),
            pl.BlockSpec((d, tn), lambda i, j: (0, j)),
        ],
        out_specs=pl.BlockSpec((tm, tn), lambda i, j: (i, j)),
        scratch_shapes=[pltpu.VMEM((tm, d), BF16)],
        compiler_params=_params(("arbitrary", "arbitrary")),
        name="norm_matmul",
    )(x, gain.reshape(1, d), w)


def _norm_gateup_kernel(x_ref, g_ref, wg_ref, wu_ref, o_ref, xn_ref):
    @pl.when(pl.program_id(1) == 0)
    def _():
        xn_ref[...] = _rms_rows(x_ref[...], g_ref[...]).astype(BF16)

    xn = xn_ref[...]
    a = jnp.dot(xn, wg_ref[...], preferred_element_type=F32)
    b = jnp.dot(xn, wu_ref[...], preferred_element_type=F32)
    o_ref[...] = (_silu(a) * b).astype(o_ref.dtype)


def norm_gateup(x, gain, wg, wu):
    t, d = x.shape
    f = wg.shape[1]
    tm, tf = _tile(t, TM_PROJ, SUBLANES), _tile(f, TF_FFN)
    return pl.pallas_call(
        _norm_gateup_kernel,
        out_shape=jax.ShapeDtypeStruct((t, f), BF16),
        grid=(t // tm, f // tf),
        in_specs=[
            pl.BlockSpec((tm, d), lambda i, j: (i, 0)),
            pl.BlockSpec((1, d), lambda i, j: (0, 0)),
            pl.BlockSpec((d, tf), lambda i, j: (0, j)),
            pl.BlockSpec((d, tf), lambda i, j: (0, j)),
        ],
        out_specs=pl.BlockSpec((tm, tf), lambda i, j: (i, j)),
        scratch_shapes=[pltpu.VMEM((tm, d), BF16)],
        compiler_params=_params(("arbitrary", "arbitrary")),
        name="norm_gateup",
    )(x, gain.reshape(1, d), wg, wu)


def _mm_res_kernel(a_ref, w_ref, r_ref, o_ref):
    o_ref[...] = r_ref[...] + jnp.dot(a_ref[...], w_ref[...], preferred_element_type=F32)


def matmul_residual(a, w, res):
    t, k = a.shape
    n = w.shape[1]
    tm = t
    for cand in (1024, 512, 256, 128):
        need = k * n * 2 + 2 * cand * k * 2 + 4 * cand * n * 4 + cand * n * 4
        if t % cand == 0 and need <= RES_VMEM_BUDGET:
            tm = cand
            break
    return pl.pallas_call(
        _mm_res_kernel,
        out_shape=jax.ShapeDtypeStruct((t, n), F32),
        grid=(t // tm,),
        in_specs=[
            pl.BlockSpec((tm, k), lambda i: (i, 0)),
            pl.BlockSpec((k, n), lambda i: (0, 0), pipeline_mode=pl.Buffered(1)),
            pl.BlockSpec((tm, n), lambda i: (i, 0)),
        ],
        out_specs=pl.BlockSpec((tm, n), lambda i: (i, 0)),
        compiler_params=_params(("arbitrary",)),
        name="matmul_residual",
    )(a, w, res)


def _ssd_kernel(x_ref, bm_ref, cm_ref, z_ref, dtt_ref, cwx_ref, cbx_ref, cwb_ref, cbb_ref,
                cwc_ref, cbc_ref, dtb_ref, alog_ref, dx_ref, gn_ref, e_ref, sh_ref,
                o_ref, pad_ref, h_ref, *, hpg, gw, nc):
    L = CHUNK
    n = SSM_STATE
    hd = gw // hpg
    rows = nc * L
    c = pl.program_id(2)
    tail = CONV_TAIL

    @pl.when(c == 0)
    def _():
        pad_ref[0:tail, :] = jnp.zeros((tail, gw + 2 * n), BF16)
        h_ref[...] = jnp.zeros_like(h_ref)

    @pl.when(c > 0)
    def _():
        pad_ref[0:tail, :] = pad_ref[rows:rows + tail, :]

    pad_ref[tail:tail + rows, 0:gw] = x_ref[0]
    pad_ref[tail:tail + rows, gw:gw + n] = bm_ref[0]
    pad_ref[tail:tail + rows, gw + n:gw + 2 * n] = cm_ref[0]

    def conv(shifted, t0, col0, width, w_ref, b_ref):
        acc = b_ref[...] + w_ref[CONV_WIDTH - 1:CONV_WIDTH, :] * pad_ref[
            t0 + tail:t0 + tail + L, col0:col0 + width].astype(F32)
        for k in range(CONV_WIDTH - 1):
            acc = acc + w_ref[k:k + 1, :] * shifted[k * L:(k + 1) * L, col0:col0 + width]
        return _silu(acc)

    raw = dtt_ref[0] + dtb_ref[...]
    dt_all = jnp.maximum(raw, 0.0) + jnp.log1p(jnp.exp(-jnp.abs(raw)))
    a_all = dt_all * (-jnp.exp(alog_ref[...]))
    row = lax.broadcasted_iota(jnp.int32, (L, L), 0)
    col = lax.broadcasted_iota(jnp.int32, (L, L), 1)
    upper = (row <= col).astype(F32)
    causal = row >= col
    lane = lax.broadcasted_iota(jnp.int32, (1, 2 * hd), 1)
    lo = lane < hd

    for ci in range(nc):
        t0 = ci * L
        shifted = jnp.dot(sh_ref[...], pad_ref[t0:t0 + tail + L, :],
                          preferred_element_type=F32)
        xs = conv(shifted, t0, 0, gw, cwx_ref, cbx_ref)
        bc = conv(shifted, t0, gw, n, cwb_ref, cbb_ref)
        cc = conv(shifted, t0, gw + n, n, cwc_ref, cbc_ref)

        dt = dt_all[:, t0:t0 + L]
        acs = jnp.dot(a_all[:, t0:t0 + L], upper, preferred_element_type=F32,
                      precision=lax.Precision.HIGHEST)
        a_last = acs[:, L - 1:L]
        dte = jnp.exp(a_last - acs)
        eacs = jnp.exp(acs)

        parts = _split3(acs) + _split3(dt) + _split3(dt * dte) + _split3(eacs)
        stacked = jnp.concatenate(parts, axis=0).T.astype(BF16)
        spread = jnp.dot(stacked, e_ref[...], preferred_element_type=F32)
        acs_col = spread[:, :hpg * L]
        dt_x = spread[:, hpg * L:hpg * L + gw]
        w_x = spread[:, hpg * L + gw:hpg * L + 2 * gw]
        eacs_x = spread[:, hpg * L + 2 * gw:]

        xdt = xs * dt_x
        bct = bc.T.astype(BF16)
        ccb = cc.astype(BF16)
        cb = jnp.dot(ccb, bct, preferred_element_type=F32)

        y_parts = []
        for q in range(hpg // 2):
            ms = []
            for j in (2 * q, 2 * q + 1):
                seg = acs_col[:, j * L:(j + 1) * L] - acs[j:j + 1, :]
                dec = jnp.exp(jnp.where(causal, seg, -jnp.inf))
                ms.append((cb * dec).astype(BF16))
            mcat = jnp.concatenate(ms, axis=1)
            xq = xdt[:, q * 2 * hd:(q + 1) * 2 * hd]
            rhs = jnp.concatenate([jnp.where(lo, xq, 0.0), jnp.where(lo, 0.0, xq)], axis=0)
            y_parts.append(jnp.dot(mcat, rhs.astype(BF16), preferred_element_type=F32))
        y_diag = jnp.concatenate(y_parts, axis=1)

        h_in = h_ref[...]
        y_off = jnp.dot(ccb, h_in.astype(BF16), preferred_element_type=F32) * eacs_x
        states = jnp.dot(bct, (xs * w_x).astype(BF16), preferred_element_type=F32)
        h_ref[...] = h_in * eacs_x[L - 1:L, :] + states

        y = y_diag + y_off + dx_ref[...] * xs
        gated = y * _silu(z_ref[0, t0:t0 + L, :].astype(F32))
        o_ref[0, t0:t0 + L, :] = _rms_rows(gated, gn_ref[...]).astype(o_ref.dtype)


def ssd_core(z, xbc, dtt, conv_w, conv_b, dt_bias, a_log, d_skip, out_norm):
    b, s, d_inner = z.shape
    g = SSM_GROUPS
    n = SSM_STATE
    heads = dt_bias.shape[0]
    hpg = heads // g
    gw = d_inner // g
    hd = gw // hpg
    L = CHUNK
    nb = d_inner // n
    assert gw % LANES == 0 and hpg % 2 == 0 and 2 * hd == LANES and s % L == 0

    eye = jnp.eye(hpg, dtype=F32)
    widths = [hpg * L, gw, gw, gw]
    mats = [jnp.repeat(eye, L, axis=1)] + [jnp.repeat(eye, hd, axis=1)] * 3
    spread_rows = []
    for qty in range(4):
        blk = jnp.concatenate([mats[qty] if c == qty else jnp.zeros((hpg, widths[c]), F32)
                               for c in range(4)], axis=1)
        spread_rows += [blk] * 3
    spread_mat = jnp.concatenate(spread_rows, axis=0).astype(BF16)
    dx = jnp.repeat(d_skip.astype(F32), hd).reshape(1, d_inner)

    nc = SSD_CHUNKS_PER_STEP if (s // L) % SSD_CHUNKS_PER_STEP == 0 else 1
    rows = nc * L
    out_row = jnp.arange((CONV_WIDTH - 1) * L, dtype=jnp.int32)
    src_row = out_row % L + CONV_TAIL - (CONV_WIDTH - 1) + out_row // L
    shift_mat = (src_row[:, None] == jnp.arange(CONV_TAIL + L, dtype=jnp.int32)[None, :]).astype(BF16)
    kern = functools.partial(_ssd_kernel, hpg=hpg, gw=gw, nc=nc)
    return pl.pallas_call(
        kern,
        out_shape=jax.ShapeDtypeStruct((b, s, d_inner), BF16),
        grid=(b, g, s // rows),
        in_specs=[
            pl.BlockSpec((1, rows, gw), lambda i, j, c: (i, c, j)),
            pl.BlockSpec((1, rows, n), lambda i, j, c: (i, c, nb + j)),
            pl.BlockSpec((1, rows, n), lambda i, j, c: (i, c, nb + g + j)),
            pl.BlockSpec((1, rows, gw), lambda i, j, c: (i, c, j)),
            pl.BlockSpec((1, hpg, rows), lambda i, j, c: (i, j, c)),
            pl.BlockSpec((CONV_WIDTH, gw), lambda i, j, c: (0, j)),
            pl.BlockSpec((1, gw), lambda i, j, c: (0, j)),
            pl.BlockSpec((CONV_WIDTH, n), lambda i, j, c: (0, nb + j)),
            pl.BlockSpec((1, n), lambda i, j, c: (0, nb + j)),
            pl.BlockSpec((CONV_WIDTH, n), lambda i, j, c: (0, nb + g + j)),
            pl.BlockSpec((1, n), lambda i, j, c: (0, nb + g + j)),
            pl.BlockSpec((hpg, 1), lambda i, j, c: (j, 0)),
            pl.BlockSpec((hpg, 1), lambda i, j, c: (j, 0)),
            pl.BlockSpec((1, gw), lambda i, j, c: (0, j)),
            pl.BlockSpec((1, gw), lambda i, j, c: (0, j)),
            pl.BlockSpec((12 * hpg, hpg * L + 3 * gw), lambda i, j, c: (0, 0)),
            pl.BlockSpec(((CONV_WIDTH - 1) * L, CONV_TAIL + L), lambda i, j, c: (0, 0)),
        ],
        out_specs=pl.BlockSpec((1, rows, gw), lambda i, j, c: (i, c, j)),
        scratch_shapes=[pltpu.VMEM((rows + CONV_TAIL, gw + 2 * n), BF16), pltpu.VMEM((n, gw), F32)],
        compiler_params=_params(("arbitrary", "arbitrary", "arbitrary")),
        name="ssd_core",
    )(xbc, xbc, xbc, z, dtt, conv_w, conv_b.reshape(1, -1), conv_w, conv_b.reshape(1, -1),
      conv_w, conv_b.reshape(1, -1), dt_bias.reshape(heads, 1), a_log.reshape(heads, 1),
      dx, out_norm.reshape(1, d_inner), spread_mat, shift_mat)


def _attn_kernel(q_ref, k_ref, v_ref, qg_ref, kg_ref, lq1_ref, lk1_ref, lq2_ref, lk2_ref,
                 sg_ref, avg_ref, o_ref, kn_ref, vt_ref, m_ref, acc_ref, s_ref, p_ref, a_ref,
                 *, tq, t, nsub, lambda_init):
    s_len = k_ref.shape[1]
    n_tiles = s_len // t
    hd = ATT_HEAD_DIM
    step = pl.program_id(2)
    lane = lax.broadcasted_iota(jnp.int32, (1, 2 * hd), 1)
    lo = lane < hd

    def comp_norm(x, gain):
        x2 = x * x
        hi = x2.astype(BF16)
        lo_part = (x2 - hi.astype(F32)).astype(BF16)
        ms = jnp.dot(jnp.concatenate([hi, lo_part], axis=1), avg_ref[...],
                     preferred_element_type=F32)
        return x * lax.rsqrt(ms + EPS) * gain

    @pl.when(step == 0)
    def _():
        def body(i, carry):
            r0 = pl.multiple_of(i * t, t)
            kn = comp_norm(k_ref[0, pl.ds(r0, t), :].astype(F32), kg_ref[...])
            kn_ref[pl.ds(r0, t), :] = kn.astype(BF16)
            vt_ref[i, 0:ATT_V_DIM, :] = v_ref[0, pl.ds(r0, t), :].astype(F32).T.astype(BF16)
            vt_ref[i, ATT_V_DIM:, :] = jnp.ones((ATT_SUM_ROWS, t), BF16)
            return carry

        lax.fori_loop(0, s_len // t, body, 0)

    lam = (jnp.exp(jnp.sum(lq1_ref[...] * lk1_ref[...], axis=-1, keepdims=True))
           - jnp.exp(jnp.sum(lq2_ref[...] * lk2_ref[...], axis=-1, keepdims=True)) + lambda_init)

    whole_head = nsub * tq == s_len
    for sub in range(nsub):
        _attn_query_tile(sub if whole_head else step * nsub + sub, q_ref, sub * tq, o_ref, qg_ref, sg_ref, kn_ref, vt_ref,
                         m_ref.at[sub], acc_ref.at[sub], s_ref.at[sub], p_ref.at[sub],
                         a_ref.at[sub], comp_norm, lo, lam, tq=tq, t=t, n_tiles=n_tiles,
                         lambda_init=lambda_init)


def _attn_query_tile(qi, q_ref, q0, o_ref, qg_ref, sg_ref, kn_ref, vt_ref, m_ref, acc_ref, s_ref,
                     p_ref, a_ref, comp_norm, lo, lam, *, tq, t, n_tiles, lambda_init):
    hd = ATT_HEAD_DIM
    qn = comp_norm(q_ref[0, q0:q0 + tq, :].astype(F32), qg_ref[...]) * (hd ** -0.5 * LOG2E)
    qc = (jnp.where(lo, qn, 0.0).astype(BF16), jnp.where(lo, 0.0, qn).astype(BF16))

    m_ref[...] = jnp.full(m_ref.shape, -jnp.inf, F32)
    acc_ref[...] = jnp.zeros(acc_ref.shape, F32)

    def scores(tile, slot):
        r0 = pl.multiple_of(jnp.minimum(tile, n_tiles - 1) * t, t)
        kk = kn_ref[pl.ds(r0, t), :]
        for c in range(2):
            s_ref[slot, c] = lax.dot_general(kk, qc[c], (((1,), (1,)), ((), ())),
                                             preferred_element_type=F32)

    def softmax(tile, slot, masked):
        for c in range(2):
            sc = s_ref[slot, c]
            if masked:
                key = tile * t + lax.broadcasted_iota(jnp.int32, (t, tq), 0)
                qry = qi * tq + lax.broadcasted_iota(jnp.int32, (t, tq), 1)
                sc = jnp.where(key <= qry, sc, -jnp.inf)
            m_prev = m_ref[c]
            m_new = jnp.maximum(m_prev, jnp.max(sc, axis=0, keepdims=True))
            alpha = jnp.exp2(m_prev - m_new)
            p = jnp.exp2(sc - m_new)
            m_ref[c] = m_new
            a_ref[slot, c] = alpha
            p_ref[slot, c] = p.astype(BF16)

    def update(tile0, tile1):
        vt0 = vt_ref[jnp.minimum(tile0, n_tiles - 1)]
        vt1 = vt_ref[jnp.minimum(tile1, n_tiles - 1)]
        for c in range(2):
            a1 = a_ref[1, c]
            pv0 = jnp.dot(vt0, p_ref[0, c], preferred_element_type=F32)
            pv1 = jnp.dot(vt1, p_ref[1, c], preferred_element_type=F32)
            acc_ref[c] = (a_ref[0, c] * a1) * acc_ref[c] + a1 * pv0 + pv1

    d0 = 2 * ((qi * tq) // (2 * t))
    n_pairs = d0 // 2 + 1
    scores(d0, 0)
    scores(d0 + 1, 1)
    static = isinstance(qi, int)
    softmax(d0, 0, True)
    if not (static and n_pairs == 1):
        scores(0, 0)
    softmax(d0 + 1, 1, True)

    def pair(u, carry):
        x = 2 * u - 2
        first = u == 1
        update(jnp.where(first, d0, x - 2), jnp.where(first, d0 + 1, x - 1))
        scores(x + 1, 1)
        softmax(x, 0, False)
        if not (static and u == n_pairs - 1):
            scores(x + 2, 0)
        softmax(x + 1, 1, False)
        return carry

    if static:
        for u in range(1, n_pairs):
            pair(u, 0)
    else:
        lax.fori_loop(1, n_pairs, pair, 0)
    only = n_pairs == 1
    update(jnp.where(only, d0, d0 - 2), jnp.where(only, d0 + 1, d0 - 1))

    e = ATT_V_DIM
    o = (acc_ref[0, 0:e, :] / acc_ref[0, e:e + 1, :]
         - lam * (acc_ref[1, 0:e, :] / acc_ref[1, e:e + 1, :]))
    o = o * lax.rsqrt(jnp.mean(o * o, axis=0, keepdims=True) + EPS) * sg_ref[...]
    o_ref[0, q0:q0 + tq, :] = (o * (1.0 - lambda_init)).T.astype(o_ref.dtype)


def diff_attention_core(qkv, q_norm, k_norm, lq1, lk1, lq2, lk2, sub_norm, lambda_init):
    b, s, w3 = qkv.shape
    width = w3 // 3
    heads = width // ATT_V_DIM
    hd = ATT_HEAD_DIM
    tq = _tile(s, TQ_ATT, LANES)
    t = _tile(s, TK_ATT, LANES)
    assert tq % t == 0 and (2 * t) % tq == 0
    vec = lambda a: a.astype(F32).reshape(1, -1)
    qg = jnp.tile(q_norm.astype(F32), 2).reshape(1, 2 * hd)
    kg = jnp.tile(k_norm.astype(F32), 2).reshape(1, 2 * hd)
    small = lambda width_: pl.BlockSpec((1, width_), lambda i, h, j: (0, 0))
    comp = jnp.arange(2 * hd, dtype=jnp.int32) // hd
    avg = jnp.tile((comp[:, None] == comp[None, :]).astype(F32) / hd, (2, 1)).astype(BF16)
    nsub = ATT_Q_TILES_PER_STEP if (s // tq) % ATT_Q_TILES_PER_STEP == 0 else 1
    rows = nsub * tq
    kern = functools.partial(_attn_kernel, tq=tq, t=t, nsub=nsub, lambda_init=lambda_init)
    return pl.pallas_call(
        kern,
        out_shape=jax.ShapeDtypeStruct((b, s, width), BF16),
        grid=(b, heads, s // rows),
        in_specs=[
            pl.BlockSpec((1, rows, 2 * hd), lambda i, h, j: (i, j, h)),
            pl.BlockSpec((1, s, 2 * hd), lambda i, h, j: (i, 0, heads + h)),
            pl.BlockSpec((1, s, ATT_V_DIM), lambda i, h, j: (i, 0, 2 * heads + h)),
            small(2 * hd), small(2 * hd), small(hd), small(hd), small(hd), small(hd),
            pl.BlockSpec((ATT_V_DIM, 1), lambda i, h, j: (0, 0)),
            pl.BlockSpec((4 * hd, 2 * hd), lambda i, h, j: (0, 0)),
        ],
        out_specs=pl.BlockSpec((1, rows, ATT_V_DIM), lambda i, h, j: (i, j, h)),
        scratch_shapes=[
            pltpu.VMEM((s, 2 * hd), BF16),
            pltpu.VMEM((s // t, ATT_V_DIM + ATT_SUM_ROWS, t), BF16),
            pltpu.VMEM((nsub, 2, 1, tq), F32),
            pltpu.VMEM((nsub, 2, ATT_V_DIM + ATT_SUM_ROWS, tq), F32),
            pltpu.VMEM((nsub, 2, 2, t, tq), F32), pltpu.VMEM((nsub, 2, 2, t, tq), BF16),
            pltpu.VMEM((nsub, 2, 2, 1, tq), F32),
        ],
        compiler_params=_params(("arbitrary", "arbitrary", "arbitrary")),
        name="diff_attention",
    )(qkv, qkv, qkv, qg, kg, vec(lq1), vec(lk1), vec(lq2), vec(lk2),
      sub_norm.astype(F32).reshape(ATT_V_DIM, 1), avg)


def _route_kernel(x_ref, g_ref, r_ref, xn_ref, info_ref, cnt_ref, run_ref, *, n_experts):
    @pl.when(pl.program_id(0) == 0)
    def _():
        run_ref[...] = jnp.zeros_like(run_ref)

    xn = _rms_rows(x_ref[...], g_ref[...])
    xn_ref[...] = xn.astype(BF16)
    xh = xn.astype(BF16)
    xl = (xn - xh.astype(F32)).astype(BF16)
    hi = jnp.dot(xh, r_ref[...], preferred_element_type=F32)
    lo_ = jnp.dot(xl, r_ref[...], preferred_element_type=F32)
    logits = (hi[:, :LANES] + hi[:, LANES:]) + (lo_[:, :LANES] + lo_[:, LANES:])
    lane = lax.broadcasted_iota(jnp.int32, logits.shape, 1)
    logits = jnp.where(lane < n_experts, logits, -jnp.inf)
    m1 = jnp.max(logits, axis=-1, keepdims=True)
    i1 = jnp.min(jnp.where(logits == m1, lane, LANES), axis=-1, keepdims=True)
    rest = jnp.where(lane == i1, -jnp.inf, logits)
    m2 = jnp.max(rest, axis=-1, keepdims=True)
    i2 = jnp.min(jnp.where(rest == m2, lane, LANES), axis=-1, keepdims=True)
    e2 = jnp.exp(m2 - m1)
    g1 = 1.0 / (1.0 + e2)
    g2 = e2 * g1
    tm = logits.shape[0]
    oh1 = (lane == i1).astype(F32)
    oh2 = (lane == i2).astype(F32)
    both = oh1 + oh2
    row = lax.broadcasted_iota(jnp.int32, (tm, tm), 0)
    col = lax.broadcasted_iota(jnp.int32, (tm, tm), 1)
    before = jnp.dot((col < row).astype(BF16), both.astype(BF16),
                     preferred_element_type=F32) + run_ref[0:1, :]
    r1 = jnp.sum(before * oh1, axis=-1, keepdims=True)
    r2 = jnp.sum(before * oh2, axis=-1, keepdims=True)
    run_ref[...] = run_ref[...] + jnp.sum(both, axis=0, keepdims=True)
    cnt_ref[...] = run_ref[...]
    info = jnp.zeros_like(logits)
    for k, val in enumerate((i1.astype(F32), i2.astype(F32), g1, g2, r1, r2)):
        info = jnp.where(lane == k, val, info)
    info_ref[...] = info


def moe_route(h, gain, router):
    t, d = h.shape
    e = router.shape[1]
    tm = _tile(t, TM_ROUTE, SUBLANES)
    rpad = jnp.zeros((d, LANES), F32).at[:, :e].set(router.astype(F32))
    r_hi = rpad.astype(BF16)
    r_parts = jnp.concatenate([r_hi, (rpad - r_hi.astype(F32)).astype(BF16)], axis=1)
    return pl.pallas_call(
        functools.partial(_route_kernel, n_experts=e),
        out_shape=(jax.ShapeDtypeStruct((t, d), BF16), jax.ShapeDtypeStruct((t, LANES), F32),
                   jax.ShapeDtypeStruct((SUBLANES, LANES), F32)),
        grid=(t // tm,),
        in_specs=[
            pl.BlockSpec((tm, d), lambda i: (i, 0)),
            pl.BlockSpec((1, d), lambda i: (0, 0)),
            pl.BlockSpec((d, 2 * LANES), lambda i: (0, 0)),
        ],
        out_specs=(pl.BlockSpec((tm, d), lambda i: (i, 0)),
                   pl.BlockSpec((tm, LANES), lambda i: (i, 0)),
                   pl.BlockSpec((SUBLANES, LANES), lambda i: (0, 0))),
        scratch_shapes=[pltpu.VMEM((SUBLANES, LANES), F32)],
        compiler_params=_params(("arbitrary",)),
        name="moe_route",
    )(h, gain.reshape(1, d), r_parts)


def _new_weights(te_ref, i):
    prev = te_ref[jnp.maximum(i - 1, 0)]
    return jnp.logical_or(i == 0, te_ref[i] != prev)


def _moe_gateup_kernel(te_ref, nu_ref, x_ref, wg_ref, wu_ref, o_ref, wgb_ref, wub_ref):
    i = pl.program_id(1)

    @pl.when(_new_weights(te_ref, i))
    def _():
        wgb_ref[...] = wg_ref[0].astype(BF16)
        wub_ref[...] = wu_ref[0].astype(BF16)

    @pl.when(i < nu_ref[0])
    def _():
        x = x_ref[...]
        a = jnp.dot(x, wgb_ref[...], preferred_element_type=F32)
        b = jnp.dot(x, wub_ref[...], preferred_element_type=F32)
        o_ref[...] = (_silu(a) * b).astype(o_ref.dtype)

    @pl.when(i >= nu_ref[0])
    def _():
        o_ref[...] = jnp.zeros_like(o_ref)


def _moe_down_kernel(te_ref, nu_ref, h_ref, wd_ref, o_ref, wdb_ref):
    i = pl.program_id(1)

    @pl.when(_new_weights(te_ref, i))
    def _():
        wdb_ref[...] = wd_ref[0].astype(BF16)

    @pl.when(i < nu_ref[0])
    def _():
        o_ref[...] = jnp.dot(h_ref[...], wdb_ref[...],
                             preferred_element_type=F32).astype(o_ref.dtype)

    @pl.when(i >= nu_ref[0])
    def _():
        o_ref[...] = jnp.zeros_like(o_ref)


def moe_experts(xs, tile_expert, n_used, w_gate, w_up, w_down, tm):
    slots, d = xs.shape
    e, _, f = w_gate.shape
    n_tiles = slots // tm
    tf = _tile(f, TF_FFN)
    hmid = pl.pallas_call(
        _moe_gateup_kernel,
        out_shape=jax.ShapeDtypeStruct((slots, f), BF16),
        grid_spec=pltpu.PrefetchScalarGridSpec(
            num_scalar_prefetch=2,
            grid=(f // tf, n_tiles),
            in_specs=[
                pl.BlockSpec((tm, d), lambda j, i, te, nu: (i, 0)),
                pl.BlockSpec((1, d, tf), lambda j, i, te, nu: (te[i], 0, j)),
                pl.BlockSpec((1, d, tf), lambda j, i, te, nu: (te[i], 0, j)),
            ],
            out_specs=pl.BlockSpec((tm, tf), lambda j, i, te, nu: (i, j)),
            scratch_shapes=[pltpu.VMEM((d, tf), BF16), pltpu.VMEM((d, tf), BF16)],
        ),
        compiler_params=_params(("arbitrary", "arbitrary")),
        name="moe_gateup",
    )(tile_expert, n_used, xs, w_gate, w_up)

    tn = _tile(d, TN_MOE_DOWN)
    return pl.pallas_call(
        _moe_down_kernel,
        out_shape=jax.ShapeDtypeStruct((slots, d), BF16),
        grid_spec=pltpu.PrefetchScalarGridSpec(
            num_scalar_prefetch=2,
            grid=(d // tn, n_tiles),
            in_specs=[
                pl.BlockSpec((tm, f), lambda j, i, te, nu: (i, 0)),
                pl.BlockSpec((1, f, tn), lambda j, i, te, nu: (te[i], 0, j),
                             pipeline_mode=pl.Buffered(1)),
            ],
            out_specs=pl.BlockSpec((tm, tn), lambda j, i, te, nu: (i, j)),
            scratch_shapes=[pltpu.VMEM((f, tn), BF16)],
        ),
        compiler_params=_params(("arbitrary", "arbitrary")),
        name="moe_down",
    )(tile_expert, n_used, hmid, w_down)


def moe_block(h, gain, router, w_gate, w_up, w_down):
    t, d = h.shape
    e = router.shape[1]
    tm = min(TM_MOE, t)
    xn, info, cnt = moe_route(h, gain, router)
    idx = info[:, :TOP_K].astype(jnp.int32)
    gates = info[:, TOP_K:2 * TOP_K]
    rank = info[:, 2 * TOP_K:3 * TOP_K].astype(jnp.int32)
    counts = cnt[0, :e].astype(jnp.int32)

    tiles_per_e = (counts + tm - 1) // tm
    tile_end = jnp.cumsum(tiles_per_e)
    starts = (tile_end - tiles_per_e) * tm
    slot = (jnp.take(starts, idx.reshape(-1)) + rank.reshape(-1))
    n_tiles = (TOP_K * t) // tm + e
    n_slots = n_tiles * tm
    token_of_slot = jnp.zeros((n_slots,), jnp.int32).at[slot].set(
        jnp.arange(TOP_K * t, dtype=jnp.int32) // TOP_K)
    tile_ids = jnp.arange(n_tiles, dtype=jnp.int32)
    tile_expert = jnp.minimum(jnp.sum((tile_ids[:, None] >= tile_end[None, :]).astype(jnp.int32), axis=1),
                              e - 1)
    n_used = tile_end[-1:].astype(jnp.int32)

    xs = xn.at[token_of_slot].get(mode="promise_in_bounds")
    ys = moe_experts(xs, tile_expert, n_used, w_gate, w_up, w_down, tm)
    slot2 = slot.reshape(t, TOP_K)
    out = h
    for k in range(TOP_K):
        out = out + gates[:, k:k + 1] * ys.at[slot2[:, k]].get(mode="promise_in_bounds").astype(F32)
    return out


def mamba_layer(h, b, s, pre_norm, in_proj, conv_w, conv_b, dt_bias, a_log, d_skip, out_norm,
                out_proj):
    d_inner = out_proj.shape[0]
    d_xbc = conv_w.shape[1]
    heads = dt_bias.shape[0]
    z = norm_matmul(h, pre_norm, in_proj[:, :d_inner].astype(BF16), BF16)
    xbc = norm_matmul(h, pre_norm, in_proj[:, d_inner:d_inner + d_xbc].astype(BF16), BF16)
    w_dt = jnp.zeros((h.shape[1], LANES), BF16).at[:, :heads].set(
        in_proj[:, d_inner + d_xbc:].astype(BF16))
    dt = norm_matmul(h, pre_norm, w_dt, F32)[:, :heads]
    dtt = jnp.swapaxes(dt.reshape(b, s, heads), 1, 2)
    y = ssd_core(z.reshape(b, s, d_inner), xbc.reshape(b, s, d_xbc), dtt, conv_w.astype(F32),
                 conv_b.astype(F32), dt_bias.astype(F32), a_log.astype(F32), d_skip, out_norm)
    return matmul_residual(y.reshape(b * s, d_inner), out_proj.astype(BF16), h)


def dense_layer(h, pre_norm, w_gate, w_up, w_down):
    mid = norm_gateup(h, pre_norm, w_gate.astype(BF16), w_up.astype(BF16))
    return matmul_residual(mid, w_down.astype(BF16), h)


def attention_layer(h, b, s, pre_norm, w_qkv, q_norm, k_norm, lq1, lk1, lq2, lk2, sub_norm, w_o,
                    lambda_init):
    qkv = norm_matmul(h, pre_norm, w_qkv.astype(BF16), BF16)
    o = diff_attention_core(qkv.reshape(b, s, -1), q_norm, k_norm, lq1, lk1, lq2, lk2, sub_norm,
                            lambda_init)
    return matmul_residual(o.reshape(b * s, -1), w_o.astype(BF16), h)


def kernel(x, ssm_pre_norm, ssm_in_proj, ssm_conv_w, ssm_conv_b, ssm_dt_bias, ssm_A_log, ssm_D,
           ssm_out_norm, ssm_out_proj, dense_pre_norm, dense_w_gate, dense_w_up, dense_w_down,
           att_pre_norm, att_w_qkv, att_q_norm, att_k_norm, att_lambda_q1, att_lambda_k1,
           att_lambda_q2, att_lambda_k2, att_sub_norm, att_w_o, moe_pre_norm, moe_router,
           moe_w_gate, moe_w_up, moe_w_down):
    b, s, d = x.shape
    h = x.reshape(b * s, d)
    for i in range(DEPTH):
        j = i // N_MIXERS
        if i % N_MIXERS == 0:
            h = mamba_layer(h, b, s, ssm_pre_norm[j], ssm_in_proj[j], ssm_conv_w[j],
                            ssm_conv_b[j], ssm_dt_bias[j], ssm_A_log[j], ssm_D[j],
                            ssm_out_norm[j], ssm_out_proj[j])
            h = dense_layer(h, dense_pre_norm[j], dense_w_gate[j], dense_w_up[j],
                            dense_w_down[j])
        else:
            lambda_init = 0.8 - 0.6 * math.exp(-0.3 * i)
            h = attention_layer(h, b, s, att_pre_norm[j], att_w_qkv[j], att_q_norm[j],
                                att_k_norm[j], att_lambda_q1[j], att_lambda_k1[j],
                                att_lambda_q2[j], att_lambda_k2[j], att_sub_norm[j], att_w_o[j],
                                lambda_init)
            h = moe_block(h, moe_pre_norm[j], moe_router[j], moe_w_gate[j], moe_w_up[j],
                          moe_w_down[j])
    return h.reshape(b, s, d)
```

```python
import functools
import math

import jax
import jax.numpy as jnp
from jax import lax
from jax.experimental import pallas as pl
from jax.experimental.pallas import tpu as pltpu

F32 = jnp.float32
BF16 = jnp.bfloat16

EPS = 1e-5
DEPTH = 2
N_MIXERS = 2

SSM_GROUPS = 8
SSM_STATE = 128
CHUNK = 128
CONV_WIDTH = 4
SSM_HEAD_DIM = 64
CONV_TAIL = 16

ATT_HEAD_DIM = 64
ATT_V_DIM = 2 * ATT_HEAD_DIM
ATT_SUM_ROWS = 16

TOP_K = 2

LOG2E = math.log2(math.e)

LANES = 128
SUBLANES = 8
VMEM_LIMIT_BYTES = 56 * 1024 * 1024

TM_PROJ = 1024
TN_PROJ = 1024
TM_ROUTE = 512
RES_VMEM_BUDGET = 46 * 1024 * 1024
TF_FFN = 512
SSD_CHUNKS_PER_STEP = 16
TM_MOE = 512
TN_MOE_DOWN = 1024
TQ_ATT = 512
TK_ATT = 256
ATT_Q_TILES_PER_STEP = 8


def _tile(dim, pref, quantum=LANES):
    if dim <= pref:
        return dim
    t = (pref // quantum) * quantum
    while t >= quantum:
        if dim % t == 0:
            return t
        t -= quantum
    return dim


def _params(sem):
    return pltpu.CompilerParams(dimension_semantics=sem, vmem_limit_bytes=VMEM_LIMIT_BYTES)


def _rms_rows(x, gain):
    return x * lax.rsqrt(jnp.mean(x * x, axis=-1, keepdims=True) + EPS) * gain


def _silu(x):
    h = 0.5 * x
    return h + h * jnp.tanh(h)


def _split3(v):
    hi = v.astype(BF16).astype(F32)
    r = v - hi
    mid = r.astype(BF16).astype(F32)
    return [hi, mid, r - mid]


def _norm_mm_kernel(x_ref, g_ref, w_ref, o_ref, xn_ref):
    @pl.when(pl.program_id(1) == 0)
    def _():
        xn_ref[...] = _rms_rows(x_ref[...], g_ref[...]).astype(BF16)

    o_ref[...] = jnp.dot(xn_ref[...], w_ref[...], preferred_element_type=F32).astype(o_ref.dtype)


def norm_matmul(x, gain, w, out_dtype):
    t, d = x.shape
    n = w.shape[1]
    tm, tn = _tile(t, TM_PROJ, SUBLANES), _tile(n, TN_PROJ)
    return pl.pallas_call(
        _norm_mm_kernel,
        out_shape=jax.ShapeDtypeStruct((t, n), out_dtype),
        grid=(t // tm, n // tn),
        in_specs=[
            pl.BlockSpec((tm, d), lambda i, j: (i, 0)),
            pl.BlockSpec((1, d), lambda i, j: (0, 0)),
            pl.BlockSpec((d, tn), lambda i, j: (0, j)),
        ],
        out_specs=pl.BlockSpec((tm, tn), lambda i, j: (i, j)),
        scratch_shapes=[pltpu.VMEM((tm, d), BF16)],
        compiler_params=_params(("arbitrary", "arbitrary")),
        name="norm_matmul",
    )(x, gain.reshape(1, d), w)


def _norm_gateup_kernel(x_ref, g_ref, wg_ref, wu_ref, o_ref, xn_ref):
    @pl.when(pl.program_id(1) == 0)
    def _():
        xn_ref[...] = _rms_rows(x_ref[...], g_ref[...]).astype(BF16)

    xn = xn_ref[...]
    a = jnp.dot(xn, wg_ref[...], preferred_element_type=F32)
    b = jnp.dot(xn, wu_ref[...], preferred_element_type=F32)
    o_ref[...] = (_silu(a) * b).astype(o_ref.dtype)


def norm_gateup(x, gain, wg, wu):
    t, d = x.shape
    f = wg.shape[1]
    tm, tf = _tile(t, TM_PROJ, SUBLANES), _tile(f, TF_FFN)
    return pl.pallas_call(
        _norm_gateup_kernel,
        out_shape=jax.ShapeDtypeStruct((t, f), BF16),
        grid=(t // tm, f // tf),
        in_specs=[
            pl.BlockSpec((tm, d), lambda i, j: (i, 0)),
            pl.BlockSpec((1, d), lambda i, j: (0, 0)),
            pl.BlockSpec((d, tf), lambda i, j: (0, j)),
            pl.BlockSpec((d, tf), lambda i, j: (0, j)),
        ],
        out_specs=pl.BlockSpec((tm, tf), lambda i, j: (i, j)),
        scratch_shapes=[pltpu.VMEM((tm, d), BF16)],
        compiler_params=_params(("arbitrary", "arbitrary")),
        name="norm_gateup",
    )(x, gain.reshape(1, d), wg, wu)


def _mm_res_kernel(a_ref, w_ref, r_ref, o_ref):
    o_ref[...] = r_ref[...] + jnp.dot(a_ref[...], w_ref[...], preferred_element_type=F32)


def matmul_residual(a, w, res):
    t, k = a.shape
    n = w.shape[1]
    tm = t
    for cand in (1024, 512, 256, 128):
        need = k * n * 2 + 2 * cand * k * 2 + 4 * cand * n * 4 + cand * n * 4
        if t % cand == 0 and need <= RES_VMEM_BUDGET:
            tm = cand
            break
    return pl.pallas_call(
        _mm_res_kernel,
        out_shape=jax.ShapeDtypeStruct((t, n), F32),
        grid=(t // tm,),
        in_specs=[
            pl.BlockSpec((tm, k), lambda i: (i, 0)),
            pl.BlockSpec((k, n), lambda i: (0, 0), pipeline_mode=pl.Buffered(1)),
            pl.BlockSpec((tm, n), lambda i: (i, 0)),
        ],
        out_specs=pl.BlockSpec((tm, n), lambda i: (i, 0)),
        compiler_params=_params(("arbitrary",)),
        name="matmul_residual",
    )(a, w, res)


def _ssd_kernel(x_ref, bm_ref, cm_ref, z_ref, dtt_ref, cwx_ref, cbx_ref, cwb_ref, cbb_ref,
                cwc_ref, cbc_ref, dtb_ref, alog_ref, dx_ref, gn_ref, e_ref, sh_ref,
                o_ref, pad_ref, h_ref, *, hpg, gw, nc):
    L = CHUNK
    n = SSM_STATE
    hd = gw // hpg
    rows = nc * L
    c = pl.program_id(2)
    tail = CONV_TAIL

    @pl.when(c == 0)
    def _():
        pad_ref[0:tail, :] = jnp.zeros((tail, gw + 2 * n), BF16)
        h_ref[...] = jnp.zeros_like(h_ref)

    @pl.when(c > 0)
    def _():
        pad_ref[0:tail, :] = pad_ref[rows:rows + tail, :]

    pad_ref[tail:tail + rows, 0:gw] = x_ref[0]
    pad_ref[tail:tail + rows, gw:gw + n] = bm_ref[0]
    pad_ref[tail:tail + rows, gw + n:gw + 2 * n] = cm_ref[0]

    def conv(shifted, t0, col0, width, w_ref, b_ref):
        acc = b_ref[...] + w_ref[CONV_WIDTH - 1:CONV_WIDTH, :] * pad_ref[
            t0 + tail:t0 + tail + L, col0:col0 + width].astype(F32)
        for k in range(CONV_WIDTH - 1):
            acc = acc + w_ref[k:k + 1, :] * shifted[k * L:(k + 1) * L, col0:col0 + width]
        return _silu(acc)

    raw = dtt_ref[0] + dtb_ref[...]
    dt_all = jnp.maximum(raw, 0.0) + jnp.log1p(jnp.exp(-jnp.abs(raw)))
    a_all = dt_all * (-jnp.exp(alog_ref[...]))
    row = lax.broadcasted_iota(jnp.int32, (L, L), 0)
    col = lax.broadcasted_iota(jnp.int32, (L, L), 1)
    upper = (row <= col).astype(F32)
    causal = row >= col
    lane = lax.broadcasted_iota(jnp.int32, (1, 2 * hd), 1)
    lo = lane < hd

    for ci in range(nc):
        t0 = ci * L
        shifted = jnp.dot(sh_ref[...], pad_ref[t0:t0 + tail + L, :],
                          preferred_element_type=F32)
        xs = conv(shifted, t0, 0, gw, cwx_ref, cbx_ref)
        bc = conv(shifted, t0, gw, n, cwb_ref, cbb_ref)
        cc = conv(shifted, t0, gw + n, n, cwc_ref, cbc_ref)

        dt = dt_all[:, t0:t0 + L]
        acs = jnp.dot(a_all[:, t0:t0 + L], upper, preferred_element_type=F32,
                      precision=lax.Precision.HIGHEST)
        a_last = acs[:, L - 1:L]
        dte = jnp.exp(a_last - acs)
        eacs = jnp.exp(acs)

        parts = _split3(acs) + _split3(dt) + _split3(dt * dte) + _split3(eacs)
        stacked = jnp.concatenate(parts, axis=0).T.astype(BF16)
        spread = jnp.dot(stacked, e_ref[...], preferred_element_type=F32)
        acs_col = spread[:, :hpg * L]
        dt_x = spread[:, hpg * L:hpg * L + gw]
        w_x = spread[:, hpg * L + gw:hpg * L + 2 * gw]
        eacs_x = spread[:, hpg * L + 2 * gw:]

        xdt = xs * dt_x
        bct = bc.T.astype(BF16)
        ccb = cc.astype(BF16)
        cb = jnp.dot(ccb, bct, preferred_element_type=F32)

        y_parts = []
        for q in range(hpg // 2):
            ms = []
            for j in (2 * q, 2 * q + 1):
                seg = acs_col[:, j * L:(j + 1) * L] - acs[j:j + 1, :]
                dec = jnp.exp(jnp.where(causal, seg, -jnp.inf))
                ms.append((cb * dec).astype(BF16))
            mcat = jnp.concatenate(ms, axis=1)
            xq = xdt[:, q * 2 * hd:(q + 1) * 2 * hd]
            rhs = jnp.concatenate([jnp.where(lo, xq, 0.0), jnp.where(lo, 0.0, xq)], axis=0)
            y_parts.append(jnp.dot(mcat, rhs.astype(BF16), preferred_element_type=F32))
        y_diag = jnp.concatenate(y_parts, axis=1)

        h_in = h_ref[...]
        y_off = jnp.dot(ccb, h_in.astype(BF16), preferred_element_type=F32) * eacs_x
        states = jnp.dot(bct, (xs * w_x).astype(BF16), preferred_element_type=F32)
        h_ref[...] = h_in * eacs_x[L - 1:L, :] + states

        y = y_diag + y_off + dx_ref[...] * xs
        gated = y * _silu(z_ref[0, t0:t0 + L, :].astype(F32))
        o_ref[0, t0:t0 + L, :] = _rms_rows(gated, gn_ref[...]).astype(o_ref.dtype)


def ssd_core(z, xbc, dtt, conv_w, conv_b, dt_bias, a_log, d_skip, out_norm):
    b, s, d_inner = z.shape
    g = SSM_GROUPS
    n = SSM_STATE
    heads = dt_bias.shape[0]
    hpg = heads // g
    gw = d_inner // g
    hd = gw // hpg
    L = CHUNK
    nb = d_inner // n
    assert gw % LANES == 0 and hpg % 2 == 0 and 2 * hd == LANES and s % L == 0

    eye = jnp.eye(hpg, dtype=F32)
    widths = [hpg * L, gw, gw, gw]
    mats = [jnp.repeat(eye, L, axis=1)] + [jnp.repeat(eye, hd, axis=1)] * 3
    spread_rows = []
    for qty in range(4):
        blk = jnp.concatenate([mats[qty] if c == qty else jnp.zeros((hpg, widths[c]), F32)
                               for c in range(4)], axis=1)
        spread_rows += [blk] * 3
    spread_mat = jnp.concatenate(spread_rows, axis=0).astype(BF16)
    dx = jnp.repeat(d_skip.astype(F32), hd).reshape(1, d_inner)

    nc = SSD_CHUNKS_PER_STEP if (s // L) % SSD_CHUNKS_PER_STEP == 0 else 1
    rows = nc * L
    out_row = jnp.arange((CONV_WIDTH - 1) * L, dtype=jnp.int32)
    src_row = out_row % L + CONV_TAIL - (CONV_WIDTH - 1) + out_row // L
    shift_mat = (src_row[:, None] == jnp.arange(CONV_TAIL + L, dtype=jnp.int32)[None, :]).astype(BF16)
    kern = functools.partial(_ssd_kernel, hpg=hpg, gw=gw, nc=nc)
    return pl.pallas_call(
        kern,
        out_shape=jax.ShapeDtypeStruct((b, s, d_inner), BF16),
        grid=(b, g, s // rows),
        in_specs=[
            pl.BlockSpec((1, rows, gw), lambda i, j, c: (i, c, j)),
            pl.BlockSpec((1, rows, n), lambda i, j, c: (i, c, nb + j)),
            pl.BlockSpec((1, rows, n), lambda i, j, c: (i, c, nb + g + j)),
            pl.BlockSpec((1, rows, gw), lambda i, j, c: (i, c, j)),
            pl.BlockSpec((1, hpg, rows), lambda i, j, c: (i, j, c)),
            pl.BlockSpec((CONV_WIDTH, gw), lambda i, j, c: (0, j)),
            pl.BlockSpec((1, gw), lambda i, j, c: (0, j)),
            pl.BlockSpec((CONV_WIDTH, n), lambda i, j, c: (0, nb + j)),
            pl.BlockSpec((1, n), lambda i, j, c: (0, nb + j)),
            pl.BlockSpec((CONV_WIDTH, n), lambda i, j, c: (0, nb + g + j)),
            pl.BlockSpec((1, n), lambda i, j, c: (0, nb + g + j)),
            pl.BlockSpec((hpg, 1), lambda i, j, c: (j, 0)),
            pl.BlockSpec((hpg, 1), lambda i, j, c: (j, 0)),
            pl.BlockSpec((1, gw), lambda i, j, c: (0, j)),
            pl.BlockSpec((1, gw), lambda i, j, c: (0, j)),
            pl.BlockSpec((12 * hpg, hpg * L + 3 * gw), lambda i, j, c: (0, 0)),
            pl.BlockSpec(((CONV_WIDTH - 1) * L, CONV_TAIL + L), lambda i, j, c: (0, 0)),
        ],
        out_specs=pl.BlockSpec((1, rows, gw), lambda i, j, c: (i, c, j)),
        scratch_shapes=[pltpu.VMEM((rows + CONV_TAIL, gw + 2 * n), BF16), pltpu.VMEM((n, gw), F32)],
        compiler_params=_params(("arbitrary", "arbitrary", "arbitrary")),
        name="ssd_core",
    )(xbc, xbc, xbc, z, dtt, conv_w, conv_b.reshape(1, -1), conv_w, conv_b.reshape(1, -1),
      conv_w, conv_b.reshape(1, -1), dt_bias.reshape(heads, 1), a_log.reshape(heads, 1),
      dx, out_norm.reshape(1, d_inner), spread_mat, shift_mat)


def _attn_kernel(q_ref, k_ref, v_ref, qg_ref, kg_ref, lq1_ref, lk1_ref, lq2_ref, lk2_ref,
                 sg_ref, avg_ref, o_ref, kn_ref, vt_ref, m_ref, acc_ref, s_ref, p_ref, a_ref,
                 *, tq, t, nsub, lambda_init):
    s_len = k_ref.shape[1]
    n_tiles = s_len // t
    hd = ATT_HEAD_DIM
    step = pl.program_id(2)
    lane = lax.broadcasted_iota(jnp.int32, (1, 2 * hd), 1)
    lo = lane < hd

    def comp_norm(x, gain):
        x2 = x * x
        hi = x2.astype(BF16)
        lo_part = (x2 - hi.astype(F32)).astype(BF16)
        ms = jnp.dot(jnp.concatenate([hi, lo_part], axis=1), avg_ref[...],
                     preferred_element_type=F32)
        return x * lax.rsqrt(ms + EPS) * gain

    @pl.when(step == 0)
    def _():
        def body(i, carry):
            r0 = pl.multiple_of(i * t, t)
            kn = comp_norm(k_ref[0, pl.ds(r0, t), :].astype(F32), kg_ref[...])
            kn_ref[pl.ds(r0, t), :] = kn.astype(BF16)
            vt_ref[i, 0:ATT_V_DIM, :] = v_ref[0, pl.ds(r0, t), :].astype(F32).T.astype(BF16)
            vt_ref[i, ATT_V_DIM:, :] = jnp.ones((ATT_SUM_ROWS, t), BF16)
            return carry

        lax.fori_loop(0, s_len // t, body, 0)

    lam = (jnp.exp(jnp.sum(lq1_ref[...] * lk1_ref[...], axis=-1, keepdims=True))
           - jnp.exp(jnp.sum(lq2_ref[...] * lk2_ref[...], axis=-1, keepdims=True)) + lambda_init)

    whole_head = nsub * tq == s_len
    for sub in range(nsub):
        _attn_query_tile(sub if whole_head else step * nsub + sub, q_ref, sub * tq, o_ref, qg_ref, sg_ref, kn_ref, vt_ref,
                         m_ref.at[sub], acc_ref.at[sub], s_ref.at[sub], p_ref.at[sub],
                         a_ref.at[sub], comp_norm, lo, lam, tq=tq, t=t, n_tiles=n_tiles,
                         lambda_init=lambda_init)


def _attn_query_tile(qi, q_ref, q0, o_ref, qg_ref, sg_ref, kn_ref, vt_ref, m_ref, acc_ref, s_ref,
                     p_ref, a_ref, comp_norm, lo, lam, *, tq, t, n_tiles, lambda_init):
    hd = ATT_HEAD_DIM
    qn = comp_norm(q_ref[0, q0:q0 + tq, :].astype(F32), qg_ref[...]) * (hd ** -0.5 * LOG2E)
    qc = (jnp.where(lo, qn, 0.0).astype(BF16), jnp.where(lo, 0.0, qn).astype(BF16))

    m_ref[...] = jnp.full(m_ref.shape, -jnp.inf, F32)
    acc_ref[...] = jnp.zeros(acc_ref.shape, F32)

    def scores(tile, slot):
        r0 = pl.multiple_of(jnp.minimum(tile, n_tiles - 1) * t, t)
        kk = kn_ref[pl.ds(r0, t), :]
        for c in range(2):
            s_ref[slot, c] = lax.dot_general(kk, qc[c], (((1,), (1,)), ((), ())),
                                             preferred_element_type=F32)

    def softmax(tile, slot, masked):
        for c in range(2):
            sc = s_ref[slot, c]
            if masked:
                key = tile * t + lax.broadcasted_iota(jnp.int32, (t, tq), 0)
                qry = qi * tq + lax.broadcasted_iota(jnp.int32, (t, tq), 1)
                sc = jnp.where(key <= qry, sc, -jnp.inf)
            m_prev = m_ref[c]
            m_new = jnp.maximum(m_prev, jnp.max(sc, axis=0, keepdims=True))
            alpha = jnp.exp2(m_prev - m_new)
            p = jnp.exp2(sc - m_new)
            m_ref[c] = m_new
            a_ref[slot, c] = alpha
            p_ref[slot, c] = p.astype(BF16)

    def update(tile0, tile1):
        vt0 = vt_ref[jnp.minimum(tile0, n_tiles - 1)]
        vt1 = vt_ref[jnp.minimum(tile1, n_tiles - 1)]
        for c in range(2):
            a1 = a_ref[1, c]
            pv0 = jnp.dot(vt0, p_ref[0, c], preferred_element_type=F32)
            pv1 = jnp.dot(vt1, p_ref[1, c], preferred_element_type=F32)
            acc_ref[c] = (a_ref[0, c] * a1) * acc_ref[c] + a1 * pv0 + pv1

    d0 = 2 * ((qi * tq) // (2 * t))
    n_pairs = d0 // 2 + 1
    scores(d0, 0)
    scores(d0 + 1, 1)
    softmax(d0, 0, True)
    scores(0, 0)
    softmax(d0 + 1, 1, True)

    def pair(u, carry):
        x = 2 * u - 2
        first = u == 1
        update(jnp.where(first, d0, x - 2), jnp.where(first, d0 + 1, x - 1))
        scores(x + 1, 1)
        softmax(x, 0, False)
        scores(x + 2, 0)
        softmax(x + 1, 1, False)
        return carry

    if isinstance(qi, int):
        for u in range(1, n_pairs):
            pair(u, 0)
    else:
        lax.fori_loop(1, n_pairs, pair, 0)
    only = n_pairs == 1
    update(jnp.where(only, d0, d0 - 2), jnp.where(only, d0 + 1, d0 - 1))

    e = ATT_V_DIM
    o = (acc_ref[0, 0:e, :] / acc_ref[0, e:e + 1, :]
         - lam * (acc_ref[1, 0:e, :] / acc_ref[1, e:e + 1, :]))
    o = o * lax.rsqrt(jnp.mean(o * o, axis=0, keepdims=True) + EPS) * sg_ref[...]
    o_ref[0, q0:q0 + tq, :] = (o * (1.0 - lambda_init)).T.astype(o_ref.dtype)


def diff_attention_core(qkv, q_norm, k_norm, lq1, lk1, lq2, lk2, sub_norm, lambda_init):
    b, s, w3 = qkv.shape
    width = w3 // 3
    heads = width // ATT_V_DIM
    hd = ATT_HEAD_DIM
    tq = _tile(s, TQ_ATT, LANES)
    t = _tile(s, TK_ATT, LANES)
    assert tq % t == 0 and (2 * t) % tq == 0
    vec = lambda a: a.astype(F32).reshape(1, -1)
    qg = jnp.tile(q_norm.astype(F32), 2).reshape(1, 2 * hd)
    kg = jnp.tile(k_norm.astype(F32), 2).reshape(1, 2 * hd)
    small = lambda width_: pl.BlockSpec((1, width_), lambda i, h, j: (0, 0))
    comp = jnp.arange(2 * hd, dtype=jnp.int32) // hd
    avg = jnp.tile((comp[:, None] == comp[None, :]).astype(F32) / hd, (2, 1)).astype(BF16)
    nsub = ATT_Q_TILES_PER_STEP if (s // tq) % ATT_Q_TILES_PER_STEP == 0 else 1
    rows = nsub * tq
    kern = functools.partial(_attn_kernel, tq=tq, t=t, nsub=nsub, lambda_init=lambda_init)
    return pl.pallas_call(
        kern,
        out_shape=jax.ShapeDtypeStruct((b, s, width), BF16),
        grid=(b, heads, s // rows),
        in_specs=[
            pl.BlockSpec((1, rows, 2 * hd), lambda i, h, j: (i, j, h)),
            pl.BlockSpec((1, s, 2 * hd), lambda i, h, j: (i, 0, heads + h)),
            pl.BlockSpec((1, s, ATT_V_DIM), lambda i, h, j: (i, 0, 2 * heads + h)),
            small(2 * hd), small(2 * hd), small(hd), small(hd), small(hd), small(hd),
            pl.BlockSpec((ATT_V_DIM, 1), lambda i, h, j: (0, 0)),
            pl.BlockSpec((4 * hd, 2 * hd), lambda i, h, j: (0, 0)),
        ],
        out_specs=pl.BlockSpec((1, rows, ATT_V_DIM), lambda i, h, j: (i, j, h)),
        scratch_shapes=[
            pltpu.VMEM((s, 2 * hd), BF16),
            pltpu.VMEM((s // t, ATT_V_DIM + ATT_SUM_ROWS, t), BF16),
            pltpu.VMEM((nsub, 2, 1, tq), F32),
            pltpu.VMEM((nsub, 2, ATT_V_DIM + ATT_SUM_ROWS, tq), F32),
            pltpu.VMEM((nsub, 2, 2, t, tq), F32), pltpu.VMEM((nsub, 2, 2, t, tq), BF16),
            pltpu.VMEM((nsub, 2, 2, 1, tq), F32),
        ],
        compiler_params=_params(("arbitrary", "arbitrary", "arbitrary")),
        name="diff_attention",
    )(qkv, qkv, qkv, qg, kg, vec(lq1), vec(lk1), vec(lq2), vec(lk2),
      sub_norm.astype(F32).reshape(ATT_V_DIM, 1), avg)


def _route_kernel(x_ref, g_ref, r_ref, xn_ref, info_ref, cnt_ref, run_ref, *, n_experts):
    @pl.when(pl.program_id(0) == 0)
    def _():
        run_ref[...] = jnp.zeros_like(run_ref)

    xn = _rms_rows(x_ref[...], g_ref[...])
    xn_ref[...] = xn.astype(BF16)
    xh = xn.astype(BF16)
    xl = (xn - xh.astype(F32)).astype(BF16)
    hi = jnp.dot(xh, r_ref[...], preferred_element_type=F32)
    lo_ = jnp.dot(xl, r_ref[...], preferred_element_type=F32)
    logits = (hi[:, :LANES] + hi[:, LANES:]) + (lo_[:, :LANES] + lo_[:, LANES:])
    lane = lax.broadcasted_iota(jnp.int32, logits.shape, 1)
    logits = jnp.where(lane < n_experts, logits, -jnp.inf)
    m1 = jnp.max(logits, axis=-1, keepdims=True)
    i1 = jnp.min(jnp.where(logits == m1, lane, LANES), axis=-1, keepdims=True)
    rest = jnp.where(lane == i1, -jnp.inf, logits)
    m2 = jnp.max(rest, axis=-1, keepdims=True)
    i2 = jnp.min(jnp.where(rest == m2, lane, LANES), axis=-1, keepdims=True)
    e2 = jnp.exp(m2 - m1)
    g1 = 1.0 / (1.0 + e2)
    g2 = e2 * g1
    tm = logits.shape[0]
    oh1 = (lane == i1).astype(F32)
    oh2 = (lane == i2).astype(F32)
    both = oh1 + oh2
    row = lax.broadcasted_iota(jnp.int32, (tm, tm), 0)
    col = lax.broadcasted_iota(jnp.int32, (tm, tm), 1)
    before = jnp.dot((col < row).astype(BF16), both.astype(BF16),
                     preferred_element_type=F32) + run_ref[0:1, :]
    r1 = jnp.sum(before * oh1, axis=-1, keepdims=True)
    r2 = jnp.sum(before * oh2, axis=-1, keepdims=True)
    run_ref[...] = run_ref[...] + jnp.sum(both, axis=0, keepdims=True)
    cnt_ref[...] = run_ref[...]
    info = jnp.zeros_like(logits)
    for k, val in enumerate((i1.astype(F32), i2.astype(F32), g1, g2, r1, r2)):
        info = jnp.where(lane == k, val, info)
    info_ref[...] = info


def moe_route(h, gain, router):
    t, d = h.shape
    e = router.shape[1]
    tm = _tile(t, TM_ROUTE, SUBLANES)
    rpad = jnp.zeros((d, LANES), F32).at[:, :e].set(router.astype(F32))
    r_hi = rpad.astype(BF16)
    r_parts = jnp.concatenate([r_hi, (rpad - r_hi.astype(F32)).astype(BF16)], axis=1)
    return pl.pallas_call(
        functools.partial(_route_kernel, n_experts=e),
        out_shape=(jax.ShapeDtypeStruct((t, d), BF16), jax.ShapeDtypeStruct((t, LANES), F32),
                   jax.ShapeDtypeStruct((SUBLANES, LANES), F32)),
        grid=(t // tm,),
        in_specs=[
            pl.BlockSpec((tm, d), lambda i: (i, 0)),
            pl.BlockSpec((1, d), lambda i: (0, 0)),
            pl.BlockSpec((d, 2 * LANES), lambda i: (0, 0)),
        ],
        out_specs=(pl.BlockSpec((tm, d), lambda i: (i, 0)),
                   pl.BlockSpec((tm, LANES), lambda i: (i, 0)),
                   pl.BlockSpec((SUBLANES, LANES), lambda i: (0, 0))),
        scratch_shapes=[pltpu.VMEM((SUBLANES, LANES), F32)],
        compiler_params=_params(("arbitrary",)),
        name="moe_route",
    )(h, gain.reshape(1, d), r_parts)


def _new_weights(te_ref, i):
    prev = te_ref[jnp.maximum(i - 1, 0)]
    return jnp.logical_or(i == 0, te_ref[i] != prev)


def _moe_gateup_kernel(te_ref, nu_ref, x_ref, wg_ref, wu_ref, o_ref, wgb_ref, wub_ref):
    i = pl.program_id(1)

    @pl.when(_new_weights(te_ref, i))
    def _():
        wgb_ref[...] = wg_ref[0].astype(BF16)
        wub_ref[...] = wu_ref[0].astype(BF16)

    @pl.when(i < nu_ref[0])
    def _():
        x = x_ref[...]
        a = jnp.dot(x, wgb_ref[...], preferred_element_type=F32)
        b = jnp.dot(x, wub_ref[...], preferred_element_type=F32)
        o_ref[...] = (_silu(a) * b).astype(o_ref.dtype)

    @pl.when(i >= nu_ref[0])
    def _():
        o_ref[...] = jnp.zeros_like(o_ref)


def _moe_down_kernel(te_ref, nu_ref, h_ref, wd_ref, o_ref, wdb_ref):
    i = pl.program_id(1)

    @pl.when(_new_weights(te_ref, i))
    def _():
        wdb_ref[...] = wd_ref[0].astype(BF16)

    @pl.when(i < nu_ref[0])
    def _():
        o_ref[...] = jnp.dot(h_ref[...], wdb_ref[...],
                             preferred_element_type=F32).astype(o_ref.dtype)

    @pl.when(i >= nu_ref[0])
    def _():
        o_ref[...] = jnp.zeros_like(o_ref)


def moe_experts(xs, tile_expert, n_used, w_gate, w_up, w_down, tm):
    slots, d = xs.shape
    e, _, f = w_gate.shape
    n_tiles = slots // tm
    tf = _tile(f, TF_FFN)
    hmid = pl.pallas_call(
        _moe_gateup_kernel,
        out_shape=jax.ShapeDtypeStruct((slots, f), BF16),
        grid_spec=pltpu.PrefetchScalarGridSpec(
            num_scalar_prefetch=2,
            grid=(f // tf, n_tiles),
            in_specs=[
                pl.BlockSpec((tm, d), lambda j, i, te, nu: (i, 0)),
                pl.BlockSpec((1, d, tf), lambda j, i, te, nu: (te[i], 0, j)),
                pl.BlockSpec((1, d, tf), lambda j, i, te, nu: (te[i], 0, j)),
            ],
            out_specs=pl.BlockSpec((tm, tf), lambda j, i, te, nu: (i, j)),
            scratch_shapes=[pltpu.VMEM((d, tf), BF16), pltpu.VMEM((d, tf), BF16)],
        ),
        compiler_params=_params(("arbitrary", "arbitrary")),
        name="moe_gateup",
    )(tile_expert, n_used, xs, w_gate, w_up)

    tn = _tile(d, TN_MOE_DOWN)
    return pl.pallas_call(
        _moe_down_kernel,
        out_shape=jax.ShapeDtypeStruct((slots, d), BF16),
        grid_spec=pltpu.PrefetchScalarGridSpec(
            num_scalar_prefetch=2,
            grid=(d // tn, n_tiles),
            in_specs=[
                pl.BlockSpec((tm, f), lambda j, i, te, nu: (i, 0)),
                pl.BlockSpec((1, f, tn), lambda j, i, te, nu: (te[i], 0, j),
                             pipeline_mode=pl.Buffered(1)),
            ],
            out_specs=pl.BlockSpec((tm, tn), lambda j, i, te, nu: (i, j)),
            scratch_shapes=[pltpu.VMEM((f, tn), BF16)],
        ),
        compiler_params=_params(("arbitrary", "arbitrary")),
        name="moe_down",
    )(tile_expert, n_used, hmid, w_down)


def moe_block(h, gain, router, w_gate, w_up, w_down):
    t, d = h.shape
    e = router.shape[1]
    tm = min(TM_MOE, t)
    xn, info, cnt = moe_route(h, gain, router)
    idx = info[:, :TOP_K].astype(jnp.int32)
    gates = info[:, TOP_K:2 * TOP_K]
    rank = info[:, 2 * TOP_K:3 * TOP_K].astype(jnp.int32)
    counts = cnt[0, :e].astype(jnp.int32)

    tiles_per_e = (counts + tm - 1) // tm
    tile_end = jnp.cumsum(tiles_per_e)
    starts = (tile_end - tiles_per_e) * tm
    slot = (jnp.take(starts, idx.reshape(-1)) + rank.reshape(-1))
    n_tiles = (TOP_K * t) // tm + e
    n_slots = n_tiles * tm
    token_of_slot = jnp.zeros((n_slots,), jnp.int32).at[slot].set(
        jnp.arange(TOP_K * t, dtype=jnp.int32) // TOP_K)
    tile_ids = jnp.arange(n_tiles, dtype=jnp.int32)
    tile_expert = jnp.minimum(jnp.sum((tile_ids[:, None] >= tile_end[None, :]).astype(jnp.int32), axis=1),
                              e - 1)
    n_used = tile_end[-1:].astype(jnp.int32)

    xs = xn.at[token_of_slot].get(mode="promise_in_bounds")
    ys = moe_experts(xs, tile_expert, n_used, w_gate, w_up, w_down, tm)
    slot2 = slot.reshape(t, TOP_K)
    out = h
    for k in range(TOP_K):
        out = out + gates[:, k:k + 1] * ys.at[slot2[:, k]].get(mode="promise_in_bounds").astype(F32)
    return out


def mamba_layer(h, b, s, pre_norm, in_proj, conv_w, conv_b, dt_bias, a_log, d_skip, out_norm,
                out_proj):
    d_inner = out_proj.shape[0]
    d_xbc = conv_w.shape[1]
    heads = dt_bias.shape[0]
    z = norm_matmul(h, pre_norm, in_proj[:, :d_inner].astype(BF16), BF16)
    xbc = norm_matmul(h, pre_norm, in_proj[:, d_inner:d_inner + d_xbc].astype(BF16), BF16)
    w_dt = jnp.zeros((h.shape[1], LANES), BF16).at[:, :heads].set(
        in_proj[:, d_inner + d_xbc:].astype(BF16))
    dt = norm_matmul(h, pre_norm, w_dt, F32)[:, :heads]
    dtt = jnp.swapaxes(dt.reshape(b, s, heads), 1, 2)
    y = ssd_core(z.reshape(b, s, d_inner), xbc.reshape(b, s, d_xbc), dtt, conv_w.astype(F32),
                 conv_b.astype(F32), dt_bias.astype(F32), a_log.astype(F32), d_skip, out_norm)
    return matmul_residual(y.reshape(b * s, d_inner), out_proj.astype(BF16), h)


def dense_layer(h, pre_norm, w_gate, w_up, w_down):
    mid = norm_gateup(h, pre_norm, w_gate.astype(BF16), w_up.astype(BF16))
    return matmul_residual(mid, w_down.astype(BF16), h)


def attention_layer(h, b, s, pre_norm, w_qkv, q_norm, k_norm, lq1, lk1, lq2, lk2, sub_norm, w_o,
                    lambda_init):
    qkv = norm_matmul(h, pre_norm, w_qkv.astype(BF16), BF16)
    o = diff_attention_core(qkv.reshape(b, s, -1), q_norm, k_norm, lq1, lk1, lq2, lk2, sub_norm,
                            lambda_init)
    return matmul_residual(o.reshape(b * s, -1), w_o.astype(BF16), h)


def kernel(x, ssm_pre_norm, ssm_in_proj, ssm_conv_w, ssm_conv_b, ssm_dt_bias, ssm_A_log, ssm_D,
           ssm_out_norm, ssm_out_proj, dense_pre_norm, dense_w_gate, dense_w_up, dense_w_down,
           att_pre_norm, att_w_qkv, att_q_norm, att_k_norm, att_lambda_q1, att_lambda_k1,
           att_lambda_q2, att_lambda_k2, att_sub_norm, att_w_o, moe_pre_norm, moe_router,
           moe_w_gate, moe_w_up, moe_w_down):
    b, s, d = x.shape
    h = x.reshape(b * s, d)
    for i in range(DEPTH):
        j = i // N_MIXERS
        if i % N_MIXERS == 0:
            h = mamba_layer(h, b, s, ssm_pre_norm[j], ssm_in_proj[j], ssm_conv_w[j],
                            ssm_conv_b[j], ssm_dt_bias[j], ssm_A_log[j], ssm_D[j],
                            ssm_out_norm[j], ssm_out_proj[j])
            h = dense_layer(h, dense_pre_norm[j], dense_w_gate[j], dense_w_up[j],
                            dense_w_down[j])
        else:
            lambda_init = 0.8 - 0.6 * math.exp(-0.3 * i)
            h = attention_layer(h, b, s, att_pre_norm[j], att_w_qkv[j], att_q_norm[j],
                                att_k_norm[j], att_lambda_q1[j], att_lambda_k1[j],
                                att_lambda_q2[j], att_lambda_k2[j], att_sub_norm[j], att_w_o[j],
                                lambda_init)
            h = moe_block(h, moe_pre_norm[j], moe_router[j], moe_w_gate[j], moe_w_up[j],
                          moe_w_down[j])
    return h.reshape(b, s, d)
```
